```python
import math
import jax
import jax.numpy as jnp
from jax import lax
import numpy as np

D_MODEL = 1024
BATCH = 8
SEQ = 4096
DEPTH = 4

N_MIXERS = 4
HEAD_DIM = 64
ROPE_THETA = 500000.0
ROT_DIM = HEAD_DIM // 4
Q_BLOCK = 128

A_HEADS = 16
A_KV_HEADS = 4
A_WINDOW = 128
B_HEADS = 16
B_Q_RANK = 384
B_KV_RANK = 256
B_NOPE = 64
B_ROPE = 32
B_V = 64
C_HEADS = 16
C_KV_HEADS = 4
C_CMP_LEN = 32
C_CMP_STRIDE = 16
C_CMP_HIDDEN = 128
C_SEL_LEN = 64
C_N_SEL = 16
C_WINDOW = 512
D_HEADS = 8
D_SUB = 64
M_GROUPS = 4
M_PER_GROUP = 8
M_EXPERTS = M_GROUPS * M_PER_GROUP
M_TOPK = 2
M_HIDDEN = 512

LN_EPS = 1e-5
RMS_EPS = 1e-6

kernel_name = 'hybrid_interleaved_deepnorm_hmoe'


def layer_norm(x, g, b):
    xf = x.astype(jnp.float32)
    mu = jnp.mean(xf, -1, keepdims=True)
    var = jnp.mean(jnp.square(xf - mu), -1, keepdims=True)
    return ((xf - mu) * lax.rsqrt(var + LN_EPS) * g + b).astype(x.dtype)


def rms_norm(x, g):
    xf = x.astype(jnp.float32)
    return (xf * lax.rsqrt(jnp.mean(jnp.square(xf), -1, keepdims=True) + RMS_EPS) * g).astype(x.dtype)


def rope(x, rot_dim):
    s = x.shape[1]
    half = rot_dim // 2
    inv_freq = 1.0 / (ROPE_THETA ** (jnp.arange(half, dtype=jnp.float32) * (2.0 / rot_dim)))
    ang = jnp.arange(s, dtype=jnp.float32)[:, None] * inv_freq[None, :]
    cos = jnp.cos(ang)[None, :, None, :]
    sin = jnp.sin(ang)[None, :, None, :]
    xr = x[..., :rot_dim].astype(jnp.float32)
    x1, x2 = xr[..., :half], xr[..., half:]
    rot = jnp.concatenate([x1 * cos - x2 * sin, x2 * cos + x1 * sin], -1).astype(x.dtype)
    return jnp.concatenate([rot, x[..., rot_dim:]], -1)


def masked_softmax(s, mask):
    s = jnp.where(mask, s, -jnp.inf)
    m = jnp.max(s, -1, keepdims=True)
    m = jnp.where(jnp.isfinite(m), m, 0.0)
    e = jnp.where(mask, jnp.exp(s - m), 0.0)
    return e / jnp.maximum(jnp.sum(e, -1, keepdims=True), 1e-30)


def causal_attention(q, k, v, scale):
    b, s, h, dq = q.shape
    nb = s // Q_BLOCK
    qb = q.reshape(b, nb, Q_BLOCK, h, dq).transpose(1, 0, 2, 3, 4)
    kpos = jnp.arange(s)

    def block(args):
        qi, blk = args
        sc = jnp.einsum('bqhd,bkhd->bhqk', qi, k).astype(jnp.float32) * scale
        qpos = blk * Q_BLOCK + jnp.arange(Q_BLOCK)
        sc = jnp.where(kpos[None, :] <= qpos[:, None], sc, -jnp.inf)
        p = jax.nn.softmax(sc, axis=-1)
        return jnp.einsum('bhqk,bkhd->bqhd', p.astype(v.dtype), v)

    o = lax.map(block, (qb, jnp.arange(nb)))
    return o.transpose(1, 0, 2, 3, 4).reshape(b, s, h, v.shape[-1])


def swa_sink_gqa(x, w_in, sinks, w_o):
    b, s, _ = x.shape
    g = A_HEADS // A_KV_HEADS
    blk = A_WINDOW
    nb = s // blk
    qd, kd = A_HEADS * HEAD_DIM, A_KV_HEADS * HEAD_DIM
    qkv = x @ w_in
    q = rope(qkv[..., :qd].reshape(b, s, A_HEADS, HEAD_DIM), ROT_DIM)
    k = rope(qkv[..., qd:qd + kd].reshape(b, s, A_KV_HEADS, HEAD_DIM), ROT_DIM)
    v = qkv[..., qd + kd:].reshape(b, s, A_KV_HEADS, HEAD_DIM)
    qb = q.reshape(b, nb, blk, A_KV_HEADS, g, HEAD_DIM)

    def band(t):
        tb = t.reshape(b, nb, blk, A_KV_HEADS, HEAD_DIM)
        prev = jnp.pad(tb, ((0, 0), (1, 0), (0, 0), (0, 0), (0, 0)))[:, :-1]
        return jnp.concatenate([prev, tb], axis=2)

    kb, vb = band(k), band(v)
    sc = jnp.einsum('bnqhgd,bnkhd->bnhgqk', qb, kb).astype(jnp.float32) * HEAD_DIM ** -0.5
    n = jnp.arange(nb)[:, None, None]
    i = jnp.arange(blk)[None, :, None]
    j = jnp.arange(2 * blk)[None, None, :]
    dist = blk + i - j
    mask = (dist >= 0) & (dist < A_WINDOW) & ((n - 1) * blk + j >= 0)
    sc = jnp.where(mask[None, :, None, None], sc, -jnp.inf)
    sink = sinks.astype(jnp.float32).reshape(A_KV_HEADS, g)[None, None, :, :, None, None]
    m = jnp.maximum(jnp.max(sc, -1, keepdims=True), sink)
    e = jnp.exp(sc - m)
    p = e / (jnp.sum(e, -1, keepdims=True) + jnp.exp(sink - m))
    o = jnp.einsum('bnhgqk,bnkhd->bnqhgd', p.astype(vb.dtype), vb)
    return o.reshape(b, s, qd) @ w_o


def mla(x, w_down, q_norm, kv_norm, w_uq, w_ukv, w_o):
    b, s, _ = x.shape
    c = x @ w_down
    cq = rms_norm(c[..., :B_Q_RANK], q_norm)
    ckv = rms_norm(c[..., B_Q_RANK:B_Q_RANK + B_KV_RANK], kv_norm)
    kr = c[..., B_Q_RANK + B_KV_RANK:]
    q = (cq @ w_uq).reshape(b, s, B_HEADS, B_NOPE + B_ROPE)
    q = jnp.concatenate([q[..., :B_NOPE], rope(q[..., B_NOPE:], B_ROPE)], -1)
    kv = (ckv @ w_ukv).reshape(b, s, B_HEADS, B_NOPE + B_V)
    k_rope = rope(kr[:, :, None, :], B_ROPE)
    k = jnp.concatenate([kv[..., :B_NOPE], jnp.broadcast_to(k_rope, (b, s, B_HEADS, B_ROPE))], -1)
    o = causal_attention(q, k, kv[..., B_NOPE:], (B_NOPE + B_ROPE) ** -0.5)
    return o.reshape(b, s, B_HEADS * B_V) @ w_o


def nsa(x, w_in, pos_k, pos_v, wk1, wk2, wv1, wv2, w_o):
    b, s, _ = x.shape
    g = C_HEADS // C_KV_HEADS
    qd, kd = C_HEADS * HEAD_DIM, C_KV_HEADS * HEAD_DIM
    h = x @ w_in
    q = rope(h[..., :qd].reshape(b, s, C_HEADS, HEAD_DIM), ROT_DIM)
    kv = h[..., qd:qd + 6 * kd].reshape(b, s, 6, C_KV_HEADS, HEAD_DIM)
    k_c, v_c = kv[:, :, 0], kv[:, :, 1]
    k_s, v_s = rope(kv[:, :, 2], ROT_DIM), kv[:, :, 3]
    k_w, v_w = rope(kv[:, :, 4], ROT_DIM), kv[:, :, 5]
    gates = jax.nn.sigmoid(h[..., qd + 6 * kd:].astype(jnp.float32)).reshape(b, s, C_HEADS, 3)

    nc = (s - C_CMP_LEN) // C_CMP_STRIDE + 1
    idx = jnp.arange(nc)[:, None] * C_CMP_STRIDE + jnp.arange(C_CMP_LEN)[None, :]

    def compress(t, pe, w1, w2):
        blocks = t[:, idx] + pe[None, None, :, None, :]
        flat = blocks.transpose(0, 1, 3, 2, 4).reshape(b, nc, C_KV_HEADS, C_CMP_LEN * HEAD_DIM)
        return jax.nn.gelu(flat @ w1) @ w2

    k_cmp = compress(k_c, pos_k, wk1, wk2)
    v_cmp = compress(v_c, pos_v, wv1, wv2)
    cmp_start = jnp.arange(nc) * C_CMP_STRIDE
    cmp_end = cmp_start + C_CMP_LEN - 1

    nsb = s // C_SEL_LEN
    n_sel = min(C_N_SEL, nsb)
    sel_start = jnp.arange(nsb) * C_SEL_LEN
    overlap = ((cmp_start[:, None] <= sel_start[None, :] + C_SEL_LEN - 1)
               & (sel_start[None, :] <= cmp_end[:, None])).astype(jnp.float32)
    ks_blk = k_s.reshape(b, nsb, C_SEL_LEN, C_KV_HEADS, HEAD_DIM).transpose(0, 3, 1, 2, 4)
    vs_blk = v_s.reshape(b, nsb, C_SEL_LEN, C_KV_HEADS, HEAD_DIM).transpose(0, 3, 1, 2, 4)
    kw_pad = jnp.pad(k_w, ((0, 0), (C_WINDOW, 0), (0, 0), (0, 0)))
    vw_pad = jnp.pad(v_w, ((0, 0), (C_WINDOW, 0), (0, 0), (0, 0)))

    nqb = s // Q_BLOCK
    qf = q.reshape(b * nqb, Q_BLOCK, C_KV_HEADS, g, HEAD_DIM)
    b_ids = jnp.repeat(jnp.arange(b), nqb)
    blk_ids = jnp.tile(jnp.arange(nqb), b)
    scale = HEAD_DIM ** -0.5
    heads = jnp.arange(C_KV_HEADS)[None, :, None]

    def block(args):
        qi, bi, blk = args
        t = blk * Q_BLOCK + jnp.arange(Q_BLOCK)
        sc = jnp.einsum('qhgd,nhd->qhgn', qi, k_cmp[bi]).astype(jnp.float32) * scale
        p_cmp = masked_softmax(sc, (cmp_end[None, :] <= t[:, None])[:, None, None, :])
        o_cmp = jnp.einsum('qhgn,nhd->qhgd', p_cmp.astype(v_cmp.dtype), v_cmp[bi])
        imp = jnp.einsum('qhgn,nj->qhj', p_cmp, overlap)
        cur = (t // C_SEL_LEN)[:, None, None]
        jb = jnp.arange(nsb)[None, None, :]
        imp = jnp.where((jb == 0) | (jb == cur) | (jb == cur - 1), jnp.inf, imp)
        imp = jnp.where(jb <= cur, imp, -jnp.inf)
        _, sel = lax.top_k(imp, n_sel)
        kg = ks_blk[bi][heads, sel]
        vg = vs_blk[bi][heads, sel].reshape(Q_BLOCK, C_KV_HEADS, n_sel * C_SEL_LEN, HEAD_DIM)
        sc = jnp.einsum('qhgd,qhnld->qhgnl', qi, kg).astype(jnp.float32)
        sc = sc.reshape(Q_BLOCK, C_KV_HEADS, g, n_sel * C_SEL_LEN) * scale
        kpos = (sel[..., None] * C_SEL_LEN + jnp.arange(C_SEL_LEN)).reshape(
            Q_BLOCK, C_KV_HEADS, 1, n_sel * C_SEL_LEN)
        p_slc = masked_softmax(sc, kpos <= t[:, None, None, None])
        o_slc = jnp.einsum('qhgk,qhkd->qhgd', p_slc.astype(vg.dtype), vg)
        kwb = lax.dynamic_slice_in_dim(kw_pad[bi], blk * Q_BLOCK, Q_BLOCK + C_WINDOW, axis=0)
        vwb = lax.dynamic_slice_in_dim(vw_pad[bi], blk * Q_BLOCK, Q_BLOCK + C_WINDOW, axis=0)
        wpos = blk * Q_BLOCK - C_WINDOW + jnp.arange(Q_BLOCK + C_WINDOW)
        dist = t[:, None] - wpos[None, :]
        wmask = (dist >= 0) & (dist < C_WINDOW) & (wpos[None, :] >= 0)
        sc = jnp.einsum('qhgd,khd->qhgk', qi, kwb).astype(jnp.float32) * scale
        p_win = masked_softmax(sc, wmask[:, None, None, :])
        o_win = jnp.einsum('qhgk,khd->qhgd', p_win.astype(vwb.dtype), vwb)
        return jnp.stack([o_cmp, o_slc, o_win], axis=-1)

    o = lax.map(block, (qf, b_ids, blk_ids)).reshape(b, s, C_HEADS, HEAD_DIM, 3)
    o = jnp.sum(o * gates[:, :, :, None, :].astype(o.dtype), axis=-1)
    return o.reshape(b, s, qd) @ w_o


def diff_attention(x, w_in, lq1, lk1, lq2, lk2, subln, w_o, layer_idx):
    b, s, _ = x.shape
    qd = D_HEADS * 2 * D_SUB
    h = x @ w_in
    q = rope(h[..., :qd].reshape(b, s, 2 * D_HEADS, D_SUB), ROT_DIM).reshape(b, s, D_HEADS, 2, D_SUB)
    k = rope(h[..., qd:2 * qd].reshape(b, s, 2 * D_HEADS, D_SUB), ROT_DIM).reshape(b, s, D_HEADS, 2, D_SUB)
    v = h[..., 2 * qd:].reshape(b, s, D_HEADS, 2 * D_SUB)
    lam_init = 0.8 - 0.6 * math.exp(-0.3 * layer_idx)
    lam = (jnp.exp(jnp.sum(lq1.astype(jnp.float32) * lk1.astype(jnp.float32)))
           - jnp.exp(jnp.sum(lq2.astype(jnp.float32) * lk2.astype(jnp.float32))) + lam_init)
    nb = s // Q_BLOCK
    qb = q.reshape(b, nb, Q_BLOCK, D_HEADS, 2, D_SUB).transpose(1, 0, 2, 3, 4, 5)
    kpos = jnp.arange(s)
    scale = D_SUB ** -0.5

    def block(args):
        qi, blk = args
        sc = jnp.einsum('bqhcd,bkhcd->cbhqk', qi, k).astype(jnp.float32) * scale
        qpos = blk * Q_BLOCK + jnp.arange(Q_BLOCK)
        sc = jnp.where(kpos[None, :] <= qpos[:, None], sc, -jnp.inf)
        p = jax.nn.softmax(sc, axis=-1)
        a = p[0] - lam * p[1]
        return jnp.einsum('bhqk,bkhe->bqhe', a.astype(v.dtype), v)

    o = lax.map(block, (qb, jnp.arange(nb))).transpose(1, 0, 2, 3, 4).reshape(b, s, D_HEADS, 2 * D_SUB)
    o = rms_norm(o, subln) * (1.0 - lam_init)
    return o.reshape(b, s, qd) @ w_o


def hier_moe(x, w_group, w_expert, w_gate, w_up, w_down):
    b, s, d = x.shape
    n_tok = b * s
    xt = x.reshape(n_tok, d)
    p_group = jax.nn.softmax((xt @ w_group).astype(jnp.float32), axis=-1)
    g_w, g_idx = lax.top_k(p_group, 1)
    e_logits = (xt @ w_expert).astype(jnp.float32).reshape(n_tok, M_GROUPS, M_PER_GROUP)
    e_logits = e_logits[jnp.arange(n_tok), g_idx[:, 0]]
    e_w, e_idx = lax.top_k(jax.nn.softmax(e_logits, axis=-1), M_TOPK)
    weight = g_w * e_w / jnp.sum(e_w, -1, keepdims=True)
    expert = (g_idx * M_PER_GROUP + e_idx).reshape(-1)
    token = jnp.repeat(jnp.arange(n_tok), M_TOPK)
    order = jnp.argsort(expert)
    tok_sorted = token[order]
    sizes = jnp.bincount(expert, length=M_EXPERTS).astype(jnp.int32)
    xs = xt[tok_sorted]
    hid = jax.nn.silu(lax.ragged_dot(xs, w_gate, sizes)) * lax.ragged_dot(xs, w_up, sizes)
    ys = lax.ragged_dot(hid, w_down, sizes) * weight.reshape(-1)[order][:, None].astype(x.dtype)
    return jax.ops.segment_sum(ys, tok_sorted, num_segments=n_tok).reshape(b, s, d)


def setup_inputs(seed: int = 0) -> dict:
    key = jax.random.key(seed)
    keys = iter(jax.random.split(key, 40))
    beta = (8 * DEPTH) ** -0.25
    n_a, n_b, n_c, n_d = (len(range(m, DEPTH, N_MIXERS)) for m in range(N_MIXERS))
    d = D_MODEL

    def normal(shape, scale):
        return jax.random.normal(next(keys), shape, jnp.float32) * scale

    def gain(shape):
        return 1.0 + normal(shape, 0.05)

    a_cols = (A_HEADS + 2 * A_KV_HEADS) * HEAD_DIM
    b_cols = B_Q_RANK + B_KV_RANK + B_ROPE
    c_cols = (C_HEADS + 6 * C_KV_HEADS) * HEAD_DIM + 3 * C_HEADS
    d_cols = 3 * D_HEADS * 2 * D_SUB
    cmp_in = C_CMP_LEN * HEAD_DIM
    return {
        'x': normal((BATCH, SEQ, d), 1.0),
        'a_w_in': normal((n_a, d, a_cols), d ** -0.5),
        'a_sinks': normal((n_a, A_HEADS), 1.0),
        'a_w_o': normal((n_a, A_HEADS * HEAD_DIM, d), (A_HEADS * HEAD_DIM) ** -0.5 * beta),
        'b_w_down': normal((n_b, d, b_cols), d ** -0.5),
        'b_q_norm': gain((n_b, B_Q_RANK)),
        'b_kv_norm': gain((n_b, B_KV_RANK)),
        'b_w_uq': normal((n_b, B_Q_RANK, B_HEADS * (B_NOPE + B_ROPE)), B_Q_RANK ** -0.5),
        'b_w_ukv': normal((n_b, B_KV_RANK, B_HEADS * (B_NOPE + B_V)), B_KV_RANK ** -0.5),
        'b_w_o': normal((n_b, B_HEADS * B_V, d), (B_HEADS * B_V) ** -0.5 * beta),
        'c_w_in': normal((n_c, d, c_cols), d ** -0.5),
        'c_pos_k': normal((n_c, C_CMP_LEN, HEAD_DIM), 0.1),
        'c_pos_v': normal((n_c, C_CMP_LEN, HEAD_DIM), 0.1),
        'c_wk1': normal((n_c, cmp_in, C_CMP_HIDDEN), cmp_in ** -0.5),
        'c_wk2': normal((n_c, C_CMP_HIDDEN, HEAD_DIM), C_CMP_HIDDEN ** -0.5),
        'c_wv1': normal((n_c, cmp_in, C_CMP_HIDDEN), cmp_in ** -0.5),
        'c_wv2': normal((n_c, C_CMP_HIDDEN, HEAD_DIM), C_CMP_HIDDEN ** -0.5),
        'c_w_o': normal((n_c, C_HEADS * HEAD_DIM, d), (C_HEADS * HEAD_DIM) ** -0.5 * beta),
        'd_w_in': normal((n_d, d, d_cols), d ** -0.5),
        'd_lq1': normal((n_d, D_SUB), 0.1),
        'd_lk1': normal((n_d, D_SUB), 0.1),
        'd_lq2': normal((n_d, D_SUB), 0.1),
        'd_lk2': normal((n_d, D_SUB), 0.1),
        'd_subln': gain((n_d, 2 * D_SUB)),
        'd_w_o': normal((n_d, D_HEADS * 2 * D_SUB, d), (D_HEADS * 2 * D_SUB) ** -0.5 * beta),
        'moe_w_group': normal((DEPTH, d, M_GROUPS), d ** -0.5),
        'moe_w_expert': normal((DEPTH, d, M_EXPERTS), d ** -0.5),
        'moe_w_gate': normal((DEPTH, M_EXPERTS, d, M_HIDDEN), d ** -0.5),
        'moe_w_up': normal((DEPTH, M_EXPERTS, d, M_HIDDEN), d ** -0.5),
        'moe_w_down': normal((DEPTH, M_EXPERTS, M_HIDDEN, d), M_HIDDEN ** -0.5 * beta),
        'ln_g': gain((DEPTH, 2, d)),
        'ln_b': normal((DEPTH, 2, d), 0.02),
    }


def reference(x, a_w_in, a_sinks, a_w_o,
              b_w_down, b_q_norm, b_kv_norm, b_w_uq, b_w_ukv, b_w_o,
              c_w_in, c_pos_k, c_pos_v, c_wk1, c_wk2, c_wv1, c_wv2, c_w_o,
              d_w_in, d_lq1, d_lk1, d_lq2, d_lk2, d_subln, d_w_o,
              moe_w_group, moe_w_expert, moe_w_gate, moe_w_up, moe_w_down,
              ln_g, ln_b):
    alpha = (2 * DEPTH) ** 0.25
    h = x
    for i in range(DEPTH):
        kind, j = i % N_MIXERS, i // N_MIXERS
        if kind == 0:
            y = swa_sink_gqa(h, a_w_in[j], a_sinks[j], a_w_o[j])
        elif kind == 1:
            y = mla(h, b_w_down[j], b_q_norm[j], b_kv_norm[j], b_w_uq[j], b_w_ukv[j], b_w_o[j])
        elif kind == 2:
            y = nsa(h, c_w_in[j], c_pos_k[j], c_pos_v[j], c_wk1[j], c_wk2[j], c_wv1[j], c_wv2[j], c_w_o[j])
        else:
            y = diff_attention(h, d_w_in[j], d_lq1[j], d_lk1[j], d_lq2[j], d_lk2[j], d_subln[j], d_w_o[j], i)
        h = layer_norm(alpha * h + y, ln_g[i, 0], ln_b[i, 0])
        y = hier_moe(h, moe_w_group[i], moe_w_expert[i], moe_w_gate[i], moe_w_up[i], moe_w_down[i])
        h = layer_norm(alpha * h + y, ln_g[i, 1], ln_b[i, 1])
    return h
```

```python
import functools
import math

import numpy as np
import jax
import jax.numpy as jnp
from jax import lax
from jax.experimental import pallas as pl
from jax.experimental.pallas import tpu as pltpu

F32 = jnp.float32
BF16 = jnp.bfloat16

HEAD_DIM = 64
ROPE_THETA = 500000.0
ROT_DIM = HEAD_DIM // 4
A_HEADS, A_KV_HEADS, A_WINDOW = 16, 4, 128
B_HEADS, B_Q_RANK, B_KV_RANK, B_NOPE, B_ROPE, B_V = 16, 384, 256, 64, 32, 64
C_HEADS, C_KV_HEADS = 16, 4
C_CMP_LEN, C_CMP_STRIDE, C_CMP_HIDDEN = 32, 16, 128
C_SEL_LEN, C_N_SEL, C_WINDOW = 64, 16, 512
D_HEADS, D_SUB = 8, 64
M_GROUPS, M_PER_GROUP, M_TOPK, M_HIDDEN = 4, 8, 2, 512
M_EXPERTS = M_GROUPS * M_PER_GROUP
N_MIXERS = 4
LN_EPS = 1e-5
RMS_EPS = 1e-6

LANES = 128
NEG = -1e30
VMEM_LIMIT = 48 * 1024 * 1024

PROJ_ROWS = 256
LN_ROWS = 512
MOE_ROWS = 256


def _params(*sem):
    return pltpu.CompilerParams(dimension_semantics=sem, vmem_limit_bytes=VMEM_LIMIT)


def _dot(a, b):
    return jnp.dot(a, b, preferred_element_type=F32)


def _dot_t(a, b):
    return lax.dot_general(a, b, (((1,), (1,)), ((), ())), preferred_element_type=F32)


def _rope_tables(seq, rot_dim, period, off):
    half = rot_dim // 2
    inv_freq = 1.0 / (ROPE_THETA ** (jnp.arange(half, dtype=F32) * (2.0 / rot_dim)))
    ang = jnp.arange(seq, dtype=F32)[:, None] * inv_freq[None, :]
    cos, sin = jnp.cos(ang), jnp.sin(ang)
    lane = np.arange(LANES) % period - off
    first = (lane >= 0) & (lane < half)
    second = (lane >= half) & (lane < rot_dim)
    idx = np.where(first, lane, np.where(second, lane - half, 0))
    cg, sg = cos[:, idx], sin[:, idx]
    c = jnp.where(first | second, cg, 1.0)
    sa = jnp.where(first, -sg, 0.0)
    sb = jnp.where(second, sg, 0.0)
    return jnp.stack([c, sa, sb])


def _rope_block(x, tab_ref, half):
    return (x * tab_ref[0] + pltpu.roll(x, LANES - half, 1) * tab_ref[1]
            + pltpu.roll(x, half, 1) * tab_ref[2])


def _proj_kernel(x_ref, w_ref, tab_ref, *out_refs, segs, half):
    xb = x_ref[...].astype(BF16)
    for (start, width, kind, scale), o_ref in zip(segs, out_refs):
        acc = _dot(xb, w_ref[:, start:start + width])
        if kind == "rope":
            for c in range(width // LANES):
                y = _rope_block(acc[:, c * LANES:(c + 1) * LANES], tab_ref, half)
                if scale != 1.0:
                    y = y * scale
                o_ref[:, c * LANES:(c + 1) * LANES] = y.astype(o_ref.dtype)
        elif kind == "sigmoid":
            o_ref[...] = (1.0 / (1.0 + jnp.exp(-acc))).astype(o_ref.dtype)
        else:
            o_ref[...] = acc.astype(o_ref.dtype)


def _proj(x, w, tab, segs, dtypes, seq, half):
    t, k = x.shape
    n = w.shape[1]
    tm = PROJ_ROWS
    spb = seq // tm
    out_shape = [jax.ShapeDtypeStruct((t, s[1]), d) for s, d in zip(segs, dtypes)]
    return pl.pallas_call(
        functools.partial(_proj_kernel, segs=tuple(segs), half=half),
        grid=(t // tm,),
        in_specs=[pl.BlockSpec((tm, k), lambda i: (i, 0)),
                  pl.BlockSpec((k, n), lambda i: (0, 0)),
                  pl.BlockSpec((3, tm, LANES), lambda i: (0, i % spb, 0))],
        out_specs=[pl.BlockSpec((tm, s[1]), lambda i: (i, 0)) for s in segs],
        out_shape=out_shape,
        compiler_params=_params("parallel"),
        name="proj",
    )(x, w, tab)


def _layer_norm(z, g, b):
    mu = jnp.mean(z, -1, keepdims=True)
    zc = z - mu
    var = jnp.mean(zc * zc, -1, keepdims=True)
    return zc * lax.rsqrt(var + LN_EPS) * g + b


def _outproj_ln_kernel(o_ref, w_ref, h_ref, g_ref, b_ref, out_ref, *, alpha):
    y = _dot(o_ref[...], w_ref[...])
    out_ref[...] = _layer_norm(alpha * h_ref[...] + y, g_ref[...], b_ref[...])


def _outproj_ln(o, w, h, g, b, alpha):
    t, k = o.shape
    d = w.shape[1]
    tm = LN_ROWS
    return pl.pallas_call(
        functools.partial(_outproj_ln_kernel, alpha=alpha),
        grid=(t // tm,),
        in_specs=[pl.BlockSpec((tm, k), lambda i: (i, 0)),
                  pl.BlockSpec((k, d), lambda i: (0, 0)),
                  pl.BlockSpec((tm, d), lambda i: (i, 0)),
                  pl.BlockSpec((1, d), lambda i: (0, 0)),
                  pl.BlockSpec((1, d), lambda i: (0, 0))],
        out_specs=pl.BlockSpec((tm, d), lambda i: (i, 0)),
        out_shape=jax.ShapeDtypeStruct((t, d), F32),
        compiler_params=_params("parallel"),
        name="outproj_ln",
    )(o, w, h, g.reshape(1, d), b.reshape(1, d))


def _add_ln_kernel(h_ref, y_ref, g_ref, b_ref, out_ref, *, alpha):
    y = y_ref[0] + y_ref[1]
    out_ref[...] = _layer_norm(alpha * h_ref[...] + y, g_ref[...], b_ref[...])


def _add_ln(h, y2, g, b, alpha):
    t, d = h.shape
    tm = LN_ROWS
    return pl.pallas_call(
        functools.partial(_add_ln_kernel, alpha=alpha),
        grid=(t // tm,),
        in_specs=[pl.BlockSpec((tm, d), lambda i: (i, 0)),
                  pl.BlockSpec((2, tm, d), lambda i: (0, i, 0)),
                  pl.BlockSpec((1, d), lambda i: (0, 0)),
                  pl.BlockSpec((1, d), lambda i: (0, 0))],
        out_specs=pl.BlockSpec((tm, d), lambda i: (i, 0)),
        out_shape=jax.ShapeDtypeStruct((t, d), F32),
        compiler_params=_params("parallel"),
        name="add_ln",
    )(h, y2, g.reshape(1, d), b.reshape(1, d))


def _stack_group_queries(q_ref, rows):
    lo = lax.broadcasted_iota(jnp.int32, (rows, LANES), 1) < HEAD_DIM
    qa, qb = q_ref[:, :LANES], q_ref[:, LANES:]
    z = jnp.zeros_like(qa)
    return jnp.concatenate([jnp.where(lo, qa, z), jnp.where(lo, z, qa),
                            jnp.where(lo, qb, z), jnp.where(lo, z, qb)], axis=0)


def _pair(lo_val, hi_val):
    lo = lax.broadcasted_iota(jnp.int32, lo_val.shape, 1) < HEAD_DIM
    return jnp.where(lo, lo_val, hi_val)


def _swa_kernel(sink_ref, q_ref, kp_ref, kc_ref, vp_ref, vc_ref, o_ref):
    n = pl.program_id(1)
    blk = A_WINDOW
    i = lax.broadcasted_iota(jnp.int32, (blk, 2 * blk), 0)
    j = lax.broadcasted_iota(jnp.int32, (blk, 2 * blk), 1)
    dist = blk + i - j
    valid = (dist >= 0) & (dist < A_WINDOW) & ((n - 1) * blk + j >= 0)
    group = A_HEADS // A_KV_HEADS
    for g in range(A_KV_HEADS):
        qs = _stack_group_queries(q_ref.at[:, g * 2 * LANES:(g + 1) * 2 * LANES], blk)
        k = jnp.concatenate([kp_ref[:, g * LANES:(g + 1) * LANES],
                             kc_ref[:, g * LANES:(g + 1) * LANES]], axis=0)
        v = jnp.concatenate([vp_ref[:, g * LANES:(g + 1) * LANES],
                             vc_ref[:, g * LANES:(g + 1) * LANES]], axis=0)
        s = _dot_t(qs, k)
        ps = []
        for r in range(group):
            sink = sink_ref[g * group + r]
            sr = jnp.where(valid, s[r * blk:(r + 1) * blk], NEG)
            m = jnp.maximum(jnp.max(sr, -1, keepdims=True), sink)
            e = jnp.where(valid, jnp.exp(sr - m), 0.0)
            p = e / (jnp.sum(e, -1, keepdims=True) + jnp.exp(sink - m))
            ps.append(p.astype(BF16))
        o = _dot(jnp.concatenate(ps, axis=0), v)
        o_ref[:, g * 2 * LANES:g * 2 * LANES + LANES] = _pair(o[:blk], o[blk:2 * blk]).astype(o_ref.dtype)
        o_ref[:, g * 2 * LANES + LANES:(g + 1) * 2 * LANES] = _pair(
            o[2 * blk:3 * blk], o[3 * blk:]).astype(o_ref.dtype)


def _swa_attention(q, kd, vd, sinks, batch, seq):
    t = q.shape[0]
    blk = A_WINDOW
    nb = seq // blk
    qd = A_HEADS * HEAD_DIM
    kw = A_KV_HEADS * LANES
    cur = lambda b, n, s: (b * nb + n, 0)
    prev = lambda b, n, s: (b * nb + jnp.maximum(n - 1, 0), 0)
    return pl.pallas_call(
        _swa_kernel,
        grid_spec=pltpu.PrefetchScalarGridSpec(
            num_scalar_prefetch=1,
            grid=(batch, nb),
            in_specs=[pl.BlockSpec((blk, qd), cur),
                      pl.BlockSpec((blk, kw), prev), pl.BlockSpec((blk, kw), cur),
                      pl.BlockSpec((blk, kw), prev), pl.BlockSpec((blk, kw), cur)],
            out_specs=pl.BlockSpec((blk, qd), cur)),
        out_shape=jax.ShapeDtypeStruct((t, qd), BF16),
        compiler_params=_params("parallel", "parallel"),
        name="swa_attention",
    )(sinks.astype(F32), q, kd, kd, vd, vd)


def _dup_cols(w, heads):
    k = w.shape[0]
    w4 = w.reshape(k, heads, 1, HEAD_DIM)
    return jnp.broadcast_to(w4, (k, heads, 2, HEAD_DIM)).reshape(k, heads * 2 * HEAD_DIM)


def _mixer_a(h, w_in, sinks, w_o, ln_g, ln_b, alpha, batch, seq):
    qd, kd = A_HEADS * HEAD_DIM, A_KV_HEADS * HEAD_DIM
    w = jnp.concatenate([w_in[:, :qd], _dup_cols(w_in[:, qd:qd + kd], A_KV_HEADS),
                         _dup_cols(w_in[:, qd + kd:], A_KV_HEADS)], axis=1).astype(BF16)
    tab = _rope_tables(seq, ROT_DIM, HEAD_DIM, 0)
    segs = [(0, qd, "rope", HEAD_DIM ** -0.5), (qd, 2 * kd, "rope", 1.0), (qd + 2 * kd, 2 * kd, "plain", 1.0)]
    q, k2, v2 = _proj(h, w, tab, segs, [BF16, BF16, BF16], seq, ROT_DIM // 2)
    o = _swa_attention(q, k2, v2, sinks, batch, seq)
    return _outproj_ln(o, w_o.astype(BF16), h, ln_g, ln_b, alpha)


def _mla_proj_kernel(x_ref, wd_ref, qn_ref, kvn_ref, wq_ref, wk_ref, wv_ref, tab_ref,
                     q_ref, k_ref, v_ref):
    half = B_ROPE // 2
    c = _dot(x_ref[...].astype(BF16), wd_ref[...])
    cq, ckv = c[:, :B_Q_RANK], c[:, B_Q_RANK:B_Q_RANK + B_KV_RANK]
    kr = _rope_block(c[:, B_Q_RANK + B_KV_RANK:], tab_ref, half)
    cqn = (cq * lax.rsqrt(jnp.mean(cq * cq, -1, keepdims=True) + RMS_EPS) * qn_ref[...]).astype(BF16)
    ckvn = (ckv * lax.rsqrt(jnp.mean(ckv * ckv, -1, keepdims=True) + RMS_EPS) * kvn_ref[...]).astype(BF16)
    q = _dot(cqn, wq_ref[...])
    kk = _dot(ckvn, wk_ref[...])
    for hd in range(B_HEADS):
        sl = slice(hd * LANES, (hd + 1) * LANES)
        q_ref[:, sl] = _rope_block(q[:, sl], tab_ref, half).astype(q_ref.dtype)
        k_ref[:, sl] = (kk[:, sl] + kr).astype(k_ref.dtype)
    v_ref[...] = _dot(ckvn, wv_ref[...]).astype(v_ref.dtype)


def _mla_attn_kernel(q_ref, k_ref, v_ref, o_ref, *, tq, scale):
    qi = pl.program_id(2)
    row = lax.broadcasted_iota(jnp.int32, (tq, tq), 0)
    col = lax.broadcasted_iota(jnp.int32, (tq, tq), 1)
    outs = []
    for hh in range(2):
        q = q_ref[:, hh * LANES:(hh + 1) * LANES]

        def step(c, carry, diagonal, q=q, hh=hh):
            m, l, acc = carry
            base = pl.multiple_of(c * tq, tq)
            k = k_ref[pl.ds(base, tq), hh * LANES:(hh + 1) * LANES]
            v = v_ref[pl.ds(base, tq), :]
            s = _dot_t(q, k) * scale
            if diagonal:
                s = jnp.where(col <= row, s, NEG)
            m_new = jnp.maximum(m, jnp.max(s, -1, keepdims=True))
            a = jnp.exp(m - m_new)
            p = jnp.exp(s - m_new)
            l = a * l + jnp.sum(p, -1, keepdims=True)
            acc = a * acc + _dot(p.astype(BF16), v)
            return m_new, l, acc

        init = (jnp.full((tq, 1), NEG, F32), jnp.zeros((tq, 1), F32), jnp.zeros((tq, LANES), F32))
        carry = lax.fori_loop(0, qi, functools.partial(step, diagonal=False), init)
        _, l, acc = step(qi, carry, True)
        outs.append(acc / l)
    o_ref[...] = _pair(outs[0], outs[1]).astype(o_ref.dtype)


def _mixer_b(h, w_down, q_norm, kv_norm, w_uq, w_ukv, w_o, ln_g, ln_b, alpha, batch, seq):
    t, d = h.shape
    dq = B_NOPE + B_ROPE
    pad = LANES - dq
    wd = jnp.concatenate([w_down[:, :B_Q_RANK + B_KV_RANK], jnp.zeros((d, B_NOPE), F32),
                          w_down[:, B_Q_RANK + B_KV_RANK:], jnp.zeros((d, pad), F32)], axis=1).astype(BF16)
    wq = jnp.pad(w_uq.reshape(B_Q_RANK, B_HEADS, dq), ((0, 0), (0, 0), (0, pad))
                 ).reshape(B_Q_RANK, B_HEADS * LANES).astype(BF16)
    wkv = w_ukv.reshape(B_KV_RANK, B_HEADS, B_NOPE + B_V)
    wk = jnp.pad(wkv[:, :, :B_NOPE], ((0, 0), (0, 0), (0, LANES - B_NOPE))
                 ).reshape(B_KV_RANK, B_HEADS * LANES).astype(BF16)
    wv = wkv[:, :, B_NOPE:].reshape(B_KV_RANK, B_HEADS * B_V).astype(BF16)
    tab = _rope_tables(seq, B_ROPE, LANES, B_NOPE)
    tm = PROJ_ROWS
    spb = seq // tm
    full = lambda a: pl.BlockSpec(a.shape, lambda i: (0,) * a.ndim)
    qn, kvn = q_norm.reshape(1, -1), kv_norm.reshape(1, -1)
    q, k, v = pl.pallas_call(
        _mla_proj_kernel,
        grid=(t // tm,),
        in_specs=[pl.BlockSpec((tm, d), lambda i: (i, 0)), full(wd), full(qn), full(kvn),
                  full(wq), full(wk), full(wv),
                  pl.BlockSpec((3, tm, LANES), lambda i: (0, i % spb, 0))],
        out_specs=[pl.BlockSpec((tm, B_HEADS * LANES), lambda i: (i, 0)),
                   pl.BlockSpec((tm, B_HEADS * LANES), lambda i: (i, 0)),
                   pl.BlockSpec((tm, B_HEADS * B_V), lambda i: (i, 0))],
        out_shape=[jax.ShapeDtypeStruct((t, B_HEADS * LANES), BF16),
                   jax.ShapeDtypeStruct((t, B_HEADS * LANES), BF16),
                   jax.ShapeDtypeStruct((t, B_HEADS * B_V), BF16)],
        compiler_params=_params("parallel"),
        name="mla_proj",
    )(h, wd, qn, kvn, wq, wk, wv, tab)

    tq = min(512, seq)
    nq = seq // tq
    o = pl.pallas_call(
        functools.partial(_mla_attn_kernel, tq=tq, scale=dq ** -0.5),
        grid=(batch, B_HEADS // 2, nq),
        in_specs=[pl.BlockSpec((tq, 2 * LANES), lambda b, p, i: (b * nq + i, p)),
                  pl.BlockSpec((seq, 2 * LANES), lambda b, p, i: (b, p)),
                  pl.BlockSpec((seq, LANES), lambda b, p, i: (b, p))],
        out_specs=pl.BlockSpec((tq, LANES), lambda b, p, i: (b * nq + i, p)),
        out_shape=jax.ShapeDtypeStruct((t, B_HEADS * B_V), BF16),
        compiler_params=_params("parallel", "parallel", "arbitrary"),
        name="mla_attention",
    )(q, k, v)
    return _outproj_ln(o, w_o.astype(BF16), h, ln_g, ln_b, alpha)


def _gelu_tanh(x):
    return x * (0.5 * (1.0 + jnp.tanh(math.sqrt(2.0 / math.pi) * (x + 0.044715 * (x * x * x)))))


def _compress_kernel(x_ref, pe_ref, w1_ref, w2_ref, o_ref):
    half = C_CMP_STRIDE * HEAD_DIM
    x = x_ref[0]
    n = x.shape[0]
    a = _dot((x + pe_ref[0:1, :]).astype(BF16), w1_ref[:half, :])
    b = _dot((x + pe_ref[1:2, :]).astype(BF16), w1_ref[half:, :])
    hid = a + pltpu.roll(b, n - 1, 0)
    o_ref[0] = _dot(_gelu_tanh(hid).astype(BF16), w2_ref[...]).astype(o_ref.dtype)


def _compress(tc, pe, w1, w2, batch, seq):
    n = seq // C_CMP_STRIDE
    half = C_CMP_STRIDE * HEAD_DIM
    x = tc.reshape(batch, n, C_CMP_STRIDE, C_KV_HEADS, HEAD_DIM).transpose(0, 3, 1, 2, 4)
    x = x.reshape(batch * C_KV_HEADS, n, half)
    pe2 = pe.reshape(2, half)
    w2d = jnp.concatenate([w2, w2], axis=1).astype(BF16)
    return pl.pallas_call(
        _compress_kernel,
        grid=(batch * C_KV_HEADS,),
        in_specs=[pl.BlockSpec((1, n, half), lambda i: (i, 0, 0)),
                  pl.BlockSpec((2, half), lambda i: (0, 0)),
                  pl.BlockSpec((2 * half, C_CMP_HIDDEN), lambda i: (0, 0)),
                  pl.BlockSpec((C_CMP_HIDDEN, LANES), lambda i: (0, 0))],
        out_specs=pl.BlockSpec((1, n, LANES), lambda i: (i, 0, 0)),
        out_shape=jax.ShapeDtypeStruct((batch * C_KV_HEADS, n, LANES), BF16),
        compiler_params=_params("parallel"),
        name="nsa_compress",
    )(x, pe2, w1.astype(BF16), w2d)


def _nsa_attn_kernel(q_ref, ks_ref, vs_ref, kw_ref, vw_ref, kc_ref, vc_ref, ov_ref, ex_ref, gt_ref,
                     o_ref, *, tq, tk, seq):
    qi = pl.program_id(2)
    q0 = qi * tq
    group = C_HEADS // C_KV_HEADS
    n_sel_blocks = seq // C_SEL_LEN
    qs = _stack_group_queries(q_ref, tq)
    rep = lambda a: jnp.concatenate([a] * group, axis=0)

    def masked_softmax(s, valid):
        m = jnp.max(jnp.where(valid, s, NEG), -1, keepdims=True)
        m = jnp.where(m > 0.5 * NEG, m, 0.0)
        e = jnp.where(valid, jnp.exp(s - m), 0.0)
        return e / jnp.maximum(jnp.sum(e, -1, keepdims=True), 1e-30)

    nc = kc_ref.shape[1]
    t_c = q0 + lax.broadcasted_iota(jnp.int32, (tq, nc), 0)
    n_c = lax.broadcasted_iota(jnp.int32, (tq, nc), 1)
    c_valid = rep(n_c * C_CMP_STRIDE + (C_CMP_LEN - 1) <= t_c)
    p_cmp = masked_softmax(_dot_t(qs, kc_ref[0]), c_valid).astype(BF16)
    o_cmp = _dot(p_cmp, vc_ref[0])
    imp4 = _dot(p_cmp, ov_ref[...])
    imp = imp4[:tq]
    for r in range(1, group):
        imp = imp + imp4[r * tq:(r + 1) * tq]

    lane = lax.broadcasted_iota(jnp.int32, (tq, LANES), 1)
    cur = (q0 + lax.broadcasted_iota(jnp.int32, (tq, LANES), 0)) // C_SEL_LEN
    imp = jnp.where((lane == 0) | (lane == cur) | (lane == cur - 1), jnp.inf, imp)
    imp = jnp.where(lane <= cur, imp, -jnp.inf)
    rank = jnp.zeros((tq, LANES), F32)
    for i in range(n_sel_blocks):
        c_i = imp[:, i:i + 1]
        rank = rank + jnp.where(lane > i, jnp.where(c_i >= imp, 1.0, 0.0), jnp.where(c_i > imp, 1.0, 0.0))
    n_sel = min(C_N_SEL, n_sel_blocks)
    sel = jnp.where((rank < n_sel) & (lane < n_sel_blocks), 1.0, 0.0).astype(BF16)

    t_k = q0 + lax.broadcasted_iota(jnp.int32, (tq, tk), 0)
    c_k = lax.broadcasted_iota(jnp.int32, (tq, tk), 1)

    def sel_step(c, carry):
        m, l, acc = carry
        base = pl.multiple_of(c * tk, tk)
        s = _dot_t(qs, ks_ref[pl.ds(base, tk), :])
        picked = _dot(sel, ex_ref[c])
        valid = rep((picked > 0.5) & (base + c_k <= t_k))
        m_new = jnp.maximum(m, jnp.max(jnp.where(valid, s, NEG), -1, keepdims=True))
        a = jnp.exp(m - m_new)
        p = jnp.where(valid, jnp.exp(s - m_new), 0.0)
        l = a * l + jnp.sum(p, -1, keepdims=True)
        acc = a * acc + _dot(p.astype(BF16), vs_ref[pl.ds(base, tk), :])
        return m_new, l, acc

    init = (jnp.full((group * tq, 1), NEG, F32), jnp.zeros((group * tq, 1), F32),
            jnp.zeros((group * tq, LANES), F32))
    _, l, acc = lax.fori_loop(0, (q0 + tq - 1) // tk + 1, sel_step, init)
    o_slc = acc / jnp.maximum(l, 1e-30)

    span = C_WINDOW + tq
    start = pl.multiple_of(jnp.maximum(q0 - C_WINDOW, 0), tq)
    t_w = q0 + lax.broadcasted_iota(jnp.int32, (tq, span), 0)
    dist = t_w - (start + lax.broadcasted_iota(jnp.int32, (tq, span), 1))
    w_valid = rep((dist >= 0) & (dist < C_WINDOW))
    p_win = masked_softmax(_dot_t(qs, kw_ref[pl.ds(start, span), :]), w_valid).astype(BF16)
    o_win = _dot(p_win, vw_ref[pl.ds(start, span), :])

    gt = gt_ref[...]
    outs = []
    for r in range(group):
        rows = slice(r * tq, (r + 1) * tq)
        outs.append(o_cmp[rows] * gt[:, 3 * r:3 * r + 1] + o_slc[rows] * gt[:, 3 * r + 1:3 * r + 2]
                    + o_win[rows] * gt[:, 3 * r + 2:3 * r + 3])
    o_ref[:, :LANES] = _pair(outs[0], outs[1]).astype(o_ref.dtype)
    o_ref[:, LANES:] = _pair(outs[2], outs[3]).astype(o_ref.dtype)


def _mixer_c(h, w_in, pos_k, pos_v, wk1, wk2, wv1, wv2, w_o, ln_g, ln_b, alpha, batch, seq):
    t, d = h.shape
    qd, kd = C_HEADS * HEAD_DIM, C_KV_HEADS * HEAD_DIM
    group = C_HEADS // C_KV_HEADS
    kv = lambda i: w_in[:, qd + i * kd:qd + (i + 1) * kd]
    wg = jnp.pad(w_in[:, qd + 6 * kd:].reshape(d, C_KV_HEADS, group * 3),
                 ((0, 0), (0, 0), (0, LANES - group * 3))).reshape(d, C_KV_HEADS * LANES)
    w = jnp.concatenate([w_in[:, :qd], _dup_cols(kv(2), C_KV_HEADS), _dup_cols(kv(4), C_KV_HEADS),
                         _dup_cols(kv(3), C_KV_HEADS), _dup_cols(kv(5), C_KV_HEADS),
                         kv(0), kv(1), wg], axis=1).astype(BF16)
    tab = _rope_tables(seq, ROT_DIM, HEAD_DIM, 0)
    kw2 = 2 * kd
    segs = [(0, qd, "rope", HEAD_DIM ** -0.5), (qd, 2 * kw2, "rope", 1.0), (qd + 2 * kw2, 2 * kw2, "plain", 1.0),
            (qd + 4 * kw2, 2 * kd, "plain", 1.0), (qd + 4 * kw2 + 2 * kd, C_KV_HEADS * LANES, "sigmoid", 1.0)]
    q, ksw, vsw, kvc, gates = _proj(h, w, tab, segs, [BF16, BF16, BF16, F32, F32], seq, ROT_DIM // 2)
    k_cmp = _compress(kvc[:, :kd], pos_k, wk1, wk2, batch, seq)
    v_cmp = _compress(kvc[:, kd:], pos_v, wv1, wv2, batch, seq)

    nc = seq // C_CMP_STRIDE
    nsb = seq // C_SEL_LEN
    tq = 128
    tk = min(512, seq)
    nq = seq // tq
    cs = np.arange(nc)[:, None] * C_CMP_STRIDE
    ss = np.arange(LANES)[None, :] * C_SEL_LEN
    overlap = ((cs <= ss + C_SEL_LEN - 1) & (ss <= cs + C_CMP_LEN - 1) & (np.arange(LANES)[None, :] < nsb))
    overlap = jnp.asarray(overlap, BF16)
    key_blk = (np.arange(seq) // C_SEL_LEN).reshape(seq // tk, 1, tk)
    expand = jnp.asarray(key_blk == np.arange(LANES)[None, :, None], BF16)

    per_bg = lambda width: pl.BlockSpec((seq, LANES), lambda b, g, i, width=width: (b, width + g))
    o = pl.pallas_call(
        functools.partial(_nsa_attn_kernel, tq=tq, tk=tk, seq=seq),
        grid=(batch, C_KV_HEADS, nq),
        in_specs=[pl.BlockSpec((tq, 2 * LANES), lambda b, g, i: (b * nq + i, g)),
                  per_bg(0), per_bg(0), per_bg(C_KV_HEADS), per_bg(C_KV_HEADS),
                  pl.BlockSpec((1, nc, LANES), lambda b, g, i: (b * C_KV_HEADS + g, 0, 0)),
                  pl.BlockSpec((1, nc, LANES), lambda b, g, i: (b * C_KV_HEADS + g, 0, 0)),
                  pl.BlockSpec((nc, LANES), lambda b, g, i: (0, 0)),
                  pl.BlockSpec((seq // tk, LANES, tk), lambda b, g, i: (0, 0, 0)),
                  pl.BlockSpec((tq, LANES), lambda b, g, i: (b * nq + i, g))],
        out_specs=pl.BlockSpec((tq, 2 * LANES), lambda b, g, i: (b * nq + i, g)),
        out_shape=jax.ShapeDtypeStruct((t, qd), BF16),
        compiler_params=_params("parallel", "parallel", "arbitrary"),
        name="nsa_attention",
    )(q, ksw, vsw, ksw, vsw, k_cmp, v_cmp, overlap, expand, gates)
    return _outproj_ln(o, w_o.astype(BF16), h, ln_g, ln_b, alpha)


def _diff_attn_kernel(q_ref, k_ref, v_ref, lam_ref, sub_ref, o_ref, *, tq, lam_init):
    qi = pl.program_id(2)
    lam = (jnp.exp(jnp.sum(lam_ref[0:1, :] * lam_ref[1:2, :], -1, keepdims=True))
           - jnp.exp(jnp.sum(lam_ref[2:3, :] * lam_ref[3:4, :], -1, keepdims=True)) + lam_init)
    lo = lax.broadcasted_iota(jnp.int32, (tq, LANES), 1) < D_SUB
    q = q_ref[...]
    z = jnp.zeros_like(q)
    qs = jnp.concatenate([jnp.where(lo, q, z), jnp.where(lo, z, q)], axis=0)
    row = lax.broadcasted_iota(jnp.int32, (tq, tq), 0)
    col = lax.broadcasted_iota(jnp.int32, (tq, tq), 1)
    causal = jnp.concatenate([col <= row] * 2, axis=0)

    def step(c, carry, diagonal):
        m, l, acc = carry
        base = pl.multiple_of(c * tq, tq)
        s = _dot_t(qs, k_ref[pl.ds(base, tq), :])
        if diagonal:
            s = jnp.where(causal, s, NEG)
        m_new = jnp.maximum(m, jnp.max(s, -1, keepdims=True))
        a = jnp.exp(m - m_new)
        p = jnp.exp(s - m_new)
        l = a * l + jnp.sum(p, -1, keepdims=True)
        acc = a * acc + _dot(p.astype(BF16), v_ref[pl.ds(base, tq), :])
        return m_new, l, acc

    init = (jnp.full((2 * tq, 1), NEG, F32), jnp.zeros((2 * tq, 1), F32), jnp.zeros((2 * tq, LANES), F32))
    carry = lax.fori_loop(0, qi, functools.partial(step, diagonal=False), init)
    _, l, acc = step(qi, carry, True)
    o = acc / l
    o = o[:tq] - lam * o[tq:]
    o = o * lax.rsqrt(jnp.mean(o * o, -1, keepdims=True) + RMS_EPS) * sub_ref[...]
    o_ref[...] = (o * (1.0 - lam_init)).astype(o_ref.dtype)


def _mixer_d(h, w_in, lq1, lk1, lq2, lk2, subln, w_o, ln_g, ln_b, alpha, layer_idx, batch, seq):
    t, d = h.shape
    qd = D_HEADS * 2 * D_SUB
    tab = _rope_tables(seq, ROT_DIM, D_SUB, 0)
    segs = [(0, qd, "rope", D_SUB ** -0.5), (qd, qd, "rope", 1.0), (2 * qd, qd, "plain", 1.0)]
    q, k, v = _proj(h, w_in.astype(BF16), tab, segs, [BF16, BF16, BF16], seq, ROT_DIM // 2)
    lam_init = 0.8 - 0.6 * math.exp(-0.3 * layer_idx)
    lam_in = jnp.stack([lq1, lk1, lq2, lk2]).astype(F32)
    tq = min(256, seq)
    nq = seq // tq
    o = pl.pallas_call(
        functools.partial(_diff_attn_kernel, tq=tq, lam_init=lam_init),
        grid=(batch, D_HEADS, nq),
        in_specs=[pl.BlockSpec((tq, LANES), lambda b, hd, i: (b * nq + i, hd)),
                  pl.BlockSpec((seq, LANES), lambda b, hd, i: (b, hd)),
                  pl.BlockSpec((seq, LANES), lambda b, hd, i: (b, hd)),
                  pl.BlockSpec((4, D_SUB), lambda b, hd, i: (0, 0)),
                  pl.BlockSpec((1, 2 * D_SUB), lambda b, hd, i: (0, 0))],
        out_specs=pl.BlockSpec((tq, LANES), lambda b, hd, i: (b * nq + i, hd)),
        out_shape=jax.ShapeDtypeStruct((t, qd), BF16),
        compiler_params=_params("parallel", "parallel", "arbitrary"),
        name="diff_attention",
    )(q, k, v, lam_in, subln.reshape(1, -1))
    return _outproj_ln(o, w_o.astype(BF16), h, ln_g, ln_b, alpha)


def _router_kernel(x_ref, w_ref, o_ref):
    xb = x_ref[...].astype(BF16)
    lg = _dot(xb, w_ref[:, :LANES])
    le = _dot(xb, w_ref[:, LANES:])
    lane = lax.broadcasted_iota(jnp.int32, lg.shape, 1)
    far = 4 * LANES

    def softmax(x, valid):
        m = jnp.max(jnp.where(valid, x, NEG), -1, keepdims=True)
        e = jnp.where(valid, jnp.exp(x - m), 0.0)
        return e / jnp.sum(e, -1, keepdims=True)

    def first_max(p, valid):
        top = jnp.max(jnp.where(valid, p, -1.0), -1, keepdims=True)
        idx = jnp.min(jnp.where(valid & (p == top), lane, far), -1, keepdims=True)
        return top, idx

    g_valid = lane < M_GROUPS
    g_w, g_idx = first_max(softmax(lg, g_valid), g_valid)
    e_valid = (lane >= g_idx * M_PER_GROUP) & (lane < (g_idx + 1) * M_PER_GROUP)
    pe = softmax(le, e_valid)
    w0, i0 = first_max(pe, e_valid)
    rest = e_valid & (lane != i0)
    w1, i1 = first_max(pe, rest)
    tot = w0 + w1
    out = jnp.where(lane == 0, i0.astype(F32), 0.0)
    out = jnp.where(lane == 1, i1.astype(F32), out)
    out = jnp.where(lane == 2, g_w * w0 / tot, out)
    out = jnp.where(lane == 3, g_w * w1 / tot, out)
    o_ref[...] = out


def _moe_kernel(eid_ref, used_ref, src_ref, dst_ref, wt_ref, h_hbm, wg_ref, wu_ref, wd_ref, out_hbm,
                xbuf, ybuf, wgb, wub, wdb, gsem, ssem, *, tm):
    t = pl.program_id(0)

    def gather_copy(r):
        return pltpu.make_async_copy(h_hbm.at[pl.ds(src_ref[0, 0, r], 1), :], xbuf.at[pl.ds(r, 1), :], gsem)

    def scatter_copy(r):
        return pltpu.make_async_copy(ybuf.at[pl.ds(r, 1), :], out_hbm.at[pl.ds(dst_ref[0, 0, r], 1), :], ssem)

    @pl.when(t < used_ref[0])
    def _():
        def start_gather(r, c):
            gather_copy(r).start()
            return c
        lax.fori_loop(0, tm, start_gather, 0)

        @pl.when((t == 0) | (eid_ref[t] != eid_ref[jnp.maximum(t - 1, 0)]))
        def _():
            wgb[...] = wg_ref[0].astype(BF16)
            wub[...] = wu_ref[0].astype(BF16)
            wdb[...] = wd_ref[0].astype(BF16)

        def wait_gather(r, c):
            gather_copy(r).wait()
            return c
        lax.fori_loop(0, tm, wait_gather, 0)

        xb = xbuf[...].astype(BF16)
        gate = _dot(xb, wgb[...])
        up = _dot(xb, wub[...])
        hid = (gate * (1.0 / (1.0 + jnp.exp(-gate))) * up).astype(BF16)
        ybuf[...] = _dot(hid, wdb[...]) * wt_ref[...]

        def start_scatter(r, c):
            @pl.when(dst_ref[0, 0, r] >= 0)
            def _():
                scatter_copy(r).start()
            return c
        lax.fori_loop(0, tm, start_scatter, 0)

        def wait_scatter(r, c):
            @pl.when(dst_ref[0, 0, r] >= 0)
            def _():
                scatter_copy(r).wait()
            return c
        lax.fori_loop(0, tm, wait_scatter, 0)


def _hier_moe(h, w_group, w_expert, w_gate, w_up, w_down):
    t, d = h.shape
    tm = MOE_ROWS
    wr = jnp.concatenate([jnp.pad(w_group, ((0, 0), (0, LANES - M_GROUPS))),
                          jnp.pad(w_expert, ((0, 0), (0, LANES - M_EXPERTS)))], axis=1).astype(BF16)
    routed = pl.pallas_call(
        _router_kernel,
        grid=(t // LN_ROWS,),
        in_specs=[pl.BlockSpec((LN_ROWS, d), lambda i: (i, 0)),
                  pl.BlockSpec((d, 2 * LANES), lambda i: (0, 0))],
        out_specs=pl.BlockSpec((LN_ROWS, LANES), lambda i: (i, 0)),
        out_shape=jax.ShapeDtypeStruct((t, LANES), F32),
        compiler_params=_params("parallel"),
        name="moe_router",
    )(h, wr)

    n_rows = t * M_TOPK
    n_tiles = n_rows // tm + M_EXPERTS
    expert = routed[:, :M_TOPK].astype(jnp.int32).reshape(-1)
    weight = routed[:, M_TOPK:2 * M_TOPK].reshape(-1)
    order = jnp.argsort(expert).astype(jnp.int32)
    sizes = jnp.zeros((M_EXPERTS,), jnp.int32).at[expert].add(1)
    tiles = (sizes + tm - 1) // tm
    tile_end = jnp.cumsum(tiles)
    seg_start = jnp.cumsum(sizes) - sizes
    used = tile_end[-1]
    tile_ids = jnp.arange(n_tiles, dtype=jnp.int32)
    tile_eid = jnp.searchsorted(tile_end, jnp.minimum(tile_ids, used - 1), side="right").astype(jnp.int32)
    tile_first = (tile_end - tiles)[tile_eid]
    offs = (tile_ids - tile_first)[:, None] * tm + jnp.arange(tm, dtype=jnp.int32)[None, :]
    valid = (offs < sizes[tile_eid][:, None]) & (tile_ids < used)[:, None]
    row = order[jnp.clip(seg_start[tile_eid][:, None] + offs, 0, n_rows - 1)]
    src = jnp.where(valid, row // M_TOPK, 0)
    dst = jnp.where(valid, (row % M_TOPK) * t + row // M_TOPK, -1)
    wrow = jnp.where(valid, weight[row], 0.0).reshape(n_tiles * tm, 1)

    wspec = lambda shape: pl.BlockSpec((1,) + shape, lambda i, eid, used: (eid[i], 0, 0))
    idx_spec = pl.BlockSpec((1, 1, tm), lambda i, eid, used: (i, 0, 0), memory_space=pltpu.SMEM)
    y = pl.pallas_call(
        functools.partial(_moe_kernel, tm=tm),
        grid_spec=pltpu.PrefetchScalarGridSpec(
            num_scalar_prefetch=2,
            grid=(n_tiles,),
            in_specs=[idx_spec, idx_spec,
                      pl.BlockSpec((tm, 1), lambda i, eid, used: (i, 0)),
                      pl.BlockSpec(memory_space=pl.ANY),
                      wspec((d, M_HIDDEN)), wspec((d, M_HIDDEN)), wspec((M_HIDDEN, d))],
            out_specs=pl.BlockSpec(memory_space=pl.ANY),
            scratch_shapes=[pltpu.VMEM((tm, d), F32), pltpu.VMEM((tm, d), F32),
                            pltpu.VMEM((d, M_HIDDEN), BF16), pltpu.VMEM((d, M_HIDDEN), BF16),
                            pltpu.VMEM((M_HIDDEN, d), BF16),
                            pltpu.SemaphoreType.DMA(()), pltpu.SemaphoreType.DMA(())]),
        out_shape=jax.ShapeDtypeStruct((M_TOPK * t, d), F32),
        compiler_params=_params("arbitrary"),
        name="moe_experts",
    )(tile_eid, used.reshape(1).astype(jnp.int32), src.reshape(n_tiles, 1, tm), dst.reshape(n_tiles, 1, tm),
      wrow, h, w_gate, w_up, w_down)
    return y.reshape(M_TOPK, t, d)


def kernel(x, a_w_in, a_sinks, a_w_o, b_w_down, b_q_norm, b_kv_norm, b_w_uq, b_w_ukv, b_w_o, c_w_in, c_pos_k, c_pos_v, c_wk1, c_wk2, c_wv1, c_wv2, c_w_o, d_w_in, d_lq1, d_lk1, d_lq2, d_lk2, d_subln, d_w_o, moe_w_group, moe_w_expert, moe_w_gate, moe_w_up, moe_w_down, ln_g, ln_b):
    batch, seq, d = x.shape
    depth = ln_g.shape[0]
    alpha = (2 * depth) ** 0.25
    h = x.reshape(batch * seq, d)
    for i in range(depth):
        kind, j = i % N_MIXERS, i // N_MIXERS
        g, b = ln_g[i, 0], ln_b[i, 0]
        if kind == 0:
            h = _mixer_a(h, a_w_in[j], a_sinks[j], a_w_o[j], g, b, alpha, batch, seq)
        elif kind == 1:
            h = _mixer_b(h, b_w_down[j], b_q_norm[j], b_kv_norm[j], b_w_uq[j], b_w_ukv[j], b_w_o[j],
                         g, b, alpha, batch, seq)
        elif kind == 2:
            h = _mixer_c(h, c_w_in[j], c_pos_k[j], c_pos_v[j], c_wk1[j], c_wk2[j], c_wv1[j], c_wv2[j],
                         c_w_o[j], g, b, alpha, batch, seq)
        else:
            h = _mixer_d(h, d_w_in[j], d_lq1[j], d_lk1[j], d_lq2[j], d_lk2[j], d_subln[j], d_w_o[j],
                         g, b, alpha, i, batch, seq)
        y2 = _hier_moe(h, moe_w_group[i], moe_w_expert[i], moe_w_gate[i], moe_w_up[i], moe_w_down[i])
        h = _add_ln(h, y2, ln_g[i, 1], ln_b[i, 1], alpha)
    return h.reshape(batch, seq, d)
```

```python
import functools
import math

import numpy as np
import jax
import jax.numpy as jnp
from jax import lax
from jax.experimental import pallas as pl
from jax.experimental.pallas import tpu as pltpu

F32 = jnp.float32
BF16 = jnp.bfloat16

HEAD_DIM = 64
ROPE_THETA = 500000.0
ROT_DIM = HEAD_DIM // 4
A_HEADS, A_KV_HEADS, A_WINDOW = 16, 4, 128
B_HEADS, B_Q_RANK, B_KV_RANK, B_NOPE, B_ROPE, B_V = 16, 384, 256, 64, 32, 64
C_HEADS, C_KV_HEADS = 16, 4
C_CMP_LEN, C_CMP_STRIDE, C_CMP_HIDDEN = 32, 16, 128
C_SEL_LEN, C_N_SEL, C_WINDOW = 64, 16, 512
D_HEADS, D_SUB = 8, 64
M_GROUPS, M_PER_GROUP, M_TOPK, M_HIDDEN = 4, 8, 2, 512
M_EXPERTS = M_GROUPS * M_PER_GROUP
N_MIXERS = 4
LN_EPS = 1e-5
RMS_EPS = 1e-6

LANES = 128
NEG = -1e30
VMEM_LIMIT = 48 * 1024 * 1024

PROJ_ROWS = 256
LN_ROWS = 512
MOE_ROWS = 256
FLASH_ROWS = 512


def _params(*sem):
    return pltpu.CompilerParams(dimension_semantics=sem, vmem_limit_bytes=VMEM_LIMIT)


def _dot(a, b):
    return jnp.dot(a, b, preferred_element_type=F32)


def _dot_t(a, b):
    return lax.dot_general(a, b, (((1,), (1,)), ((), ())), preferred_element_type=F32)


def _rope_tables(seq, rot_dim, period, off):
    half = rot_dim // 2
    inv_freq = 1.0 / (ROPE_THETA ** (jnp.arange(half, dtype=F32) * (2.0 / rot_dim)))
    ang = jnp.arange(seq, dtype=F32)[:, None] * inv_freq[None, :]
    cos, sin = jnp.cos(ang), jnp.sin(ang)
    lane = np.arange(LANES) % period - off
    first = (lane >= 0) & (lane < half)
    second = (lane >= half) & (lane < rot_dim)
    idx = np.where(first, lane, np.where(second, lane - half, 0))
    cg, sg = cos[:, idx], sin[:, idx]
    c = jnp.where(first | second, cg, 1.0)
    sa = jnp.where(first, -sg, 0.0)
    sb = jnp.where(second, sg, 0.0)
    return jnp.stack([c, sa, sb])


def _rope_block(x, tab_ref, half):
    return (x * tab_ref[0] + pltpu.roll(x, LANES - half, 1) * tab_ref[1]
            + pltpu.roll(x, half, 1) * tab_ref[2])


def _proj_kernel(x_ref, w_ref, tab_ref, *out_refs, segs, half):
    xb = x_ref[...].astype(BF16)
    for (start, width, kind, scale), o_ref in zip(segs, out_refs):
        acc = _dot(xb, w_ref[:, start:start + width])
        if kind == "rope":
            for c in range(width // LANES):
                y = _rope_block(acc[:, c * LANES:(c + 1) * LANES], tab_ref, half)
                if scale != 1.0:
                    y = y * scale
                o_ref[:, c * LANES:(c + 1) * LANES] = y.astype(o_ref.dtype)
        elif kind == "sigmoid":
            o_ref[...] = (1.0 / (1.0 + jnp.exp(-acc))).astype(o_ref.dtype)
        else:
            o_ref[...] = acc.astype(o_ref.dtype)


def _proj(x, w, tab, segs, dtypes, seq, half):
    t, k = x.shape
    n = w.shape[1]
    tm = PROJ_ROWS
    spb = seq // tm
    out_shape = [jax.ShapeDtypeStruct((t, s[1]), d) for s, d in zip(segs, dtypes)]
    return pl.pallas_call(
        functools.partial(_proj_kernel, segs=tuple(segs), half=half),
        grid=(t // tm,),
        in_specs=[pl.BlockSpec((tm, k), lambda i: (i, 0)),
                  pl.BlockSpec((k, n), lambda i: (0, 0)),
                  pl.BlockSpec((3, tm, LANES), lambda i: (0, i % spb, 0))],
        out_specs=[pl.BlockSpec((tm, s[1]), lambda i: (i, 0)) for s in segs],
        out_shape=out_shape,
        compiler_params=_params("parallel"),
        name="proj",
    )(x, w, tab)


def _layer_norm(z, g, b):
    mu = jnp.mean(z, -1, keepdims=True)
    zc = z - mu
    var = jnp.mean(zc * zc, -1, keepdims=True)
    return zc * lax.rsqrt(var + LN_EPS) * g + b


def _outproj_ln_kernel(o_ref, w_ref, h_ref, g_ref, b_ref, out_ref, *, alpha):
    y = _dot(o_ref[...], w_ref[...])
    out_ref[...] = _layer_norm(alpha * h_ref[...] + y, g_ref[...], b_ref[...])


def _outproj_ln(o, w, h, g, b, alpha):
    t, k = o.shape
    d = w.shape[1]
    tm = LN_ROWS
    return pl.pallas_call(
        functools.partial(_outproj_ln_kernel, alpha=alpha),
        grid=(t // tm,),
        in_specs=[pl.BlockSpec((tm, k), lambda i: (i, 0)),
                  pl.BlockSpec((k, d), lambda i: (0, 0)),
                  pl.BlockSpec((tm, d), lambda i: (i, 0)),
                  pl.BlockSpec((1, d), lambda i: (0, 0)),
                  pl.BlockSpec((1, d), lambda i: (0, 0))],
        out_specs=pl.BlockSpec((tm, d), lambda i: (i, 0)),
        out_shape=jax.ShapeDtypeStruct((t, d), F32),
        compiler_params=_params("parallel"),
        name="outproj_ln",
    )(o, w, h, g.reshape(1, d), b.reshape(1, d))


def _add_ln_kernel(h_ref, y0_ref, y1_ref, g_ref, b_ref, out_ref, *, alpha):
    y = y0_ref[...] + y1_ref[...]
    out_ref[...] = _layer_norm(alpha * h_ref[...] + y, g_ref[...], b_ref[...])


def _add_ln(h, y, g, b, alpha):
    t, d = h.shape
    tm = LN_ROWS
    nt = t // tm
    return pl.pallas_call(
        functools.partial(_add_ln_kernel, alpha=alpha),
        grid=(nt,),
        in_specs=[pl.BlockSpec((tm, d), lambda i: (i, 0)),
                  pl.BlockSpec((tm, d), lambda i: (i, 0)),
                  pl.BlockSpec((tm, d), lambda i: (nt + i, 0)),
                  pl.BlockSpec((1, d), lambda i: (0, 0)),
                  pl.BlockSpec((1, d), lambda i: (0, 0))],
        out_specs=pl.BlockSpec((tm, d), lambda i: (i, 0)),
        out_shape=jax.ShapeDtypeStruct((t, d), F32),
        compiler_params=_params("parallel"),
        name="add_ln",
    )(h, y, y, g.reshape(1, d), b.reshape(1, d))


def _stack_group_queries(q_ref, rows):
    lo = lax.broadcasted_iota(jnp.int32, (rows, LANES), 1) < HEAD_DIM
    qa, qb = q_ref[:, :LANES], q_ref[:, LANES:]
    z = jnp.zeros_like(qa)
    return jnp.concatenate([jnp.where(lo, qa, z), jnp.where(lo, z, qa),
                            jnp.where(lo, qb, z), jnp.where(lo, z, qb)], axis=0)


def _pair(lo_val, hi_val):
    lo = lax.broadcasted_iota(jnp.int32, lo_val.shape, 1) < HEAD_DIM
    return jnp.where(lo, lo_val, hi_val)


def _transpose_chunks(src_ref, dst_ref, chunk):
    for c in range(src_ref.shape[0] // chunk):
        dst_ref[c] = src_ref[c * chunk:(c + 1) * chunk, :].astype(F32).T.astype(dst_ref.dtype)


def _softmax_update_t(s, valid, m, l, scale=None, always_valid=True):
    if valid is not None:
        s = jnp.where(valid, s, NEG)
    elif not always_valid:
        valid = s > 0.5 * NEG
    m_new = jnp.maximum(m, jnp.max(s, 0, keepdims=True))
    if scale is None:
        a = jnp.exp(m - m_new)
        p = jnp.exp(s - m_new)
    else:
        a = jnp.exp((m - m_new) * scale)
        p = jnp.exp((s - m_new) * scale)
    if not always_valid:
        p = jnp.where(valid, p, 0.0)
    return m_new, a * l + jnp.sum(p, 0, keepdims=True), a, p.astype(BF16)


def _flash_scratch(n_chains, dv, m_cols, tk):
    return [pltpu.VMEM((n_chains, 3, 1, m_cols), F32), pltpu.VMEM((n_chains, dv, m_cols), F32),
            pltpu.VMEM((n_chains, 2, tk, m_cols), F32), pltpu.VMEM((n_chains, 2, tk, m_cols), BF16)]


def _flash_causal_t(score_fn, value_fn, n_chains, qi, causal, state, scale=None, premasked=False):
    st_ref, acc_ref, s_ref, p_ref = state
    for ch in range(n_chains):
        st_ref[ch, 0] = jnp.full(st_ref.shape[2:], NEG, F32)
        st_ref[ch, 1] = jnp.zeros(st_ref.shape[2:], F32)
        st_ref[ch, 2] = jnp.ones(st_ref.shape[2:], F32)
        acc_ref[ch] = jnp.zeros(acc_ref.shape[1:], F32)
        p_ref[ch, 1] = jnp.zeros(p_ref.shape[2:], BF16)
        s_ref[ch, 0] = score_fn(ch, 0)

    def half(c, cur, valid, last):
        nxt = 1 - cur
        for ch in range(n_chains):
            m, l, a, p = _softmax_update_t(s_ref[ch, cur], valid, st_ref[ch, 0], st_ref[ch, 1], scale,
                                           always_valid=not premasked)
            if not last:
                s_ref[ch, nxt] = score_fn(ch, c + 1)
            acc = st_ref[ch, 2] * acc_ref[ch] + _dot(value_fn(ch, jnp.maximum(c - 1, 0)), p_ref[ch, nxt])
            if last:
                acc = a * acc + _dot(value_fn(ch, c), p)
            else:
                p_ref[ch, cur] = p
                st_ref[ch, 2] = a
            acc_ref[ch] = acc
            st_ref[ch, 0] = m
            st_ref[ch, 1] = l

    def pair(j, carry):
        half(2 * j, 0, None, False)
        half(2 * j + 1, 1, None, False)
        return carry

    lax.fori_loop(0, qi // 2, pair, 0)

    @pl.when(qi % 2 == 1)
    def _():
        half(qi - 1, 0, None, False)
        half(qi, 1, causal, True)

    @pl.when(qi % 2 == 0)
    def _():
        half(qi, 0, causal, True)

    return [(st_ref[ch, 1], acc_ref[ch]) for ch in range(n_chains)]


def _swa_kernel(sink_ref, q_ref, kp_ref, kc_ref, vp_ref, vc_ref, o_ref):
    n = pl.program_id(1)
    blk = A_WINDOW
    i = lax.broadcasted_iota(jnp.int32, (blk, 2 * blk), 0)
    j = lax.broadcasted_iota(jnp.int32, (blk, 2 * blk), 1)
    dist = blk + i - j
    valid = (dist >= 0) & (dist < A_WINDOW) & ((n - 1) * blk + j >= 0)
    group = A_HEADS // A_KV_HEADS
    for g in range(A_KV_HEADS):
        qs = _stack_group_queries(q_ref.at[:, g * 2 * LANES:(g + 1) * 2 * LANES], blk)
        k = jnp.concatenate([kp_ref[:, g * LANES:(g + 1) * LANES],
                             kc_ref[:, g * LANES:(g + 1) * LANES]], axis=0)
        v = jnp.concatenate([vp_ref[:, g * LANES:(g + 1) * LANES],
                             vc_ref[:, g * LANES:(g + 1) * LANES]], axis=0)
        s = _dot_t(qs, k)
        ps = []
        for r in range(group):
            sink = sink_ref[g * group + r]
            sr = jnp.where(valid, s[r * blk:(r + 1) * blk], NEG)
            m = jnp.maximum(jnp.max(sr, -1, keepdims=True), sink)
            e = jnp.where(valid, jnp.exp(sr - m), 0.0)
            p = e / (jnp.sum(e, -1, keepdims=True) + jnp.exp(sink - m))
            ps.append(p.astype(BF16))
        o = _dot(jnp.concatenate(ps, axis=0), v)
        o_ref[:, g * 2 * LANES:g * 2 * LANES + LANES] = _pair(o[:blk], o[blk:2 * blk]).astype(o_ref.dtype)
        o_ref[:, g * 2 * LANES + LANES:(g + 1) * 2 * LANES] = _pair(
            o[2 * blk:3 * blk], o[3 * blk:]).astype(o_ref.dtype)


def _swa_attention(q, kd, vd, sinks, batch, seq):
    t = q.shape[0]
    blk = A_WINDOW
    nb = seq // blk
    qd = A_HEADS * HEAD_DIM
    kw = A_KV_HEADS * LANES
    cur = lambda b, n, s: (b * nb + n, 0)
    prev = lambda b, n, s: (b * nb + jnp.maximum(n - 1, 0), 0)
    return pl.pallas_call(
        _swa_kernel,
        grid_spec=pltpu.PrefetchScalarGridSpec(
            num_scalar_prefetch=1,
            grid=(batch, nb),
            in_specs=[pl.BlockSpec((blk, qd), cur),
                      pl.BlockSpec((blk, kw), prev), pl.BlockSpec((blk, kw), cur),
                      pl.BlockSpec((blk, kw), prev), pl.BlockSpec((blk, kw), cur)],
            out_specs=pl.BlockSpec((blk, qd), cur)),
        out_shape=jax.ShapeDtypeStruct((t, qd), BF16),
        compiler_params=_params("parallel", "parallel"),
        name="swa_attention",
    )(sinks.astype(F32), q, kd, kd, vd, vd)


def _dup_cols(w, heads):
    k = w.shape[0]
    w4 = w.reshape(k, heads, 1, HEAD_DIM)
    return jnp.broadcast_to(w4, (k, heads, 2, HEAD_DIM)).reshape(k, heads * 2 * HEAD_DIM)


def _mixer_a(h, w_in, sinks, w_o, ln_g, ln_b, alpha, batch, seq):
    qd, kd = A_HEADS * HEAD_DIM, A_KV_HEADS * HEAD_DIM
    w = jnp.concatenate([w_in[:, :qd], _dup_cols(w_in[:, qd:qd + kd], A_KV_HEADS),
                         _dup_cols(w_in[:, qd + kd:], A_KV_HEADS)], axis=1).astype(BF16)
    tab = _rope_tables(seq, ROT_DIM, HEAD_DIM, 0)
    segs = [(0, qd, "rope", HEAD_DIM ** -0.5), (qd, 2 * kd, "rope", 1.0), (qd + 2 * kd, 2 * kd, "plain", 1.0)]
    q, k2, v2 = _proj(h, w, tab, segs, [BF16, BF16, BF16], seq, ROT_DIM // 2)
    o = _swa_attention(q, k2, v2, sinks, batch, seq)
    return _outproj_ln(o, w_o.astype(BF16), h, ln_g, ln_b, alpha)


def _mla_proj_kernel(x_ref, wd_ref, qn_ref, kvn_ref, wq_ref, wk_ref, wv_ref, tab_ref,
                     q_ref, k_ref, v_ref):
    half = B_ROPE // 2
    c = _dot(x_ref[...].astype(BF16), wd_ref[...])
    cq, ckv = c[:, :B_Q_RANK], c[:, B_Q_RANK:B_Q_RANK + B_KV_RANK]
    kr = _rope_block(c[:, B_Q_RANK + B_KV_RANK:], tab_ref, half)
    cqn = (cq * lax.rsqrt(jnp.mean(cq * cq, -1, keepdims=True) + RMS_EPS) * qn_ref[...]).astype(BF16)
    ckvn = (ckv * lax.rsqrt(jnp.mean(ckv * ckv, -1, keepdims=True) + RMS_EPS) * kvn_ref[...]).astype(BF16)
    q = _dot(cqn, wq_ref[...])
    kk = _dot(ckvn, wk_ref[...])
    for hd in range(B_HEADS):
        sl = slice(hd * LANES, (hd + 1) * LANES)
        q_ref[:, sl] = _rope_block(q[:, sl], tab_ref, half).astype(q_ref.dtype)
        k_ref[:, sl] = (kk[:, sl] + kr).astype(k_ref.dtype)
    v_ref[...] = _dot(ckvn, wv_ref[...]).astype(v_ref.dtype)


def _mla_attn_kernel(q_ref, k_ref, v_ref, o_ref, vt_ref, *state, tq, scale):
    qi = pl.program_id(2)

    @pl.when(qi == 0)
    def _():
        _transpose_chunks(v_ref, vt_ref, tq)

    q_t = [q_ref[:, hh * LANES:(hh + 1) * LANES].astype(F32).T.astype(BF16) for hh in range(2)]
    key = lax.broadcasted_iota(jnp.int32, (tq, tq), 0)
    qry = lax.broadcasted_iota(jnp.int32, (tq, tq), 1)
    causal = key <= qry

    def scores(hh, c):
        return _dot(k_ref[pl.ds(pl.multiple_of(c * tq, tq), tq), hh * LANES:(hh + 1) * LANES], q_t[hh])

    def values(hh, c):
        return vt_ref[c, hh * B_V:(hh + 1) * B_V, :]

    res = _flash_causal_t(scores, values, 2, qi, causal, state, scale)
    o_t = jnp.concatenate([acc / l for l, acc in res], axis=0)
    o_ref[...] = o_t.T.astype(o_ref.dtype)


def _mixer_b(h, w_down, q_norm, kv_norm, w_uq, w_ukv, w_o, ln_g, ln_b, alpha, batch, seq):
    t, d = h.shape
    dq = B_NOPE + B_ROPE
    pad = LANES - dq
    wd = jnp.concatenate([w_down[:, :B_Q_RANK + B_KV_RANK], jnp.zeros((d, B_NOPE), F32),
                          w_down[:, B_Q_RANK + B_KV_RANK:], jnp.zeros((d, pad), F32)], axis=1).astype(BF16)
    wq = jnp.pad(w_uq.reshape(B_Q_RANK, B_HEADS, dq), ((0, 0), (0, 0), (0, pad))
                 ).reshape(B_Q_RANK, B_HEADS * LANES).astype(BF16)
    wkv = w_ukv.reshape(B_KV_RANK, B_HEADS, B_NOPE + B_V)
    wk = jnp.pad(wkv[:, :, :B_NOPE], ((0, 0), (0, 0), (0, LANES - B_NOPE))
                 ).reshape(B_KV_RANK, B_HEADS * LANES).astype(BF16)
    wv = wkv[:, :, B_NOPE:].reshape(B_KV_RANK, B_HEADS * B_V).astype(BF16)
    tab = _rope_tables(seq, B_ROPE, LANES, B_NOPE)
    tm = PROJ_ROWS
    spb = seq // tm
    full = lambda a: pl.BlockSpec(a.shape, lambda i: (0,) * a.ndim)
    qn, kvn = q_norm.reshape(1, -1), kv_norm.reshape(1, -1)
    q, k, v = pl.pallas_call(
        _mla_proj_kernel,
        grid=(t // tm,),
        in_specs=[pl.BlockSpec((tm, d), lambda i: (i, 0)), full(wd), full(qn), full(kvn),
                  full(wq), full(wk), full(wv),
                  pl.BlockSpec((3, tm, LANES), lambda i: (0, i % spb, 0))],
        out_specs=[pl.BlockSpec((tm, B_HEADS * LANES), lambda i: (i, 0)),
                   pl.BlockSpec((tm, B_HEADS * LANES), lambda i: (i, 0)),
                   pl.BlockSpec((tm, B_HEADS * B_V), lambda i: (i, 0))],
        out_shape=[jax.ShapeDtypeStruct((t, B_HEADS * LANES), BF16),
                   jax.ShapeDtypeStruct((t, B_HEADS * LANES), BF16),
                   jax.ShapeDtypeStruct((t, B_HEADS * B_V), BF16)],
        compiler_params=_params("parallel"),
        name="mla_proj",
    )(h, wd, qn, kvn, wq, wk, wv, tab)

    tq = min(FLASH_ROWS, seq)
    nq = seq // tq
    o = pl.pallas_call(
        functools.partial(_mla_attn_kernel, tq=tq, scale=dq ** -0.5),
        grid=(batch, B_HEADS // 2, nq),
        in_specs=[pl.BlockSpec((tq, 2 * LANES), lambda b, p, i: (b * nq + i, p)),
                  pl.BlockSpec((seq, 2 * LANES), lambda b, p, i: (b, p)),
                  pl.BlockSpec((seq, LANES), lambda b, p, i: (b, p))],
        out_specs=pl.BlockSpec((tq, LANES), lambda b, p, i: (b * nq + i, p)),
        out_shape=jax.ShapeDtypeStruct((t, B_HEADS * B_V), BF16),
        scratch_shapes=[pltpu.VMEM((nq, LANES, tq), BF16)] + _flash_scratch(2, B_V, tq, tq),
        compiler_params=_params("parallel", "parallel", "arbitrary"),
        name="mla_attention",
    )(q, k, v)
    return _outproj_ln(o, w_o.astype(BF16), h, ln_g, ln_b, alpha)


def _gelu_tanh(x):
    return x * (0.5 * (1.0 + jnp.tanh(math.sqrt(2.0 / math.pi) * (x + 0.044715 * (x * x * x)))))


def _compress_kernel(x_ref, pe_ref, w1_ref, w2_ref, o_ref):
    half = C_CMP_STRIDE * HEAD_DIM
    x = x_ref[0]
    n = x.shape[0]
    a = _dot((x + pe_ref[0:1, :]).astype(BF16), w1_ref[:half, :])
    b = _dot((x + pe_ref[1:2, :]).astype(BF16), w1_ref[half:, :])
    hid = a + pltpu.roll(b, n - 1, 0)
    o_ref[0] = _dot(_gelu_tanh(hid).astype(BF16), w2_ref[...]).astype(o_ref.dtype)


def _compress(tc, pe, w1, w2, batch, seq):
    n = seq // C_CMP_STRIDE
    half = C_CMP_STRIDE * HEAD_DIM
    x = tc.reshape(batch, n, C_CMP_STRIDE, C_KV_HEADS, HEAD_DIM).transpose(0, 3, 1, 2, 4)
    x = x.reshape(batch * C_KV_HEADS, n, half)
    pe2 = pe.reshape(2, half)
    w2d = jnp.concatenate([w2, w2], axis=1).astype(BF16)
    return pl.pallas_call(
        _compress_kernel,
        grid=(batch * C_KV_HEADS,),
        in_specs=[pl.BlockSpec((1, n, half), lambda i: (i, 0, 0)),
                  pl.BlockSpec((2, half), lambda i: (0, 0)),
                  pl.BlockSpec((2 * half, C_CMP_HIDDEN), lambda i: (0, 0)),
                  pl.BlockSpec((C_CMP_HIDDEN, LANES), lambda i: (0, 0))],
        out_specs=pl.BlockSpec((1, n, LANES), lambda i: (i, 0, 0)),
        out_shape=jax.ShapeDtypeStruct((batch * C_KV_HEADS, n, LANES), BF16),
        compiler_params=_params("parallel"),
        name="nsa_compress",
    )(x, pe2, w1.astype(BF16), w2d)


def _nsa_attn_kernel(q_ref, ks_ref, vs_ref, kw_ref, vw_ref, kc_ref, vc_ref, ov_ref, ex_ref, gt_ref,
                     o_ref, vst_ref, vwt_ref, vct_ref, *state, tq, tk, seq):
    qi = pl.program_id(2)
    q0 = qi * tq
    group = C_HEADS // C_KV_HEADS
    n_sel_blocks = seq // C_SEL_LEN
    nc = kc_ref.shape[1]

    @pl.when(qi == 0)
    def _():
        _transpose_chunks(vs_ref, vst_ref, tk)
        _transpose_chunks(vw_ref, vwt_ref, tq)
        vct_ref[...] = vc_ref[0].astype(F32).T.astype(vct_ref.dtype)

    top = lax.broadcasted_iota(jnp.int32, (LANES, tq), 0) < HEAD_DIM
    qa_t = q_ref[:, :LANES].astype(F32).T
    qb_t = q_ref[:, LANES:].astype(F32).T
    qs_t = jnp.concatenate([jnp.where(top, qa_t, 0.0), jnp.where(top, 0.0, qa_t),
                            jnp.where(top, qb_t, 0.0), jnp.where(top, 0.0, qb_t)], axis=1).astype(BF16)
    rep = lambda a: jnp.concatenate([a] * group, axis=1)

    def masked_softmax(s, valid):
        m = jnp.max(jnp.where(valid, s, NEG), 0, keepdims=True)
        m = jnp.where(m > 0.5 * NEG, m, 0.0)
        e = jnp.where(valid, jnp.exp(s - m), 0.0)
        return e / jnp.maximum(jnp.sum(e, 0, keepdims=True), 1e-30)

    t_c = q0 + lax.broadcasted_iota(jnp.int32, (nc, tq), 1)
    n_c = lax.broadcasted_iota(jnp.int32, (nc, tq), 0)
    c_valid = rep(n_c * C_CMP_STRIDE + (C_CMP_LEN - 1) <= t_c)
    p_cmp = masked_softmax(_dot(kc_ref[0], qs_t), c_valid).astype(BF16)
    o_cmp = _dot(vct_ref[...], p_cmp)
    imp4 = _dot(ov_ref[...], p_cmp)
    imp = imp4[:n_sel_blocks, :tq]
    for r in range(1, group):
        imp = imp + imp4[:n_sel_blocks, r * tq:(r + 1) * tq]

    blk = lax.broadcasted_iota(jnp.int32, (n_sel_blocks, tq), 0)
    cur = (q0 + lax.broadcasted_iota(jnp.int32, (n_sel_blocks, tq), 1)) // C_SEL_LEN
    imp = jnp.where((blk == 0) | (blk == cur) | (blk == cur - 1), jnp.inf, imp)
    imp = jnp.where(blk <= cur, imp, -jnp.inf)
    rank = jnp.zeros((n_sel_blocks, tq), F32)
    for i in range(n_sel_blocks):
        r_i = imp[i:i + 1, :]
        rank = rank + jnp.where(blk > i, jnp.where(r_i >= imp, 1.0, 0.0), jnp.where(r_i > imp, 1.0, 0.0))
    sel = jnp.where(rank < min(C_N_SEL, n_sel_blocks), 1.0, 0.0)
    sel = jnp.concatenate([sel, jnp.zeros((LANES - n_sel_blocks, tq), F32)], axis=0).astype(BF16)

    span = C_WINDOW + tq
    start = pl.multiple_of(jnp.maximum(q0 - C_WINDOW, 0), tq)
    t_w = q0 + lax.broadcasted_iota(jnp.int32, (span, tq), 1)
    dist = t_w - (start + lax.broadcasted_iota(jnp.int32, (span, tq), 0))
    w_valid = rep((dist >= 0) & (dist < C_WINDOW))
    p_win = masked_softmax(_dot(kw_ref[pl.ds(start, span), :], qs_t), w_valid).astype(BF16)
    o_win = _dot(vwt_ref[start // tq], p_win[:tq])
    for cb in range(1, span // tq):
        o_win = o_win + _dot(vwt_ref[start // tq + cb], p_win[cb * tq:(cb + 1) * tq])

    t_k = q0 + lax.broadcasted_iota(jnp.int32, (tk, tq), 1)
    k_k = lax.broadcasted_iota(jnp.int32, (tk, tq), 0)

    def sel_scores(ch, c):
        base = pl.multiple_of(c * tk, tk)
        picked = _dot(ex_ref[c], sel)
        valid = rep((picked > 0.5) & (base + k_k <= t_k))
        return jnp.where(valid, _dot(ks_ref[pl.ds(base, tk), :], qs_t), NEG)

    (l, acc), = _flash_causal_t(sel_scores, lambda ch, c: vst_ref[c], 1, (q0 + tq - 1) // tk, None, state,
                                premasked=True)
    o_slc = acc / jnp.maximum(l, 1e-30)

    g_t = gt_ref[...].T
    outs = []
    for r in range(group):
        cols = slice(r * tq, (r + 1) * tq)
        outs.append(o_cmp[:, cols] * g_t[3 * r:3 * r + 1] + o_slc[:, cols] * g_t[3 * r + 1:3 * r + 2]
                    + o_win[:, cols] * g_t[3 * r + 2:3 * r + 3])
    o_ref[:, :LANES] = jnp.where(top, outs[0], outs[1]).T.astype(o_ref.dtype)
    o_ref[:, LANES:] = jnp.where(top, outs[2], outs[3]).T.astype(o_ref.dtype)


def _mixer_c(h, w_in, pos_k, pos_v, wk1, wk2, wv1, wv2, w_o, ln_g, ln_b, alpha, batch, seq):
    t, d = h.shape
    qd, kd = C_HEADS * HEAD_DIM, C_KV_HEADS * HEAD_DIM
    group = C_HEADS // C_KV_HEADS
    kv = lambda i: w_in[:, qd + i * kd:qd + (i + 1) * kd]
    wg = jnp.pad(w_in[:, qd + 6 * kd:].reshape(d, C_KV_HEADS, group * 3),
                 ((0, 0), (0, 0), (0, LANES - group * 3))).reshape(d, C_KV_HEADS * LANES)
    w = jnp.concatenate([w_in[:, :qd], _dup_cols(kv(2), C_KV_HEADS), _dup_cols(kv(4), C_KV_HEADS),
                         _dup_cols(kv(3), C_KV_HEADS), _dup_cols(kv(5), C_KV_HEADS),
                         kv(0), kv(1), wg], axis=1).astype(BF16)
    tab = _rope_tables(seq, ROT_DIM, HEAD_DIM, 0)
    kw2 = 2 * kd
    segs = [(0, qd, "rope", HEAD_DIM ** -0.5), (qd, 2 * kw2, "rope", 1.0), (qd + 2 * kw2, 2 * kw2, "plain", 1.0),
            (qd + 4 * kw2, 2 * kd, "plain", 1.0), (qd + 4 * kw2 + 2 * kd, C_KV_HEADS * LANES, "sigmoid", 1.0)]
    q, ksw, vsw, kvc, gates = _proj(h, w, tab, segs, [BF16, BF16, BF16, F32, F32], seq, ROT_DIM // 2)
    k_cmp = _compress(kvc[:, :kd], pos_k, wk1, wk2, batch, seq)
    v_cmp = _compress(kvc[:, kd:], pos_v, wv1, wv2, batch, seq)

    nc = seq // C_CMP_STRIDE
    nsb = seq // C_SEL_LEN
    tq = 128
    tk = min(512, seq)
    nq = seq // tq
    cs = np.arange(nc)[None, :] * C_CMP_STRIDE
    ss = np.arange(LANES)[:, None] * C_SEL_LEN
    overlap = ((cs <= ss + C_SEL_LEN - 1) & (ss <= cs + C_CMP_LEN - 1) & (np.arange(LANES)[:, None] < nsb))
    overlap = jnp.asarray(overlap, BF16)
    key_blk = (np.arange(seq) // C_SEL_LEN).reshape(seq // tk, tk, 1)
    expand = jnp.asarray(key_blk == np.arange(LANES)[None, None, :], BF16)

    per_bg = lambda width: pl.BlockSpec((seq, LANES), lambda b, g, i, width=width: (b, width + g))
    o = pl.pallas_call(
        functools.partial(_nsa_attn_kernel, tq=tq, tk=tk, seq=seq),
        grid=(batch, C_KV_HEADS, nq),
        in_specs=[pl.BlockSpec((tq, 2 * LANES), lambda b, g, i: (b * nq + i, g)),
                  per_bg(0), per_bg(0), per_bg(C_KV_HEADS), per_bg(C_KV_HEADS),
                  pl.BlockSpec((1, nc, LANES), lambda b, g, i: (b * C_KV_HEADS + g, 0, 0)),
                  pl.BlockSpec((1, nc, LANES), lambda b, g, i: (b * C_KV_HEADS + g, 0, 0)),
                  pl.BlockSpec((LANES, nc), lambda b, g, i: (0, 0)),
                  pl.BlockSpec((seq // tk, tk, LANES), lambda b, g, i: (0, 0, 0)),
                  pl.BlockSpec((tq, LANES), lambda b, g, i: (b * nq + i, g))],
        out_specs=pl.BlockSpec((tq, 2 * LANES), lambda b, g, i: (b * nq + i, g)),
        out_shape=jax.ShapeDtypeStruct((t, qd), BF16),
        scratch_shapes=[pltpu.VMEM((seq // tk, LANES, tk), BF16), pltpu.VMEM((seq // tq, LANES, tq), BF16),
                        pltpu.VMEM((LANES, nc), BF16)] + _flash_scratch(1, LANES, group * tq, tk),
        compiler_params=_params("parallel", "parallel", "arbitrary"),
        name="nsa_attention",
    )(q, ksw, vsw, ksw, vsw, k_cmp, v_cmp, overlap, expand, gates)
    return _outproj_ln(o, w_o.astype(BF16), h, ln_g, ln_b, alpha)


def _diff_attn_kernel(q_ref, k_ref, v_ref, lam_ref, sub_ref, o_ref, vt_ref, *state, tq, lam_init):
    qi = pl.program_id(2)

    n_heads = q_ref.shape[1] // LANES

    @pl.when(qi == 0)
    def _():
        for hd in range(n_heads):
            _transpose_chunks(v_ref.at[:, hd * LANES:(hd + 1) * LANES], vt_ref.at[hd], tq)

    lam = (jnp.exp(jnp.sum(lam_ref[0:1, :] * lam_ref[1:2, :], -1, keepdims=True))
           - jnp.exp(jnp.sum(lam_ref[2:3, :] * lam_ref[3:4, :], -1, keepdims=True)) + lam_init)
    top = lax.broadcasted_iota(jnp.int32, (LANES, tq), 0) < D_SUB
    qs_t = []
    for hd in range(n_heads):
        q_t = q_ref[:, hd * LANES:(hd + 1) * LANES].astype(F32).T
        qs_t.append(jnp.concatenate([jnp.where(top, q_t, 0.0), jnp.where(top, 0.0, q_t)], axis=1).astype(BF16))
    key = lax.broadcasted_iota(jnp.int32, (tq, tq), 0)
    qry = lax.broadcasted_iota(jnp.int32, (tq, tq), 1)
    causal = jnp.concatenate([key <= qry] * 2, axis=1)

    def scores(hd, c):
        return _dot(k_ref[pl.ds(pl.multiple_of(c * tq, tq), tq), hd * LANES:(hd + 1) * LANES], qs_t[hd])

    def values(hd, c):
        return vt_ref[hd, c]

    res = _flash_causal_t(scores, values, n_heads, qi, causal, state)
    for hd in range(n_heads):
        l, acc = res[hd]
        o = (acc / l).T
        o = o[:tq] - lam * o[tq:]
        o = o * lax.rsqrt(jnp.mean(o * o, -1, keepdims=True) + RMS_EPS) * sub_ref[...]
        o_ref[:, hd * LANES:(hd + 1) * LANES] = (o * (1.0 - lam_init)).astype(o_ref.dtype)


def _mixer_d(h, w_in, lq1, lk1, lq2, lk2, subln, w_o, ln_g, ln_b, alpha, layer_idx, batch, seq):
    t, d = h.shape
    qd = D_HEADS * 2 * D_SUB
    tab = _rope_tables(seq, ROT_DIM, D_SUB, 0)
    segs = [(0, qd, "rope", D_SUB ** -0.5), (qd, qd, "rope", 1.0), (2 * qd, qd, "plain", 1.0)]
    q, k, v = _proj(h, w_in.astype(BF16), tab, segs, [BF16, BF16, BF16], seq, ROT_DIM // 2)
    lam_init = 0.8 - 0.6 * math.exp(-0.3 * layer_idx)
    lam_in = jnp.stack([lq1, lk1, lq2, lk2]).astype(F32)
    tq = min(FLASH_ROWS, seq)
    nq = seq // tq
    hps = 2
    o = pl.pallas_call(
        functools.partial(_diff_attn_kernel, tq=tq, lam_init=lam_init),
        grid=(batch, D_HEADS // hps, nq),
        in_specs=[pl.BlockSpec((tq, hps * LANES), lambda b, hd, i: (b * nq + i, hd)),
                  pl.BlockSpec((seq, hps * LANES), lambda b, hd, i: (b, hd)),
                  pl.BlockSpec((seq, hps * LANES), lambda b, hd, i: (b, hd)),
                  pl.BlockSpec((4, D_SUB), lambda b, hd, i: (0, 0)),
                  pl.BlockSpec((1, 2 * D_SUB), lambda b, hd, i: (0, 0))],
        out_specs=pl.BlockSpec((tq, hps * LANES), lambda b, hd, i: (b * nq + i, hd)),
        out_shape=jax.ShapeDtypeStruct((t, qd), BF16),
        scratch_shapes=[pltpu.VMEM((hps, nq, LANES, tq), BF16)] + _flash_scratch(hps, LANES, 2 * tq, tq),
        compiler_params=_params("parallel", "parallel", "arbitrary"),
        name="diff_attention",
    )(q, k, v, lam_in, subln.reshape(1, -1))
    return _outproj_ln(o, w_o.astype(BF16), h, ln_g, ln_b, alpha)


def _router_kernel(x_ref, w_ref, o_ref):
    xb = x_ref[...].astype(BF16)
    lg = _dot(xb, w_ref[:, :LANES])
    le = _dot(xb, w_ref[:, LANES:])
    lane = lax.broadcasted_iota(jnp.int32, lg.shape, 1)
    far = 4 * LANES

    def softmax(x, valid):
        m = jnp.max(jnp.where(valid, x, NEG), -1, keepdims=True)
        e = jnp.where(valid, jnp.exp(x - m), 0.0)
        return e / jnp.sum(e, -1, keepdims=True)

    def first_max(p, valid):
        top = jnp.max(jnp.where(valid, p, -1.0), -1, keepdims=True)
        idx = jnp.min(jnp.where(valid & (p == top), lane, far), -1, keepdims=True)
        return top, idx

    g_valid = lane < M_GROUPS
    g_w, g_idx = first_max(softmax(lg, g_valid), g_valid)
    e_valid = (lane >= g_idx * M_PER_GROUP) & (lane < (g_idx + 1) * M_PER_GROUP)
    pe = softmax(le, e_valid)
    w0, i0 = first_max(pe, e_valid)
    rest = e_valid & (lane != i0)
    w1, i1 = first_max(pe, rest)
    tot = w0 + w1
    out = jnp.where(lane == 0, i0.astype(F32), 0.0)
    out = jnp.where(lane == 1, i1.astype(F32), out)
    out = jnp.where(lane == 2, g_w * w0 / tot, out)
    out = jnp.where(lane == 3, g_w * w1 / tot, out)
    o_ref[...] = out


def _moe_kernel(eid_ref, used_ref, src0_ref, src_ref, dstp_ref, wt_ref, h_hbm, wg_ref, wu_ref, wd_ref, out_hbm,
                xbuf, ybuf, wgb, wub, wdb, gsem, ssem, *, tm):
    t = pl.program_id(0)
    used = used_ref[0]

    def start_gather(idx_ref, s):
        for r in range(tm):
            pltpu.make_async_copy(h_hbm.at[pl.ds(idx_ref[0, 0, r], 1), :],
                                  xbuf.at[s, pl.ds(r, 1), :], gsem.at[s]).start()

    def start_scatter(s):
        for r in range(tm):
            pltpu.make_async_copy(ybuf.at[s, pl.ds(r, 1), :],
                                  out_hbm.at[pl.ds(dstp_ref[0, 0, r], 1), :], ssem.at[s]).start()

    def wait_gather(s):
        pltpu.make_async_copy(h_hbm.at[pl.ds(0, tm), :], xbuf.at[s], gsem.at[s]).wait()

    def wait_scatter(s):
        pltpu.make_async_copy(ybuf.at[s], out_hbm.at[pl.ds(0, tm), :], ssem.at[s]).wait()

    @pl.when(t == 0)
    def _():
        start_gather(src0_ref, 0)
        ybuf[1] = jnp.zeros(ybuf.shape[1:], F32)
        fill = pltpu.make_async_copy(ybuf.at[1], out_hbm.at[pl.ds(out_hbm.shape[0] - 2 * tm, tm), :], ssem.at[0])
        fill.start()
        fill.wait()

    @pl.when((t < used) & ((t == 0) | (eid_ref[t] != eid_ref[jnp.maximum(t - 1, 0)])))
    def _():
        wgb[...] = wg_ref[0].astype(BF16)
        wub[...] = wu_ref[0].astype(BF16)
        wdb[...] = wd_ref[0].astype(BF16)

    for s in range(2):
        @pl.when((t < used) & (t % 2 == s))
        def _(s=s):
            wait_gather(s)

            @pl.when(t >= 1)
            def _():
                wait_scatter(s)

            start_scatter(1 - s)
            start_gather(src_ref, 1 - s)
            xb = xbuf[s].astype(BF16)
            gate = _dot(xb, wgb[...])
            up = _dot(xb, wub[...])
            hid = (gate * (1.0 / (1.0 + jnp.exp(-gate))) * up).astype(BF16)
            ybuf[s] = _dot(hid, wdb[...]) * wt_ref[...]

        @pl.when((t == used) & (t % 2 == s))
        def _(s=s):
            start_scatter(1 - s)
            wait_scatter(s)
            wait_scatter(1 - s)
            wait_gather(s)


def _hier_moe(h, w_group, w_expert, w_gate, w_up, w_down):
    t, d = h.shape
    tm = MOE_ROWS
    wr = jnp.concatenate([jnp.pad(w_group, ((0, 0), (0, LANES - M_GROUPS))),
                          jnp.pad(w_expert, ((0, 0), (0, LANES - M_EXPERTS)))], axis=1).astype(BF16)
    routed = pl.pallas_call(
        _router_kernel,
        grid=(t // LN_ROWS,),
        in_specs=[pl.BlockSpec((LN_ROWS, d), lambda i: (i, 0)),
                  pl.BlockSpec((d, 2 * LANES), lambda i: (0, 0))],
        out_specs=pl.BlockSpec((LN_ROWS, LANES), lambda i: (i, 0)),
        out_shape=jax.ShapeDtypeStruct((t, LANES), F32),
        compiler_params=_params("parallel"),
        name="moe_router",
    )(h, wr)

    n_rows = t * M_TOPK
    n_tiles = n_rows // tm + M_EXPERTS
    expert = routed[:, :M_TOPK].astype(jnp.int32).reshape(-1)
    weight = routed[:, M_TOPK:2 * M_TOPK].reshape(-1)
    order = jnp.argsort(expert).astype(jnp.int32)
    sizes = jnp.zeros((M_EXPERTS,), jnp.int32).at[expert].add(1)
    tiles = (sizes + tm - 1) // tm
    tile_end = jnp.cumsum(tiles)
    seg_start = jnp.cumsum(sizes) - sizes
    used = tile_end[-1]
    tile_ids = jnp.arange(n_tiles, dtype=jnp.int32)
    tile_eid = jnp.searchsorted(tile_end, jnp.minimum(tile_ids, used - 1), side="right").astype(jnp.int32)
    tile_first = (tile_end - tiles)[tile_eid]
    lane_r = jnp.arange(tm, dtype=jnp.int32)[None, :]
    offs = (tile_ids - tile_first)[:, None] * tm + lane_r
    valid = (offs < sizes[tile_eid][:, None]) & (tile_ids < used)[:, None]
    row = order[jnp.clip(seg_start[tile_eid][:, None] + offs, 0, n_rows - 1)]
    src = jnp.where(valid, row // M_TOPK, 0)
    trash = n_rows + (tile_ids % 2)[:, None] * tm + lane_r
    dst = jnp.where(valid, (row % M_TOPK) * t + row // M_TOPK, trash)
    dstp = jnp.concatenate([n_rows + tm + lane_r, dst], axis=0).reshape(n_tiles + 1, 1, tm)
    wrow = jnp.where(valid, weight[row], 0.0).reshape(n_tiles * tm, 1)
    src = src.reshape(n_tiles, 1, tm)

    wspec = lambda shape: pl.BlockSpec((1,) + shape, lambda i, eid, used: (eid[i], 0, 0))
    idx_spec = pl.BlockSpec((1, 1, tm), lambda i, eid, used: (i, 0, 0), memory_space=pltpu.SMEM)
    nxt_spec = pl.BlockSpec((1, 1, tm), lambda i, eid, used: (jnp.minimum(i + 1, n_tiles - 1), 0, 0),
                            memory_space=pltpu.SMEM)
    return pl.pallas_call(
        functools.partial(_moe_kernel, tm=tm),
        grid_spec=pltpu.PrefetchScalarGridSpec(
            num_scalar_prefetch=2,
            grid=(n_tiles,),
            in_specs=[idx_spec, nxt_spec, idx_spec,
                      pl.BlockSpec((tm, 1), lambda i, eid, used: (i, 0)),
                      pl.BlockSpec(memory_space=pl.ANY),
                      wspec((d, M_HIDDEN)), wspec((d, M_HIDDEN)), wspec((M_HIDDEN, d))],
            out_specs=pl.BlockSpec(memory_space=pl.ANY),
            scratch_shapes=[pltpu.VMEM((2, tm, d), F32), pltpu.VMEM((2, tm, d), F32),
                            pltpu.VMEM((d, M_HIDDEN), BF16), pltpu.VMEM((d, M_HIDDEN), BF16),
                            pltpu.VMEM((M_HIDDEN, d), BF16),
                            pltpu.SemaphoreType.DMA((2,)), pltpu.SemaphoreType.DMA((2,))]),
        out_shape=jax.ShapeDtypeStruct((n_rows + 2 * tm, d), F32),
        compiler_params=_params("arbitrary"),
        name="moe_experts",
    )(tile_eid, used.reshape(1).astype(jnp.int32), src, src, dstp, wrow, h, w_gate, w_up, w_down)


def kernel(x, a_w_in, a_sinks, a_w_o, b_w_down, b_q_norm, b_kv_norm, b_w_uq, b_w_ukv, b_w_o, c_w_in, c_pos_k, c_pos_v, c_wk1, c_wk2, c_wv1, c_wv2, c_w_o, d_w_in, d_lq1, d_lk1, d_lq2, d_lk2, d_subln, d_w_o, moe_w_group, moe_w_expert, moe_w_gate, moe_w_up, moe_w_down, ln_g, ln_b):
    batch, seq, d = x.shape
    depth = ln_g.shape[0]
    alpha = (2 * depth) ** 0.25
    h = x.reshape(batch * seq, d)
    for i in range(depth):
        kind, j = i % N_MIXERS, i // N_MIXERS
        g, b = ln_g[i, 0], ln_b[i, 0]
        if kind == 0:
            h = _mixer_a(h, a_w_in[j], a_sinks[j], a_w_o[j], g, b, alpha, batch, seq)
        elif kind == 1:
            h = _mixer_b(h, b_w_down[j], b_q_norm[j], b_kv_norm[j], b_w_uq[j], b_w_ukv[j], b_w_o[j],
                         g, b, alpha, batch, seq)
        elif kind == 2:
            h = _mixer_c(h, c_w_in[j], c_pos_k[j], c_pos_v[j], c_wk1[j], c_wk2[j], c_wv1[j], c_wv2[j],
                         c_w_o[j], g, b, alpha, batch, seq)
        else:
            h = _mixer_d(h, d_w_in[j], d_lq1[j], d_lk1[j], d_lq2[j], d_lk2[j], d_subln[j], d_w_o[j],
                         g, b, alpha, i, batch, seq)
        y = _hier_moe(h, moe_w_group[i], moe_w_expert[i], moe_w_gate[i], moe_w_up[i], moe_w_down[i])
        h = _add_ln(h, y, ln_g[i, 1], ln_b[i, 1], alpha)
    return h.reshape(batch, seq, d)
```

```python
import functools
import math

import numpy as np
import jax
import jax.numpy as jnp
from jax import lax
from jax.experimental import pallas as pl
from jax.experimental.pallas import tpu as pltpu

F32 = jnp.float32
BF16 = jnp.bfloat16

HEAD_DIM = 64
ROPE_THETA = 500000.0
ROT_DIM = HEAD_DIM // 4
A_HEADS, A_KV_HEADS, A_WINDOW = 16, 4, 128
B_HEADS, B_Q_RANK, B_KV_RANK, B_NOPE, B_ROPE, B_V = 16, 384, 256, 64, 32, 64
C_HEADS, C_KV_HEADS = 16, 4
C_CMP_LEN, C_CMP_STRIDE, C_CMP_HIDDEN = 32, 16, 128
C_SEL_LEN, C_N_SEL, C_WINDOW = 64, 16, 512
D_HEADS, D_SUB = 8, 64
M_GROUPS, M_PER_GROUP, M_TOPK, M_HIDDEN = 4, 8, 2, 512
M_EXPERTS = M_GROUPS * M_PER_GROUP
N_MIXERS = 4
LN_EPS = 1e-5
RMS_EPS = 1e-6

LANES = 128
NEG = -1e30
MAX_FLOOR = -1e20
VMEM_LIMIT = 48 * 1024 * 1024

PROJ_ROWS = 256
LN_ROWS = 512
MOE_ROWS = 256
FLASH_ROWS = 512


def _params(*sem):
    return pltpu.CompilerParams(dimension_semantics=sem, vmem_limit_bytes=VMEM_LIMIT)


def _dot(a, b):
    return jnp.dot(a, b, preferred_element_type=F32)


def _dot_t(a, b):
    return lax.dot_general(a, b, (((1,), (1,)), ((), ())), preferred_element_type=F32)


def _rope_tables(seq, rot_dim, period, off):
    half = rot_dim // 2
    inv_freq = 1.0 / (ROPE_THETA ** (jnp.arange(half, dtype=F32) * (2.0 / rot_dim)))
    ang = jnp.arange(seq, dtype=F32)[:, None] * inv_freq[None, :]
    cos, sin = jnp.cos(ang), jnp.sin(ang)
    lane = np.arange(LANES) % period - off
    first = (lane >= 0) & (lane < half)
    second = (lane >= half) & (lane < rot_dim)
    idx = np.where(first, lane, np.where(second, lane - half, 0))
    cg, sg = cos[:, idx], sin[:, idx]
    c = jnp.where(first | second, cg, 1.0)
    sa = jnp.where(first, -sg, 0.0)
    sb = jnp.where(second, sg, 0.0)
    return jnp.stack([c, sa, sb])


def _rope_block(x, tab_ref, half):
    return (x * tab_ref[0] + pltpu.roll(x, LANES - half, 1) * tab_ref[1]
            + pltpu.roll(x, half, 1) * tab_ref[2])


def _proj_kernel(x_ref, w_ref, tab_ref, *out_refs, segs, half):
    xb = x_ref[...].astype(BF16)
    for (start, width, kind, scale), o_ref in zip(segs, out_refs):
        acc = _dot(xb, w_ref[:, start:start + width])
        if kind == "rope":
            for c in range(width // LANES):
                y = _rope_block(acc[:, c * LANES:(c + 1) * LANES], tab_ref, half)
                if scale != 1.0:
                    y = y * scale
                o_ref[:, c * LANES:(c + 1) * LANES] = y.astype(o_ref.dtype)
        elif kind == "sigmoid":
            o_ref[...] = (1.0 / (1.0 + jnp.exp(-acc))).astype(o_ref.dtype)
        else:
            o_ref[...] = acc.astype(o_ref.dtype)


def _proj(x, w, tab, segs, dtypes, seq, half):
    t, k = x.shape
    n = w.shape[1]
    tm = PROJ_ROWS
    spb = seq // tm
    out_shape = [jax.ShapeDtypeStruct((t, s[1]), d) for s, d in zip(segs, dtypes)]
    return pl.pallas_call(
        functools.partial(_proj_kernel, segs=tuple(segs), half=half),
        grid=(t // tm,),
        in_specs=[pl.BlockSpec((tm, k), lambda i: (i, 0)),
                  pl.BlockSpec((k, n), lambda i: (0, 0)),
                  pl.BlockSpec((3, tm, LANES), lambda i: (0, i % spb, 0))],
        out_specs=[pl.BlockSpec((tm, s[1]), lambda i: (i, 0)) for s in segs],
        out_shape=out_shape,
        compiler_params=_params("parallel"),
        name="proj",
    )(x, w, tab)


def _layer_norm(z, g, b):
    mu = jnp.mean(z, -1, keepdims=True)
    zc = z - mu
    var = jnp.mean(zc * zc, -1, keepdims=True)
    return zc * lax.rsqrt(var + LN_EPS) * g + b


def _outproj_ln_kernel(o_ref, w_ref, h_ref, g_ref, b_ref, out_ref, *, alpha):
    y = _dot(o_ref[...], w_ref[...])
    out_ref[...] = _layer_norm(alpha * h_ref[...] + y, g_ref[...], b_ref[...])


def _outproj_ln(o, w, h, g, b, alpha):
    t, k = o.shape
    d = w.shape[1]
    tm = LN_ROWS
    return pl.pallas_call(
        functools.partial(_outproj_ln_kernel, alpha=alpha),
        grid=(t // tm,),
        in_specs=[pl.BlockSpec((tm, k), lambda i: (i, 0)),
                  pl.BlockSpec((k, d), lambda i: (0, 0)),
                  pl.BlockSpec((tm, d), lambda i: (i, 0)),
                  pl.BlockSpec((1, d), lambda i: (0, 0)),
                  pl.BlockSpec((1, d), lambda i: (0, 0))],
        out_specs=pl.BlockSpec((tm, d), lambda i: (i, 0)),
        out_shape=jax.ShapeDtypeStruct((t, d), F32),
        compiler_params=_params("parallel"),
        name="outproj_ln",
    )(o, w, h, g.reshape(1, d), b.reshape(1, d))


def _add_ln_kernel(h_ref, y0_ref, y1_ref, g_ref, b_ref, out_ref, *, alpha):
    y = y0_ref[...] + y1_ref[...]
    out_ref[...] = _layer_norm(alpha * h_ref[...] + y, g_ref[...], b_ref[...])


def _add_ln(h, y, g, b, alpha):
    t, d = h.shape
    tm = LN_ROWS
    nt = t // tm
    return pl.pallas_call(
        functools.partial(_add_ln_kernel, alpha=alpha),
        grid=(nt,),
        in_specs=[pl.BlockSpec((tm, d), lambda i: (i, 0)),
                  pl.BlockSpec((tm, d), lambda i: (i, 0)),
                  pl.BlockSpec((tm, d), lambda i: (nt + i, 0)),
                  pl.BlockSpec((1, d), lambda i: (0, 0)),
                  pl.BlockSpec((1, d), lambda i: (0, 0))],
        out_specs=pl.BlockSpec((tm, d), lambda i: (i, 0)),
        out_shape=jax.ShapeDtypeStruct((t, d), F32),
        compiler_params=_params("parallel"),
        name="add_ln",
    )(h, y, y, g.reshape(1, d), b.reshape(1, d))


def _stack_group_queries(q_ref, rows):
    lo = lax.broadcasted_iota(jnp.int32, (rows, LANES), 1) < HEAD_DIM
    qa, qb = q_ref[:, :LANES], q_ref[:, LANES:]
    z = jnp.zeros_like(qa)
    return jnp.concatenate([jnp.where(lo, qa, z), jnp.where(lo, z, qa),
                            jnp.where(lo, qb, z), jnp.where(lo, z, qb)], axis=0)


def _pair(lo_val, hi_val):
    lo = lax.broadcasted_iota(jnp.int32, lo_val.shape, 1) < HEAD_DIM
    return jnp.where(lo, lo_val, hi_val)


def _transpose_chunks(src_ref, dst_ref, chunk):
    for c in range(src_ref.shape[0] // chunk):
        dst_ref[c] = src_ref[c * chunk:(c + 1) * chunk, :].astype(F32).T.astype(dst_ref.dtype)


def _softmax_update_t(s, valid, m, l, scale=None):
    if valid is not None:
        s = jnp.where(valid, s, NEG)
    m_new = jnp.maximum(m, jnp.max(s, 0, keepdims=True))
    if scale is None:
        a = jnp.exp(m - m_new)
        p = jnp.exp(s - m_new)
    else:
        a = jnp.exp((m - m_new) * scale)
        p = jnp.exp((s - m_new) * scale)
    return m_new, a * l + jnp.sum(p, 0, keepdims=True), a, p.astype(BF16)


def _flash_scratch(n_chains, dv, m_cols, tk):
    return [pltpu.VMEM((n_chains, 3, 1, m_cols), F32), pltpu.VMEM((n_chains, dv, m_cols), F32),
            pltpu.VMEM((n_chains, 2, tk, m_cols), F32), pltpu.VMEM((n_chains, 2, tk, m_cols), BF16)]


def _flash_causal_t(score_fn, value_fn, n_chains, qi, causal, state, scale=None):
    st_ref, acc_ref, s_ref, p_ref = state
    for ch in range(n_chains):
        st_ref[ch, 0] = jnp.full(st_ref.shape[2:], MAX_FLOOR, F32)
        st_ref[ch, 1] = jnp.zeros(st_ref.shape[2:], F32)
        st_ref[ch, 2] = jnp.ones(st_ref.shape[2:], F32)
        acc_ref[ch] = jnp.zeros(acc_ref.shape[1:], F32)
        p_ref[ch, 1] = jnp.zeros(p_ref.shape[2:], BF16)
        s_ref[ch, 0] = score_fn(ch, 0)

    def half(c, cur, valid, last):
        nxt = 1 - cur
        for ch in range(n_chains):
            m, l, a, p = _softmax_update_t(s_ref[ch, cur], valid, st_ref[ch, 0], st_ref[ch, 1], scale)
            if not last:
                s_ref[ch, nxt] = score_fn(ch, c + 1)
            acc = st_ref[ch, 2] * acc_ref[ch] + _dot(value_fn(ch, jnp.maximum(c - 1, 0)), p_ref[ch, nxt])
            if last:
                acc = a * acc + _dot(value_fn(ch, c), p)
            else:
                p_ref[ch, cur] = p
                st_ref[ch, 2] = a
            acc_ref[ch] = acc
            st_ref[ch, 0] = m
            st_ref[ch, 1] = l

    def pair(j, carry):
        half(2 * j, 0, None, False)
        half(2 * j + 1, 1, None, False)
        return carry

    lax.fori_loop(0, qi // 2, pair, 0)

    @pl.when(qi % 2 == 1)
    def _():
        half(qi - 1, 0, None, False)
        half(qi, 1, causal, True)

    @pl.when(qi % 2 == 0)
    def _():
        half(qi, 0, causal, True)

    return [(st_ref[ch, 1], acc_ref[ch]) for ch in range(n_chains)]


def _swa_kernel(sink_ref, q_ref, kp_ref, kc_ref, vp_ref, vc_ref, o_ref):
    n = pl.program_id(1)
    blk = A_WINDOW
    i = lax.broadcasted_iota(jnp.int32, (blk, 2 * blk), 0)
    j = lax.broadcasted_iota(jnp.int32, (blk, 2 * blk), 1)
    dist = blk + i - j
    valid = (dist >= 0) & (dist < A_WINDOW) & ((n - 1) * blk + j >= 0)
    group = A_HEADS // A_KV_HEADS
    for g in range(A_KV_HEADS):
        qs = _stack_group_queries(q_ref.at[:, g * 2 * LANES:(g + 1) * 2 * LANES], blk)
        k = jnp.concatenate([kp_ref[:, g * LANES:(g + 1) * LANES],
                             kc_ref[:, g * LANES:(g + 1) * LANES]], axis=0)
        v = jnp.concatenate([vp_ref[:, g * LANES:(g + 1) * LANES],
                             vc_ref[:, g * LANES:(g + 1) * LANES]], axis=0)
        s = _dot_t(qs, k)
        ps = []
        for r in range(group):
            sink = sink_ref[g * group + r]
            sr = jnp.where(valid, s[r * blk:(r + 1) * blk], NEG)
            m = jnp.maximum(jnp.max(sr, -1, keepdims=True), sink)
            e = jnp.where(valid, jnp.exp(sr - m), 0.0)
            p = e / (jnp.sum(e, -1, keepdims=True) + jnp.exp(sink - m))
            ps.append(p.astype(BF16))
        o = _dot(jnp.concatenate(ps, axis=0), v)
        o_ref[:, g * 2 * LANES:g * 2 * LANES + LANES] = _pair(o[:blk], o[blk:2 * blk]).astype(o_ref.dtype)
        o_ref[:, g * 2 * LANES + LANES:(g + 1) * 2 * LANES] = _pair(
            o[2 * blk:3 * blk], o[3 * blk:]).astype(o_ref.dtype)


def _swa_attention(q, kd, vd, sinks, batch, seq):
    t = q.shape[0]
    blk = A_WINDOW
    nb = seq // blk
    qd = A_HEADS * HEAD_DIM
    kw = A_KV_HEADS * LANES
    cur = lambda b, n, s: (b * nb + n, 0)
    prev = lambda b, n, s: (b * nb + jnp.maximum(n - 1, 0), 0)
    return pl.pallas_call(
        _swa_kernel,
        grid_spec=pltpu.PrefetchScalarGridSpec(
            num_scalar_prefetch=1,
            grid=(batch, nb),
            in_specs=[pl.BlockSpec((blk, qd), cur),
                      pl.BlockSpec((blk, kw), prev), pl.BlockSpec((blk, kw), cur),
                      pl.BlockSpec((blk, kw), prev), pl.BlockSpec((blk, kw), cur)],
            out_specs=pl.BlockSpec((blk, qd), cur)),
        out_shape=jax.ShapeDtypeStruct((t, qd), BF16),
        compiler_params=_params("parallel", "parallel"),
        name="swa_attention",
    )(sinks.astype(F32), q, kd, kd, vd, vd)


def _dup_cols(w, heads):
    k = w.shape[0]
    w4 = w.reshape(k, heads, 1, HEAD_DIM)
    return jnp.broadcast_to(w4, (k, heads, 2, HEAD_DIM)).reshape(k, heads * 2 * HEAD_DIM)


def _mixer_a(h, w_in, sinks, w_o, ln_g, ln_b, alpha, batch, seq):
    qd, kd = A_HEADS * HEAD_DIM, A_KV_HEADS * HEAD_DIM
    w = jnp.concatenate([w_in[:, :qd], _dup_cols(w_in[:, qd:qd + kd], A_KV_HEADS),
                         _dup_cols(w_in[:, qd + kd:], A_KV_HEADS)], axis=1).astype(BF16)
    tab = _rope_tables(seq, ROT_DIM, HEAD_DIM, 0)
    segs = [(0, qd, "rope", HEAD_DIM ** -0.5), (qd, 2 * kd, "rope", 1.0), (qd + 2 * kd, 2 * kd, "plain", 1.0)]
    q, k2, v2 = _proj(h, w, tab, segs, [BF16, BF16, BF16], seq, ROT_DIM // 2)
    o = _swa_attention(q, k2, v2, sinks, batch, seq)
    return _outproj_ln(o, w_o.astype(BF16), h, ln_g, ln_b, alpha)


def _mla_proj_kernel(x_ref, wd_ref, qn_ref, kvn_ref, wq_ref, wk_ref, wv_ref, tab_ref,
                     q_ref, k_ref, v_ref):
    half = B_ROPE // 2
    c = _dot(x_ref[...].astype(BF16), wd_ref[...])
    cq, ckv = c[:, :B_Q_RANK], c[:, B_Q_RANK:B_Q_RANK + B_KV_RANK]
    kr = _rope_block(c[:, B_Q_RANK + B_KV_RANK:], tab_ref, half)
    cqn = (cq * lax.rsqrt(jnp.mean(cq * cq, -1, keepdims=True) + RMS_EPS) * qn_ref[...]).astype(BF16)
    ckvn = (ckv * lax.rsqrt(jnp.mean(ckv * ckv, -1, keepdims=True) + RMS_EPS) * kvn_ref[...]).astype(BF16)
    q = _dot(cqn, wq_ref[...])
    kk = _dot(ckvn, wk_ref[...])
    for hd in range(B_HEADS):
        sl = slice(hd * LANES, (hd + 1) * LANES)
        q_ref[:, sl] = _rope_block(q[:, sl], tab_ref, half).astype(q_ref.dtype)
        k_ref[:, sl] = (kk[:, sl] + kr).astype(k_ref.dtype)
    v_ref[...] = _dot(ckvn, wv_ref[...]).astype(v_ref.dtype)


def _mla_attn_kernel(q_ref, k_ref, v_ref, o_ref, vt_ref, *state, tq, scale):
    qi = pl.program_id(2)

    @pl.when(qi == 0)
    def _():
        _transpose_chunks(v_ref, vt_ref, tq)

    q_t = [q_ref[:, hh * LANES:(hh + 1) * LANES].astype(F32).T.astype(BF16) for hh in range(2)]
    key = lax.broadcasted_iota(jnp.int32, (tq, tq), 0)
    qry = lax.broadcasted_iota(jnp.int32, (tq, tq), 1)
    causal = key <= qry

    def scores(hh, c):
        return _dot(k_ref[pl.ds(pl.multiple_of(c * tq, tq), tq), hh * LANES:(hh + 1) * LANES], q_t[hh])

    def values(hh, c):
        return vt_ref[c, hh * B_V:(hh + 1) * B_V, :]

    res = _flash_causal_t(scores, values, 2, qi, causal, state, scale)
    o_t = jnp.concatenate([acc / l for l, acc in res], axis=0)
    o_ref[...] = o_t.T.astype(o_ref.dtype)


def _mixer_b(h, w_down, q_norm, kv_norm, w_uq, w_ukv, w_o, ln_g, ln_b, alpha, batch, seq):
    t, d = h.shape
    dq = B_NOPE + B_ROPE
    pad = LANES - dq
    wd = jnp.concatenate([w_down[:, :B_Q_RANK + B_KV_RANK], jnp.zeros((d, B_NOPE), F32),
                          w_down[:, B_Q_RANK + B_KV_RANK:], jnp.zeros((d, pad), F32)], axis=1).astype(BF16)
    wq = jnp.pad(w_uq.reshape(B_Q_RANK, B_HEADS, dq), ((0, 0), (0, 0), (0, pad))
                 ).reshape(B_Q_RANK, B_HEADS * LANES).astype(BF16)
    wkv = w_ukv.reshape(B_KV_RANK, B_HEADS, B_NOPE + B_V)
    wk = jnp.pad(wkv[:, :, :B_NOPE], ((0, 0), (0, 0), (0, LANES - B_NOPE))
                 ).reshape(B_KV_RANK, B_HEADS * LANES).astype(BF16)
    wv = wkv[:, :, B_NOPE:].reshape(B_KV_RANK, B_HEADS * B_V).astype(BF16)
    tab = _rope_tables(seq, B_ROPE, LANES, B_NOPE)
    tm = PROJ_ROWS
    spb = seq // tm
    full = lambda a: pl.BlockSpec(a.shape, lambda i: (0,) * a.ndim)
    qn, kvn = q_norm.reshape(1, -1), kv_norm.reshape(1, -1)
    q, k, v = pl.pallas_call(
        _mla_proj_kernel,
        grid=(t // tm,),
        in_specs=[pl.BlockSpec((tm, d), lambda i: (i, 0)), full(wd), full(qn), full(kvn),
                  full(wq), full(wk), full(wv),
                  pl.BlockSpec((3, tm, LANES), lambda i: (0, i % spb, 0))],
        out_specs=[pl.BlockSpec((tm, B_HEADS * LANES), lambda i: (i, 0)),
                   pl.BlockSpec((tm, B_HEADS * LANES), lambda i: (i, 0)),
                   pl.BlockSpec((tm, B_HEADS * B_V), lambda i: (i, 0))],
        out_shape=[jax.ShapeDtypeStruct((t, B_HEADS * LANES), BF16),
                   jax.ShapeDtypeStruct((t, B_HEADS * LANES), BF16),
                   jax.ShapeDtypeStruct((t, B_HEADS * B_V), BF16)],
        compiler_params=_params("parallel"),
        name="mla_proj",
    )(h, wd, qn, kvn, wq, wk, wv, tab)

    tq = min(FLASH_ROWS, seq)
    nq = seq // tq
    o = pl.pallas_call(
        functools.partial(_mla_attn_kernel, tq=tq, scale=dq ** -0.5),
        grid=(batch, B_HEADS // 2, nq),
        in_specs=[pl.BlockSpec((tq, 2 * LANES), lambda b, p, i: (b * nq + i, p)),
                  pl.BlockSpec((seq, 2 * LANES), lambda b, p, i: (b, p)),
                  pl.BlockSpec((seq, LANES), lambda b, p, i: (b, p))],
        out_specs=pl.BlockSpec((tq, LANES), lambda b, p, i: (b * nq + i, p)),
        out_shape=jax.ShapeDtypeStruct((t, B_HEADS * B_V), BF16),
        scratch_shapes=[pltpu.VMEM((nq, LANES, tq), BF16)] + _flash_scratch(2, B_V, tq, tq),
        compiler_params=_params("parallel", "parallel", "arbitrary"),
        name="mla_attention",
    )(q, k, v)
    return _outproj_ln(o, w_o.astype(BF16), h, ln_g, ln_b, alpha)


def _gelu_tanh(x):
    return x * (0.5 * (1.0 + jnp.tanh(math.sqrt(2.0 / math.pi) * (x + 0.044715 * (x * x * x)))))


def _compress_kernel(x_ref, pe_ref, w1_ref, w2_ref, o_ref):
    n = x_ref.shape[0] // C_CMP_STRIDE
    a = b = None
    for l in range(C_CMP_STRIDE):
        y = x_ref[pl.ds(l, n, stride=C_CMP_STRIDE), :]
        ta = _dot((y + pe_ref[l:l + 1, :]).astype(BF16), w1_ref[l])
        tb = _dot((y + pe_ref[C_CMP_STRIDE + l:C_CMP_STRIDE + l + 1, :]).astype(BF16), w1_ref[C_CMP_STRIDE + l])
        a = ta if a is None else a + ta
        b = tb if b is None else b + tb
    hid = a + pltpu.roll(b, n - 1, 0)
    o_ref[0] = _dot(_gelu_tanh(hid).astype(BF16), w2_ref[...]).astype(o_ref.dtype)


def _compress(kvc, which, pe, w1, w2, batch, seq):
    n = seq // C_CMP_STRIDE
    hp = LANES // HEAD_DIM
    pairs = C_KV_HEADS // hp
    eye = jnp.eye(hp, dtype=F32)
    w1bd = jnp.einsum("lij,gh->lgihj", w1.reshape(C_CMP_LEN, HEAD_DIM, C_CMP_HIDDEN), eye)
    w1bd = w1bd.reshape(C_CMP_LEN, LANES, hp * C_CMP_HIDDEN).astype(BF16)
    w2bd = jnp.einsum("ij,gh->gihj", jnp.concatenate([w2, w2], axis=1), eye)
    w2bd = w2bd.reshape(hp * C_CMP_HIDDEN, hp * LANES).astype(BF16)
    return pl.pallas_call(
        _compress_kernel,
        grid=(batch, pairs),
        in_specs=[pl.BlockSpec((seq, LANES), lambda i, j: (i, which * pairs + j)),
                  pl.BlockSpec((C_CMP_LEN, LANES), lambda i, j: (0, 0)),
                  pl.BlockSpec(w1bd.shape, lambda i, j: (0, 0, 0)),
                  pl.BlockSpec(w2bd.shape, lambda i, j: (0, 0))],
        out_specs=pl.BlockSpec((1, n, hp * LANES), lambda i, j: (i, 0, j)),
        out_shape=jax.ShapeDtypeStruct((batch, n, C_KV_HEADS * LANES), BF16),
        compiler_params=_params("parallel", "parallel"),
        name="nsa_compress",
    )(kvc, jnp.tile(pe, (1, hp)), w1bd, w2bd)


def _nsa_attn_kernel(q_ref, ks_ref, vs_ref, kw_ref, vw_ref, kc_ref, vc_ref, ov_ref, ex_ref, gt_ref,
                     o_ref, vst_ref, vwt_ref, vct_ref, *state, tq, tk, seq):
    qi = pl.program_id(2)
    q0 = qi * tq
    group = C_HEADS // C_KV_HEADS
    n_sel_blocks = seq // C_SEL_LEN
    nc = kc_ref.shape[1]

    @pl.when(qi == 0)
    def _():
        _transpose_chunks(vs_ref, vst_ref, tk)
        _transpose_chunks(vw_ref, vwt_ref, tq)
        vct_ref[...] = vc_ref[0].astype(F32).T.astype(vct_ref.dtype)

    top = lax.broadcasted_iota(jnp.int32, (LANES, tq), 0) < HEAD_DIM
    qa_t = q_ref[:, :LANES].astype(F32).T
    qb_t = q_ref[:, LANES:].astype(F32).T
    qs_t = jnp.concatenate([jnp.where(top, qa_t, 0.0), jnp.where(top, 0.0, qa_t),
                            jnp.where(top, qb_t, 0.0), jnp.where(top, 0.0, qb_t)], axis=1).astype(BF16)
    rep = lambda a: jnp.concatenate([a] * group, axis=1)

    def masked_softmax(s, valid):
        s = jnp.where(valid, s, NEG)
        e = jnp.exp(s - jnp.maximum(jnp.max(s, 0, keepdims=True), MAX_FLOOR))
        return e / jnp.maximum(jnp.sum(e, 0, keepdims=True), 1e-30)

    t_c = q0 + lax.broadcasted_iota(jnp.int32, (nc, tq), 1)
    n_c = lax.broadcasted_iota(jnp.int32, (nc, tq), 0)
    c_valid = rep(n_c * C_CMP_STRIDE + (C_CMP_LEN - 1) <= t_c)
    p_cmp = masked_softmax(_dot(kc_ref[0], qs_t), c_valid).astype(BF16)
    o_cmp = _dot(vct_ref[...], p_cmp)
    imp4 = _dot(ov_ref[...], p_cmp)
    imp = imp4[:n_sel_blocks, :tq]
    for r in range(1, group):
        imp = imp + imp4[:n_sel_blocks, r * tq:(r + 1) * tq]

    blk = lax.broadcasted_iota(jnp.int32, (n_sel_blocks, tq), 0)
    cur = (q0 + lax.broadcasted_iota(jnp.int32, (n_sel_blocks, tq), 1)) // C_SEL_LEN
    imp = jnp.where((blk == 0) | (blk == cur) | (blk == cur - 1), jnp.inf, imp)
    imp = jnp.where(blk <= cur, imp, -jnp.inf)
    rank = jnp.zeros((n_sel_blocks, tq), F32)
    for i in range(n_sel_blocks):
        r_i = imp[i:i + 1, :]
        rank = rank + jnp.where(blk > i, jnp.where(r_i >= imp, 1.0, 0.0), jnp.where(r_i > imp, 1.0, 0.0))
    sel = jnp.where(rank < min(C_N_SEL, n_sel_blocks), 0.0, NEG)
    sel = jnp.concatenate([sel, jnp.zeros((LANES - n_sel_blocks, tq), F32)], axis=0)
    qs_aug = jnp.concatenate([qs_t, rep(sel).astype(BF16)], axis=0)

    span = C_WINDOW + tq
    start = pl.multiple_of(jnp.maximum(q0 - C_WINDOW, 0), tq)
    t_w = q0 + lax.broadcasted_iota(jnp.int32, (span, tq), 1)
    dist = t_w - (start + lax.broadcasted_iota(jnp.int32, (span, tq), 0))
    w_valid = rep((dist >= 0) & (dist < C_WINDOW))
    p_win = masked_softmax(_dot(kw_ref[pl.ds(start, span), :], qs_t), w_valid).astype(BF16)
    o_win = _dot(vwt_ref[start // tq], p_win[:tq])
    for cb in range(1, span // tq):
        o_win = o_win + _dot(vwt_ref[start // tq + cb], p_win[cb * tq:(cb + 1) * tq])

    t_k = q0 + lax.broadcasted_iota(jnp.int32, (tk, tq), 1)
    k_k = lax.broadcasted_iota(jnp.int32, (tk, tq), 0)

    def sel_scores(ch, c):
        base = pl.multiple_of(c * tk, tk)
        k_aug = jnp.concatenate([ks_ref[pl.ds(base, tk), :], ex_ref[c]], axis=1)
        return jnp.where(rep(base + k_k <= t_k), _dot(k_aug, qs_aug), NEG)

    (l, acc), = _flash_causal_t(sel_scores, lambda ch, c: vst_ref[c], 1, (q0 + tq - 1) // tk, None, state)
    o_slc = acc / jnp.maximum(l, 1e-30)

    g_t = gt_ref[...].T
    outs = []
    for r in range(group):
        cols = slice(r * tq, (r + 1) * tq)
        outs.append(o_cmp[:, cols] * g_t[3 * r:3 * r + 1] + o_slc[:, cols] * g_t[3 * r + 1:3 * r + 2]
                    + o_win[:, cols] * g_t[3 * r + 2:3 * r + 3])
    o_ref[:, :LANES] = jnp.where(top, outs[0], outs[1]).T.astype(o_ref.dtype)
    o_ref[:, LANES:] = jnp.where(top, outs[2], outs[3]).T.astype(o_ref.dtype)


def _mixer_c(h, w_in, pos_k, pos_v, wk1, wk2, wv1, wv2, w_o, ln_g, ln_b, alpha, batch, seq):
    t, d = h.shape
    qd, kd = C_HEADS * HEAD_DIM, C_KV_HEADS * HEAD_DIM
    group = C_HEADS // C_KV_HEADS
    kv = lambda i: w_in[:, qd + i * kd:qd + (i + 1) * kd]
    wg = jnp.pad(w_in[:, qd + 6 * kd:].reshape(d, C_KV_HEADS, group * 3),
                 ((0, 0), (0, 0), (0, LANES - group * 3))).reshape(d, C_KV_HEADS * LANES)
    w = jnp.concatenate([w_in[:, :qd], _dup_cols(kv(2), C_KV_HEADS), _dup_cols(kv(4), C_KV_HEADS),
                         _dup_cols(kv(3), C_KV_HEADS), _dup_cols(kv(5), C_KV_HEADS),
                         kv(0), kv(1), wg], axis=1).astype(BF16)
    tab = _rope_tables(seq, ROT_DIM, HEAD_DIM, 0)
    kw2 = 2 * kd
    segs = [(0, qd, "rope", HEAD_DIM ** -0.5), (qd, 2 * kw2, "rope", 1.0), (qd + 2 * kw2, 2 * kw2, "plain", 1.0),
            (qd + 4 * kw2, 2 * kd, "plain", 1.0), (qd + 4 * kw2 + 2 * kd, C_KV_HEADS * LANES, "sigmoid", 1.0)]
    q, ksw, vsw, kvc, gates = _proj(h, w, tab, segs, [BF16, BF16, BF16, F32, F32], seq, ROT_DIM // 2)
    k_cmp = _compress(kvc, 0, pos_k, wk1, wk2, batch, seq)
    v_cmp = _compress(kvc, 1, pos_v, wv1, wv2, batch, seq)

    nc = seq // C_CMP_STRIDE
    nsb = seq // C_SEL_LEN
    tq = 128
    tk = min(512, seq)
    nq = seq // tq
    cs = np.arange(nc)[None, :] * C_CMP_STRIDE
    ss = np.arange(LANES)[:, None] * C_SEL_LEN
    overlap = ((cs <= ss + C_SEL_LEN - 1) & (ss <= cs + C_CMP_LEN - 1) & (np.arange(LANES)[:, None] < nsb))
    overlap = jnp.asarray(overlap, BF16)
    key_blk = (np.arange(seq) // C_SEL_LEN).reshape(seq // tk, tk, 1)
    expand = jnp.asarray(key_blk == np.arange(LANES)[None, None, :], BF16)

    per_bg = lambda width: pl.BlockSpec((seq, LANES), lambda b, g, i, width=width: (b, width + g))
    o = pl.pallas_call(
        functools.partial(_nsa_attn_kernel, tq=tq, tk=tk, seq=seq),
        grid=(batch, C_KV_HEADS, nq),
        in_specs=[pl.BlockSpec((tq, 2 * LANES), lambda b, g, i: (b * nq + i, g)),
                  per_bg(0), per_bg(0), per_bg(C_KV_HEADS), per_bg(C_KV_HEADS),
                  pl.BlockSpec((1, nc, LANES), lambda b, g, i: (b, 0, g)),
                  pl.BlockSpec((1, nc, LANES), lambda b, g, i: (b, 0, g)),
                  pl.BlockSpec((LANES, nc), lambda b, g, i: (0, 0)),
                  pl.BlockSpec((seq // tk, tk, LANES), lambda b, g, i: (0, 0, 0)),
                  pl.BlockSpec((tq, LANES), lambda b, g, i: (b * nq + i, g))],
        out_specs=pl.BlockSpec((tq, 2 * LANES), lambda b, g, i: (b * nq + i, g)),
        out_shape=jax.ShapeDtypeStruct((t, qd), BF16),
        scratch_shapes=[pltpu.VMEM((seq // tk, LANES, tk), BF16), pltpu.VMEM((seq // tq, LANES, tq), BF16),
                        pltpu.VMEM((LANES, nc), BF16)] + _flash_scratch(1, LANES, group * tq, tk),
        compiler_params=_params("parallel", "parallel", "arbitrary"),
        name="nsa_attention",
    )(q, ksw, vsw, ksw, vsw, k_cmp, v_cmp, overlap, expand, gates)
    return _outproj_ln(o, w_o.astype(BF16), h, ln_g, ln_b, alpha)


def _diff_attn_kernel(q_ref, k_ref, v_ref, lam_ref, sub_ref, o_ref, vt_ref, *state, tq, lam_init):
    qi = pl.program_id(2)

    n_heads = q_ref.shape[1] // LANES

    @pl.when(qi == 0)
    def _():
        for hd in range(n_heads):
            _transpose_chunks(v_ref.at[:, hd * LANES:(hd + 1) * LANES], vt_ref.at[hd], tq)

    lam = (jnp.exp(jnp.sum(lam_ref[0:1, :] * lam_ref[1:2, :], -1, keepdims=True))
           - jnp.exp(jnp.sum(lam_ref[2:3, :] * lam_ref[3:4, :], -1, keepdims=True)) + lam_init)
    top = lax.broadcasted_iota(jnp.int32, (LANES, tq), 0) < D_SUB
    qs_t = []
    for hd in range(n_heads):
        q_t = q_ref[:, hd * LANES:(hd + 1) * LANES].astype(F32).T
        qs_t.append(jnp.concatenate([jnp.where(top, q_t, 0.0), jnp.where(top, 0.0, q_t)], axis=1).astype(BF16))
    key = lax.broadcasted_iota(jnp.int32, (tq, tq), 0)
    qry = lax.broadcasted_iota(jnp.int32, (tq, tq), 1)
    causal = jnp.concatenate([key <= qry] * 2, axis=1)

    def scores(hd, c):
        return _dot(k_ref[pl.ds(pl.multiple_of(c * tq, tq), tq), hd * LANES:(hd + 1) * LANES], qs_t[hd])

    def values(hd, c):
        return vt_ref[hd, c]

    res = _flash_causal_t(scores, values, n_heads, qi, causal, state)
    for hd in range(n_heads):
        l, acc = res[hd]
        o = (acc / l).T
        o = o[:tq] - lam * o[tq:]
        o = o * lax.rsqrt(jnp.mean(o * o, -1, keepdims=True) + RMS_EPS) * sub_ref[...]
        o_ref[:, hd * LANES:(hd + 1) * LANES] = (o * (1.0 - lam_init)).astype(o_ref.dtype)


def _mixer_d(h, w_in, lq1, lk1, lq2, lk2, subln, w_o, ln_g, ln_b, alpha, layer_idx, batch, seq):
    t, d = h.shape
    qd = D_HEADS * 2 * D_SUB
    tab = _rope_tables(seq, ROT_DIM, D_SUB, 0)
    segs = [(0, qd, "rope", D_SUB ** -0.5), (qd, qd, "rope", 1.0), (2 * qd, qd, "plain", 1.0)]
    q, k, v = _proj(h, w_in.astype(BF16), tab, segs, [BF16, BF16, BF16], seq, ROT_DIM // 2)
    lam_init = 0.8 - 0.6 * math.exp(-0.3 * layer_idx)
    lam_in = jnp.stack([lq1, lk1, lq2, lk2]).astype(F32)
    tq = min(FLASH_ROWS, seq)
    nq = seq // tq
    hps = 2
    o = pl.pallas_call(
        functools.partial(_diff_attn_kernel, tq=tq, lam_init=lam_init),
        grid=(batch, D_HEADS // hps, nq),
        in_specs=[pl.BlockSpec((tq, hps * LANES), lambda b, hd, i: (b * nq + i, hd)),
                  pl.BlockSpec((seq, hps * LANES), lambda b, hd, i: (b, hd)),
                  pl.BlockSpec((seq, hps * LANES), lambda b, hd, i: (b, hd)),
                  pl.BlockSpec((4, D_SUB), lambda b, hd, i: (0, 0)),
                  pl.BlockSpec((1, 2 * D_SUB), lambda b, hd, i: (0, 0))],
        out_specs=pl.BlockSpec((tq, hps * LANES), lambda b, hd, i: (b * nq + i, hd)),
        out_shape=jax.ShapeDtypeStruct((t, qd), BF16),
        scratch_shapes=[pltpu.VMEM((hps, nq, LANES, tq), BF16)] + _flash_scratch(hps, LANES, 2 * tq, tq),
        compiler_params=_params("parallel", "parallel", "arbitrary"),
        name="diff_attention",
    )(q, k, v, lam_in, subln.reshape(1, -1))
    return _outproj_ln(o, w_o.astype(BF16), h, ln_g, ln_b, alpha)


def _router_kernel(x_ref, w_ref, o_ref):
    xb = x_ref[...].astype(BF16)
    logits = _dot(xb, w_ref[...])
    lg = logits[:, :LANES].T[:8]
    le = logits[:, LANES:].T[:M_EXPERTS]
    far = 4 * LANES

    def softmax(x, valid):
        m = jnp.max(jnp.where(valid, x, NEG), 0, keepdims=True)
        e = jnp.where(valid, jnp.exp(x - m), 0.0)
        return e / jnp.sum(e, 0, keepdims=True)

    def first_max(p, valid, row):
        top = jnp.max(jnp.where(valid, p, -1.0), 0, keepdims=True)
        idx = jnp.min(jnp.where(valid & (p == top), row, far), 0, keepdims=True)
        return top, idx

    g_row = lax.broadcasted_iota(jnp.int32, lg.shape, 0)
    e_row = lax.broadcasted_iota(jnp.int32, le.shape, 0)
    g_valid = g_row < M_GROUPS
    g_w, g_idx = first_max(softmax(lg, g_valid), g_valid, g_row)
    e_valid = (e_row >= g_idx * M_PER_GROUP) & (e_row < (g_idx + 1) * M_PER_GROUP)
    pe = softmax(le, e_valid)
    w0, i0 = first_max(pe, e_valid, e_row)
    w1, i1 = first_max(pe, e_valid & (e_row != i0), e_row)
    tot = w0 + w1
    out = jnp.where(g_row == 0, i0.astype(F32), 0.0)
    out = jnp.where(g_row == 1, i1.astype(F32), out)
    out = jnp.where(g_row == 2, g_w * w0 / tot, out)
    out = jnp.where(g_row == 3, g_w * w1 / tot, out)
    o_ref[...] = out


def _moe_kernel(eid_ref, used_ref, src0_ref, src_ref, dstp_ref, wt_ref, h_hbm, wg_ref, wu_ref, wd_ref, out_hbm,
                xbuf, ybuf, wgb, wub, wdb, gsem, ssem, *, tm):
    t = pl.program_id(0)
    used = used_ref[0]

    def start_gather(idx_ref, s):
        for r in range(tm):
            pltpu.make_async_copy(h_hbm.at[pl.ds(idx_ref[0, 0, r], 1), :],
                                  xbuf.at[s, pl.ds(r, 1), :], gsem.at[s]).start()

    def start_scatter(s):
        for r in range(tm):
            pltpu.make_async_copy(ybuf.at[s, pl.ds(r, 1), :],
                                  out_hbm.at[pl.ds(dstp_ref[0, 0, r], 1), :], ssem.at[s]).start()

    def wait_gather(s):
        pltpu.make_async_copy(h_hbm.at[pl.ds(0, tm), :], xbuf.at[s], gsem.at[s]).wait()

    def wait_scatter(s):
        pltpu.make_async_copy(ybuf.at[s], out_hbm.at[pl.ds(0, tm), :], ssem.at[s]).wait()

    @pl.when(t == 0)
    def _():
        start_gather(src0_ref, 0)
        ybuf[1] = jnp.zeros(ybuf.shape[1:], F32)
        fill = pltpu.make_async_copy(ybuf.at[1], out_hbm.at[pl.ds(out_hbm.shape[0] - 2 * tm, tm), :], ssem.at[0])
        fill.start()
        fill.wait()

    @pl.when((t < used) & ((t == 0) | (eid_ref[t] != eid_ref[jnp.maximum(t - 1, 0)])))
    def _():
        wgb[...] = wg_ref[0].astype(BF16)
        wub[...] = wu_ref[0].astype(BF16)
        wdb[...] = wd_ref[0].astype(BF16)

    for s in range(2):
        @pl.when((t < used) & (t % 2 == s))
        def _(s=s):
            wait_gather(s)

            @pl.when(t >= 1)
            def _():
                wait_scatter(s)

            start_scatter(1 - s)
            start_gather(src_ref, 1 - s)
            xb = xbuf[s].astype(BF16)
            gate = _dot(xb, wgb[...])
            up = _dot(xb, wub[...])
            hid = (gate * (1.0 / (1.0 + jnp.exp(-gate))) * up).astype(BF16)
            ybuf[s] = _dot(hid, wdb[...]) * wt_ref[...]

        @pl.when((t == used) & (t % 2 == s))
        def _(s=s):
            start_scatter(1 - s)
            wait_scatter(s)
            wait_scatter(1 - s)
            wait_gather(s)


def _hier_moe(h, w_group, w_expert, w_gate, w_up, w_down):
    t, d = h.shape
    tm = MOE_ROWS
    wr = jnp.concatenate([jnp.pad(w_group, ((0, 0), (0, LANES - M_GROUPS))),
                          jnp.pad(w_expert, ((0, 0), (0, LANES - M_EXPERTS)))], axis=1).astype(BF16)
    routed = pl.pallas_call(
        _router_kernel,
        grid=(t // LN_ROWS,),
        in_specs=[pl.BlockSpec((LN_ROWS, d), lambda i: (i, 0)),
                  pl.BlockSpec((d, 2 * LANES), lambda i: (0, 0))],
        out_specs=pl.BlockSpec((8, LN_ROWS), lambda i: (0, i)),
        out_shape=jax.ShapeDtypeStruct((8, t), F32),
        compiler_params=_params("parallel"),
        name="moe_router",
    )(h, wr)

    n_rows = t * M_TOPK
    n_tiles = n_rows // tm + M_EXPERTS
    expert = routed[:M_TOPK].T.astype(jnp.int32).reshape(-1)
    weight = routed[M_TOPK:2 * M_TOPK].T.reshape(-1)
    order = jnp.argsort(expert).astype(jnp.int32)
    sizes = jnp.sum(expert[:, None] == jnp.arange(M_EXPERTS, dtype=jnp.int32)[None, :], axis=0, dtype=jnp.int32)
    tiles = (sizes + tm - 1) // tm
    tile_end = jnp.cumsum(tiles)
    seg_start = jnp.cumsum(sizes) - sizes
    used = tile_end[-1]
    tile_ids = jnp.arange(n_tiles, dtype=jnp.int32)
    tile_eid = jnp.sum(jnp.minimum(tile_ids, used - 1)[:, None] >= tile_end[None, :], axis=1).astype(jnp.int32)
    tile_first = (tile_end - tiles)[tile_eid]
    lane_r = jnp.arange(tm, dtype=jnp.int32)[None, :]
    offs = (tile_ids - tile_first)[:, None] * tm + lane_r
    valid = (offs < sizes[tile_eid][:, None]) & (tile_ids < used)[:, None]
    row = order[jnp.clip(seg_start[tile_eid][:, None] + offs, 0, n_rows - 1)]
    src = jnp.where(valid, row // M_TOPK, 0)
    trash = n_rows + (tile_ids % 2)[:, None] * tm + lane_r
    dst = jnp.where(valid, (row % M_TOPK) * t + row // M_TOPK, trash)
    dstp = jnp.concatenate([n_rows + tm + lane_r, dst], axis=0).reshape(n_tiles + 1, 1, tm)
    wrow = jnp.where(valid, weight[row], 0.0).reshape(n_tiles * tm, 1)
    src = src.reshape(n_tiles, 1, tm)

    wspec = lambda shape: pl.BlockSpec((1,) + shape, lambda i, eid, used: (eid[i], 0, 0))
    idx_spec = pl.BlockSpec((1, 1, tm), lambda i, eid, used: (i, 0, 0), memory_space=pltpu.SMEM)
    nxt_spec = pl.BlockSpec((1, 1, tm), lambda i, eid, used: (jnp.minimum(i + 1, n_tiles - 1), 0, 0),
                            memory_space=pltpu.SMEM)
    return pl.pallas_call(
        functools.partial(_moe_kernel, tm=tm),
        grid_spec=pltpu.PrefetchScalarGridSpec(
            num_scalar_prefetch=2,
            grid=(n_tiles,),
            in_specs=[idx_spec, nxt_spec, idx_spec,
                      pl.BlockSpec((tm, 1), lambda i, eid, used: (i, 0)),
                      pl.BlockSpec(memory_space=pl.ANY),
                      wspec((d, M_HIDDEN)), wspec((d, M_HIDDEN)), wspec((M_HIDDEN, d))],
            out_specs=pl.BlockSpec(memory_space=pl.ANY),
            scratch_shapes=[pltpu.VMEM((2, tm, d), F32), pltpu.VMEM((2, tm, d), F32),
                            pltpu.VMEM((d, M_HIDDEN), BF16), pltpu.VMEM((d, M_HIDDEN), BF16),
                            pltpu.VMEM((M_HIDDEN, d), BF16),
                            pltpu.SemaphoreType.DMA((2,)), pltpu.SemaphoreType.DMA((2,))]),
        out_shape=jax.ShapeDtypeStruct((n_rows + 2 * tm, d), F32),
        compiler_params=_params("arbitrary"),
        name="moe_experts",
    )(tile_eid, used.reshape(1).astype(jnp.int32), src, src, dstp, wrow, h, w_gate, w_up, w_down)


def kernel(x, a_w_in, a_sinks, a_w_o, b_w_down, b_q_norm, b_kv_norm, b_w_uq, b_w_ukv, b_w_o, c_w_in, c_pos_k, c_pos_v, c_wk1, c_wk2, c_wv1, c_wv2, c_w_o, d_w_in, d_lq1, d_lk1, d_lq2, d_lk2, d_subln, d_w_o, moe_w_group, moe_w_expert, moe_w_gate, moe_w_up, moe_w_down, ln_g, ln_b):
    batch, seq, d = x.shape
    depth = ln_g.shape[0]
    alpha = (2 * depth) ** 0.25
    h = x.reshape(batch * seq, d)
    for i in range(depth):
        kind, j = i % N_MIXERS, i // N_MIXERS
        g, b = ln_g[i, 0], ln_b[i, 0]
        if kind == 0:
            h = _mixer_a(h, a_w_in[j], a_sinks[j], a_w_o[j], g, b, alpha, batch, seq)
        elif kind == 1:
            h = _mixer_b(h, b_w_down[j], b_q_norm[j], b_kv_norm[j], b_w_uq[j], b_w_ukv[j], b_w_o[j],
                         g, b, alpha, batch, seq)
        elif kind == 2:
            h = _mixer_c(h, c_w_in[j], c_pos_k[j], c_pos_v[j], c_wk1[j], c_wk2[j], c_wv1[j], c_wv2[j],
                         c_w_o[j], g, b, alpha, batch, seq)
        else:
            h = _mixer_d(h, d_w_in[j], d_lq1[j], d_lk1[j], d_lq2[j], d_lk2[j], d_subln[j], d_w_o[j],
                         g, b, alpha, i, batch, seq)
        y = _hier_moe(h, moe_w_group[i], moe_w_expert[i], moe_w_gate[i], moe_w_up[i], moe_w_down[i])
        h = _add_ln(h, y, ln_g[i, 1], ln_b[i, 1], alpha)
    return h.reshape(batch, seq, d)
```

```python
import functools
import math

import numpy as np
import jax
import jax.numpy as jnp
from jax import lax
from jax.experimental import pallas as pl
from jax.experimental.pallas import tpu as pltpu

F32 = jnp.float32
BF16 = jnp.bfloat16

HEAD_DIM = 64
ROPE_THETA = 500000.0
ROT_DIM = HEAD_DIM // 4
A_HEADS, A_KV_HEADS, A_WINDOW = 16, 4, 128
B_HEADS, B_Q_RANK, B_KV_RANK, B_NOPE, B_ROPE, B_V = 16, 384, 256, 64, 32, 64
C_HEADS, C_KV_HEADS = 16, 4
C_CMP_LEN, C_CMP_STRIDE, C_CMP_HIDDEN = 32, 16, 128
C_SEL_LEN, C_N_SEL, C_WINDOW = 64, 16, 512
D_HEADS, D_SUB = 8, 64
M_GROUPS, M_PER_GROUP, M_TOPK, M_HIDDEN = 4, 8, 2, 512
M_EXPERTS = M_GROUPS * M_PER_GROUP
N_MIXERS = 4
LN_EPS = 1e-5
RMS_EPS = 1e-6

LANES = 128
NEG = -1e30
MAX_FLOOR = -1e20
VMEM_LIMIT = 48 * 1024 * 1024

PROJ_ROWS = 256
LN_ROWS = 512
MOE_ROWS = 256
COMBINE_ROWS = 256
FLASH_ROWS = 512


def _params(*sem):
    return pltpu.CompilerParams(dimension_semantics=sem, vmem_limit_bytes=VMEM_LIMIT)


def _dot(a, b):
    return jnp.dot(a, b, preferred_element_type=F32)


def _dot_t(a, b):
    return lax.dot_general(a, b, (((1,), (1,)), ((), ())), preferred_element_type=F32)


def _rope_tables(seq, rot_dim, period, off):
    half = rot_dim // 2
    inv_freq = 1.0 / (ROPE_THETA ** (jnp.arange(half, dtype=F32) * (2.0 / rot_dim)))
    ang = jnp.arange(seq, dtype=F32)[:, None] * inv_freq[None, :]
    cos, sin = jnp.cos(ang), jnp.sin(ang)
    lane = np.arange(LANES) % period - off
    first = (lane >= 0) & (lane < half)
    second = (lane >= half) & (lane < rot_dim)
    idx = np.where(first, lane, np.where(second, lane - half, 0))
    cg, sg = cos[:, idx], sin[:, idx]
    c = jnp.where(first | second, cg, 1.0)
    sa = jnp.where(first, -sg, 0.0)
    sb = jnp.where(second, sg, 0.0)
    return jnp.stack([c, sa, sb])


def _rope_block(x, tab_ref, half):
    return (x * tab_ref[0] + pltpu.roll(x, LANES - half, 1) * tab_ref[1]
            + pltpu.roll(x, half, 1) * tab_ref[2])


def _proj_kernel(x_ref, w_ref, tab_ref, *out_refs, segs, half):
    xb = x_ref[...].astype(BF16)
    for (start, width, kind, scale), o_ref in zip(segs, out_refs):
        acc = _dot(xb, w_ref[:, start:start + width])
        if kind == "rope":
            for c in range(width // LANES):
                y = _rope_block(acc[:, c * LANES:(c + 1) * LANES], tab_ref, half)
                if scale != 1.0:
                    y = y * scale
                o_ref[:, c * LANES:(c + 1) * LANES] = y.astype(o_ref.dtype)
        elif kind == "sigmoid":
            o_ref[...] = (1.0 / (1.0 + jnp.exp(-acc))).astype(o_ref.dtype)
        else:
            o_ref[...] = acc.astype(o_ref.dtype)


def _proj(x, w, tab, segs, dtypes, seq, half):
    t, k = x.shape
    n = w.shape[1]
    tm = PROJ_ROWS
    spb = seq // tm
    out_shape = [jax.ShapeDtypeStruct((t, s[1]), d) for s, d in zip(segs, dtypes)]
    return pl.pallas_call(
        functools.partial(_proj_kernel, segs=tuple(segs), half=half),
        grid=(t // tm,),
        in_specs=[pl.BlockSpec((tm, k), lambda i: (i, 0)),
                  pl.BlockSpec((k, n), lambda i: (0, 0)),
                  pl.BlockSpec((3, tm, LANES), lambda i: (0, i % spb, 0))],
        out_specs=[pl.BlockSpec((tm, s[1]), lambda i: (i, 0)) for s in segs],
        out_shape=out_shape,
        compiler_params=_params("parallel"),
        name="proj",
    )(x, w, tab)


def _layer_norm(z, g, b):
    mu = jnp.mean(z, -1, keepdims=True)
    zc = z - mu
    var = jnp.mean(zc * zc, -1, keepdims=True)
    return zc * lax.rsqrt(var + LN_EPS) * g + b


def _outproj_ln_kernel(o_ref, w_ref, h_ref, g_ref, b_ref, out_ref, *, alpha):
    y = _dot(o_ref[...], w_ref[...])
    out_ref[...] = _layer_norm(alpha * h_ref[...] + y, g_ref[...], b_ref[...])


def _outproj_ln(o, w, h, g, b, alpha):
    t, k = o.shape
    d = w.shape[1]
    tm = LN_ROWS
    return pl.pallas_call(
        functools.partial(_outproj_ln_kernel, alpha=alpha),
        grid=(t // tm,),
        in_specs=[pl.BlockSpec((tm, k), lambda i: (i, 0)),
                  pl.BlockSpec((k, d), lambda i: (0, 0)),
                  pl.BlockSpec((tm, d), lambda i: (i, 0)),
                  pl.BlockSpec((1, d), lambda i: (0, 0)),
                  pl.BlockSpec((1, d), lambda i: (0, 0))],
        out_specs=pl.BlockSpec((tm, d), lambda i: (i, 0)),
        out_shape=jax.ShapeDtypeStruct((t, d), F32),
        compiler_params=_params("parallel"),
        name="outproj_ln",
    )(o, w, h, g.reshape(1, d), b.reshape(1, d))


def _stack_group_queries(q_ref, rows):
    lo = lax.broadcasted_iota(jnp.int32, (rows, LANES), 1) < HEAD_DIM
    qa, qb = q_ref[:, :LANES], q_ref[:, LANES:]
    z = jnp.zeros_like(qa)
    return jnp.concatenate([jnp.where(lo, qa, z), jnp.where(lo, z, qa),
                            jnp.where(lo, qb, z), jnp.where(lo, z, qb)], axis=0)


def _pair(lo_val, hi_val):
    lo = lax.broadcasted_iota(jnp.int32, lo_val.shape, 1) < HEAD_DIM
    return jnp.where(lo, lo_val, hi_val)


def _transpose_chunks(src_ref, dst_ref, chunk):
    for c in range(src_ref.shape[0] // chunk):
        dst_ref[c] = src_ref[c * chunk:(c + 1) * chunk, :].astype(F32).T.astype(dst_ref.dtype)


def _softmax_update_t(s, valid, m, l, scale=None):
    if valid is not None:
        s = jnp.where(valid, s, NEG)
    m_new = jnp.maximum(m, jnp.max(s, 0, keepdims=True))
    if scale is None:
        a = jnp.exp(m - m_new)
        p = jnp.exp(s - m_new)
    else:
        a = jnp.exp((m - m_new) * scale)
        p = jnp.exp((s - m_new) * scale)
    return m_new, a * l + jnp.sum(p, 0, keepdims=True), a, p.astype(BF16)


def _flash_scratch(n_chains, dv, m_cols, tk):
    return [pltpu.VMEM((n_chains, 3, 1, m_cols), F32), pltpu.VMEM((n_chains, dv, m_cols), F32),
            pltpu.VMEM((n_chains, 2, tk, m_cols), F32), pltpu.VMEM((n_chains, 2, tk, m_cols), BF16)]


def _flash_causal_t(score_fn, value_fn, n_chains, qi, causal, state, scale=None):
    st_ref, acc_ref, s_ref, p_ref = state
    for ch in range(n_chains):
        st_ref[ch, 0] = jnp.full(st_ref.shape[2:], MAX_FLOOR, F32)
        st_ref[ch, 1] = jnp.zeros(st_ref.shape[2:], F32)
        st_ref[ch, 2] = jnp.ones(st_ref.shape[2:], F32)
        acc_ref[ch] = jnp.zeros(acc_ref.shape[1:], F32)
        p_ref[ch, 1] = jnp.zeros(p_ref.shape[2:], BF16)
        s_ref[ch, 0] = score_fn(ch, 0)

    def half(c, cur, valid, last):
        nxt = 1 - cur
        for ch in range(n_chains):
            m, l, a, p = _softmax_update_t(s_ref[ch, cur], valid, st_ref[ch, 0], st_ref[ch, 1], scale)
            if not last:
                s_ref[ch, nxt] = score_fn(ch, c + 1)
            acc = st_ref[ch, 2] * acc_ref[ch] + _dot(value_fn(ch, jnp.maximum(c - 1, 0)), p_ref[ch, nxt])
            if last:
                acc = a * acc + _dot(value_fn(ch, c), p)
            else:
                p_ref[ch, cur] = p
                st_ref[ch, 2] = a
            acc_ref[ch] = acc
            st_ref[ch, 0] = m
            st_ref[ch, 1] = l

    def pair(j, carry):
        half(2 * j, 0, None, False)
        half(2 * j + 1, 1, None, False)
        return carry

    lax.fori_loop(0, qi // 2, pair, 0)

    @pl.when(qi % 2 == 1)
    def _():
        half(qi - 1, 0, None, False)
        half(qi, 1, causal, True)

    @pl.when(qi % 2 == 0)
    def _():
        half(qi, 0, causal, True)

    return [(st_ref[ch, 1], acc_ref[ch]) for ch in range(n_chains)]


def _swa_kernel(sink_ref, q_ref, kp_ref, kc_ref, vp_ref, vc_ref, o_ref):
    n = pl.program_id(1)
    blk = A_WINDOW
    i = lax.broadcasted_iota(jnp.int32, (blk, 2 * blk), 0)
    j = lax.broadcasted_iota(jnp.int32, (blk, 2 * blk), 1)
    dist = blk + i - j
    valid = (dist >= 0) & (dist < A_WINDOW) & ((n - 1) * blk + j >= 0)
    group = A_HEADS // A_KV_HEADS
    for g in range(A_KV_HEADS):
        qs = _stack_group_queries(q_ref.at[:, g * 2 * LANES:(g + 1) * 2 * LANES], blk)
        k = jnp.concatenate([kp_ref[:, g * LANES:(g + 1) * LANES],
                             kc_ref[:, g * LANES:(g + 1) * LANES]], axis=0)
        v = jnp.concatenate([vp_ref[:, g * LANES:(g + 1) * LANES],
                             vc_ref[:, g * LANES:(g + 1) * LANES]], axis=0)
        s = _dot_t(qs, k)
        ps = []
        for r in range(group):
            sink = sink_ref[g * group + r]
            sr = jnp.where(valid, s[r * blk:(r + 1) * blk], NEG)
            m = jnp.maximum(jnp.max(sr, -1, keepdims=True), sink)
            e = jnp.where(valid, jnp.exp(sr - m), 0.0)
            p = e / (jnp.sum(e, -1, keepdims=True) + jnp.exp(sink - m))
            ps.append(p.astype(BF16))
        o = _dot(jnp.concatenate(ps, axis=0), v)
        o_ref[:, g * 2 * LANES:g * 2 * LANES + LANES] = _pair(o[:blk], o[blk:2 * blk]).astype(o_ref.dtype)
        o_ref[:, g * 2 * LANES + LANES:(g + 1) * 2 * LANES] = _pair(
            o[2 * blk:3 * blk], o[3 * blk:]).astype(o_ref.dtype)


def _swa_attention(q, kd, vd, sinks, batch, seq):
    t = q.shape[0]
    blk = A_WINDOW
    nb = seq // blk
    qd = A_HEADS * HEAD_DIM
    kw = A_KV_HEADS * LANES
    cur = lambda b, n, s: (b * nb + n, 0)
    prev = lambda b, n, s: (b * nb + jnp.maximum(n - 1, 0), 0)
    return pl.pallas_call(
        _swa_kernel,
        grid_spec=pltpu.PrefetchScalarGridSpec(
            num_scalar_prefetch=1,
            grid=(batch, nb),
            in_specs=[pl.BlockSpec((blk, qd), cur),
                      pl.BlockSpec((blk, kw), prev), pl.BlockSpec((blk, kw), cur),
                      pl.BlockSpec((blk, kw), prev), pl.BlockSpec((blk, kw), cur)],
            out_specs=pl.BlockSpec((blk, qd), cur)),
        out_shape=jax.ShapeDtypeStruct((t, qd), BF16),
        compiler_params=_params("parallel", "parallel"),
        name="swa_attention",
    )(sinks.astype(F32), q, kd, kd, vd, vd)


def _dup_cols(w, heads):
    k = w.shape[0]
    w4 = w.reshape(k, heads, 1, HEAD_DIM)
    return jnp.broadcast_to(w4, (k, heads, 2, HEAD_DIM)).reshape(k, heads * 2 * HEAD_DIM)


def _mixer_a(h, w_in, sinks, w_o, ln_g, ln_b, alpha, batch, seq):
    qd, kd = A_HEADS * HEAD_DIM, A_KV_HEADS * HEAD_DIM
    w = jnp.concatenate([w_in[:, :qd], _dup_cols(w_in[:, qd:qd + kd], A_KV_HEADS),
                         _dup_cols(w_in[:, qd + kd:], A_KV_HEADS)], axis=1).astype(BF16)
    tab = _rope_tables(seq, ROT_DIM, HEAD_DIM, 0)
    segs = [(0, qd, "rope", HEAD_DIM ** -0.5), (qd, 2 * kd, "rope", 1.0), (qd + 2 * kd, 2 * kd, "plain", 1.0)]
    q, k2, v2 = _proj(h, w, tab, segs, [BF16, BF16, BF16], seq, ROT_DIM // 2)
    o = _swa_attention(q, k2, v2, sinks, batch, seq)
    return _outproj_ln(o, w_o.astype(BF16), h, ln_g, ln_b, alpha)


def _mla_proj_kernel(x_ref, wd_ref, qn_ref, kvn_ref, wq_ref, wk_ref, wv_ref, tab_ref,
                     q_ref, k_ref, v_ref):
    half = B_ROPE // 2
    c = _dot(x_ref[...].astype(BF16), wd_ref[...])
    cq, ckv = c[:, :B_Q_RANK], c[:, B_Q_RANK:B_Q_RANK + B_KV_RANK]
    kr = _rope_block(c[:, B_Q_RANK + B_KV_RANK:], tab_ref, half)
    cqn = (cq * lax.rsqrt(jnp.mean(cq * cq, -1, keepdims=True) + RMS_EPS) * qn_ref[...]).astype(BF16)
    ckvn = (ckv * lax.rsqrt(jnp.mean(ckv * ckv, -1, keepdims=True) + RMS_EPS) * kvn_ref[...]).astype(BF16)
    q = _dot(cqn, wq_ref[...])
    kk = _dot(ckvn, wk_ref[...])
    for hd in range(B_HEADS):
        sl = slice(hd * LANES, (hd + 1) * LANES)
        q_ref[:, sl] = _rope_block(q[:, sl], tab_ref, half).astype(q_ref.dtype)
        k_ref[:, sl] = (kk[:, sl] + kr).astype(k_ref.dtype)
    v_ref[...] = _dot(ckvn, wv_ref[...]).astype(v_ref.dtype)


def _mla_attn_kernel(q_ref, k_ref, v_ref, o_ref, vt_ref, *state, tq, scale):
    qi = pl.program_id(2)

    @pl.when(qi == 0)
    def _():
        _transpose_chunks(v_ref, vt_ref, tq)

    q_t = [q_ref[:, hh * LANES:(hh + 1) * LANES].astype(F32).T.astype(BF16) for hh in range(2)]
    key = lax.broadcasted_iota(jnp.int32, (tq, tq), 0)
    qry = lax.broadcasted_iota(jnp.int32, (tq, tq), 1)
    causal = key <= qry

    def scores(hh, c):
        return _dot(k_ref[pl.ds(pl.multiple_of(c * tq, tq), tq), hh * LANES:(hh + 1) * LANES], q_t[hh])

    def values(hh, c):
        return vt_ref[c, hh * B_V:(hh + 1) * B_V, :]

    res = _flash_causal_t(scores, values, 2, qi, causal, state, scale)
    o_t = jnp.concatenate([acc / l for l, acc in res], axis=0)
    o_ref[...] = o_t.T.astype(o_ref.dtype)


def _mixer_b(h, w_down, q_norm, kv_norm, w_uq, w_ukv, w_o, ln_g, ln_b, alpha, batch, seq):
    t, d = h.shape
    dq = B_NOPE + B_ROPE
    pad = LANES - dq
    wd = jnp.concatenate([w_down[:, :B_Q_RANK + B_KV_RANK], jnp.zeros((d, B_NOPE), F32),
                          w_down[:, B_Q_RANK + B_KV_RANK:], jnp.zeros((d, pad), F32)], axis=1).astype(BF16)
    wq = jnp.pad(w_uq.reshape(B_Q_RANK, B_HEADS, dq), ((0, 0), (0, 0), (0, pad))
                 ).reshape(B_Q_RANK, B_HEADS * LANES).astype(BF16)
    wkv = w_ukv.reshape(B_KV_RANK, B_HEADS, B_NOPE + B_V)
    wk = jnp.pad(wkv[:, :, :B_NOPE], ((0, 0), (0, 0), (0, LANES - B_NOPE))
                 ).reshape(B_KV_RANK, B_HEADS * LANES).astype(BF16)
    wv = wkv[:, :, B_NOPE:].reshape(B_KV_RANK, B_HEADS * B_V).astype(BF16)
    tab = _rope_tables(seq, B_ROPE, LANES, B_NOPE)
    tm = PROJ_ROWS
    spb = seq // tm
    full = lambda a: pl.BlockSpec(a.shape, lambda i: (0,) * a.ndim)
    qn, kvn = q_norm.reshape(1, -1), kv_norm.reshape(1, -1)
    q, k, v = pl.pallas_call(
        _mla_proj_kernel,
        grid=(t // tm,),
        in_specs=[pl.BlockSpec((tm, d), lambda i: (i, 0)), full(wd), full(qn), full(kvn),
                  full(wq), full(wk), full(wv),
                  pl.BlockSpec((3, tm, LANES), lambda i: (0, i % spb, 0))],
        out_specs=[pl.BlockSpec((tm, B_HEADS * LANES), lambda i: (i, 0)),
                   pl.BlockSpec((tm, B_HEADS * LANES), lambda i: (i, 0)),
                   pl.BlockSpec((tm, B_HEADS * B_V), lambda i: (i, 0))],
        out_shape=[jax.ShapeDtypeStruct((t, B_HEADS * LANES), BF16),
                   jax.ShapeDtypeStruct((t, B_HEADS * LANES), BF16),
                   jax.ShapeDtypeStruct((t, B_HEADS * B_V), BF16)],
        compiler_params=_params("parallel"),
        name="mla_proj",
    )(h, wd, qn, kvn, wq, wk, wv, tab)

    tq = min(FLASH_ROWS, seq)
    nq = seq // tq
    o = pl.pallas_call(
        functools.partial(_mla_attn_kernel, tq=tq, scale=dq ** -0.5),
        grid=(batch, B_HEADS // 2, nq),
        in_specs=[pl.BlockSpec((tq, 2 * LANES), lambda b, p, i: (b * nq + i, p)),
                  pl.BlockSpec((seq, 2 * LANES), lambda b, p, i: (b, p)),
                  pl.BlockSpec((seq, LANES), lambda b, p, i: (b, p))],
        out_specs=pl.BlockSpec((tq, LANES), lambda b, p, i: (b * nq + i, p)),
        out_shape=jax.ShapeDtypeStruct((t, B_HEADS * B_V), BF16),
        scratch_shapes=[pltpu.VMEM((nq, LANES, tq), BF16)] + _flash_scratch(2, B_V, tq, tq),
        compiler_params=_params("parallel", "parallel", "arbitrary"),
        name="mla_attention",
    )(q, k, v)
    return _outproj_ln(o, w_o.astype(BF16), h, ln_g, ln_b, alpha)


def _gelu_tanh(x):
    return x * (0.5 * (1.0 + jnp.tanh(math.sqrt(2.0 / math.pi) * (x + 0.044715 * (x * x * x)))))


def _compress_kernel(x_ref, pe_ref, w1_ref, w2_ref, o_ref):
    n = x_ref.shape[0] // C_CMP_STRIDE
    a = b = None
    for l in range(C_CMP_STRIDE):
        y = x_ref[pl.ds(l, n, stride=C_CMP_STRIDE), :]
        ta = _dot((y + pe_ref[l:l + 1, :]).astype(BF16), w1_ref[l])
        tb = _dot((y + pe_ref[C_CMP_STRIDE + l:C_CMP_STRIDE + l + 1, :]).astype(BF16), w1_ref[C_CMP_STRIDE + l])
        a = ta if a is None else a + ta
        b = tb if b is None else b + tb
    hid = a + pltpu.roll(b, n - 1, 0)
    o_ref[0] = _dot(_gelu_tanh(hid).astype(BF16), w2_ref[...]).astype(o_ref.dtype)


def _compress(kvc, which, pe, w1, w2, batch, seq):
    n = seq // C_CMP_STRIDE
    hp = LANES // HEAD_DIM
    pairs = C_KV_HEADS // hp
    eye = jnp.eye(hp, dtype=F32)
    w1bd = jnp.einsum("lij,gh->lgihj", w1.reshape(C_CMP_LEN, HEAD_DIM, C_CMP_HIDDEN), eye)
    w1bd = w1bd.reshape(C_CMP_LEN, LANES, hp * C_CMP_HIDDEN).astype(BF16)
    w2bd = jnp.einsum("ij,gh->gihj", jnp.concatenate([w2, w2], axis=1), eye)
    w2bd = w2bd.reshape(hp * C_CMP_HIDDEN, hp * LANES).astype(BF16)
    return pl.pallas_call(
        _compress_kernel,
        grid=(batch, pairs),
        in_specs=[pl.BlockSpec((seq, LANES), lambda i, j: (i, which * pairs + j)),
                  pl.BlockSpec((C_CMP_LEN, LANES), lambda i, j: (0, 0)),
                  pl.BlockSpec(w1bd.shape, lambda i, j: (0, 0, 0)),
                  pl.BlockSpec(w2bd.shape, lambda i, j: (0, 0))],
        out_specs=pl.BlockSpec((1, n, hp * LANES), lambda i, j: (i, 0, j)),
        out_shape=jax.ShapeDtypeStruct((batch, n, C_KV_HEADS * LANES), BF16),
        compiler_params=_params("parallel", "parallel"),
        name="nsa_compress",
    )(kvc, jnp.tile(pe, (1, hp)), w1bd, w2bd)


def _nsa_attn_kernel(q_ref, ks_ref, vs_ref, kw_ref, vw_ref, kc_ref, vc_ref, ov_ref, ex_ref, gt_ref,
                     o_ref, vst_ref, vwt_ref, vct_ref, *state, tq, tk, seq):
    qi = pl.program_id(2)
    q0 = qi * tq
    group = C_HEADS // C_KV_HEADS
    n_sel_blocks = seq // C_SEL_LEN
    nc = kc_ref.shape[1]

    @pl.when(qi == 0)
    def _():
        _transpose_chunks(vs_ref, vst_ref, tk)
        _transpose_chunks(vw_ref, vwt_ref, tq)
        vct_ref[...] = vc_ref[0].astype(F32).T.astype(vct_ref.dtype)

    top = lax.broadcasted_iota(jnp.int32, (LANES, tq), 0) < HEAD_DIM
    qa_t = q_ref[:, :LANES].astype(F32).T
    qb_t = q_ref[:, LANES:].astype(F32).T
    qs_t = jnp.concatenate([jnp.where(top, qa_t, 0.0), jnp.where(top, 0.0, qa_t),
                            jnp.where(top, qb_t, 0.0), jnp.where(top, 0.0, qb_t)], axis=1).astype(BF16)
    rep = lambda a: jnp.concatenate([a] * group, axis=1)

    def masked_softmax(s, valid):
        s = jnp.where(valid, s, NEG)
        e = jnp.exp(s - jnp.maximum(jnp.max(s, 0, keepdims=True), MAX_FLOOR))
        return e / jnp.maximum(jnp.sum(e, 0, keepdims=True), 1e-30)

    t_c = q0 + lax.broadcasted_iota(jnp.int32, (nc, tq), 1)
    n_c = lax.broadcasted_iota(jnp.int32, (nc, tq), 0)
    c_valid = rep(n_c * C_CMP_STRIDE + (C_CMP_LEN - 1) <= t_c)
    p_cmp = masked_softmax(_dot(kc_ref[0], qs_t), c_valid).astype(BF16)
    o_cmp = _dot(vct_ref[...], p_cmp)
    imp4 = _dot(ov_ref[...], p_cmp)
    imp = imp4[:n_sel_blocks, :tq]
    for r in range(1, group):
        imp = imp + imp4[:n_sel_blocks, r * tq:(r + 1) * tq]

    blk = lax.broadcasted_iota(jnp.int32, (n_sel_blocks, tq), 0)
    cur = (q0 + lax.broadcasted_iota(jnp.int32, (n_sel_blocks, tq), 1)) // C_SEL_LEN
    imp = jnp.where((blk == 0) | (blk == cur) | (blk == cur - 1), jnp.inf, imp)
    imp = jnp.where(blk <= cur, imp, -jnp.inf)
    rank = jnp.zeros((n_sel_blocks, tq), F32)
    for i in range(n_sel_blocks):
        r_i = imp[i:i + 1, :]
        rank = rank + jnp.where(blk > i, jnp.where(r_i >= imp, 1.0, 0.0), jnp.where(r_i > imp, 1.0, 0.0))
    sel = jnp.where(rank < min(C_N_SEL, n_sel_blocks), 0.0, NEG)
    sel = jnp.concatenate([sel, jnp.zeros((LANES - n_sel_blocks, tq), F32)], axis=0)
    qs_aug = jnp.concatenate([qs_t, rep(sel).astype(BF16)], axis=0)

    span = C_WINDOW + tq
    start = pl.multiple_of(jnp.maximum(q0 - C_WINDOW, 0), tq)
    t_w = q0 + lax.broadcasted_iota(jnp.int32, (span, tq), 1)
    dist = t_w - (start + lax.broadcasted_iota(jnp.int32, (span, tq), 0))
    w_valid = rep((dist >= 0) & (dist < C_WINDOW))
    p_win = masked_softmax(_dot(kw_ref[pl.ds(start, span), :], qs_t), w_valid).astype(BF16)
    o_win = _dot(vwt_ref[start // tq], p_win[:tq])
    for cb in range(1, span // tq):
        o_win = o_win + _dot(vwt_ref[start // tq + cb], p_win[cb * tq:(cb + 1) * tq])

    t_k = q0 + lax.broadcasted_iota(jnp.int32, (tk, tq), 1)
    k_k = lax.broadcasted_iota(jnp.int32, (tk, tq), 0)

    def sel_scores(ch, c):
        base = pl.multiple_of(c * tk, tk)
        k_aug = jnp.concatenate([ks_ref[pl.ds(base, tk), :], ex_ref[c]], axis=1)
        return jnp.where(rep(base + k_k <= t_k), _dot(k_aug, qs_aug), NEG)

    (l, acc), = _flash_causal_t(sel_scores, lambda ch, c: vst_ref[c], 1, (q0 + tq - 1) // tk, None, state)
    o_slc = acc / jnp.maximum(l, 1e-30)

    g_t = gt_ref[...].T
    outs = []
    for r in range(group):
        cols = slice(r * tq, (r + 1) * tq)
        outs.append(o_cmp[:, cols] * g_t[3 * r:3 * r + 1] + o_slc[:, cols] * g_t[3 * r + 1:3 * r + 2]
                    + o_win[:, cols] * g_t[3 * r + 2:3 * r + 3])
    o_ref[:, :LANES] = jnp.where(top, outs[0], outs[1]).T.astype(o_ref.dtype)
    o_ref[:, LANES:] = jnp.where(top, outs[2], outs[3]).T.astype(o_ref.dtype)


def _mixer_c(h, w_in, pos_k, pos_v, wk1, wk2, wv1, wv2, w_o, ln_g, ln_b, alpha, batch, seq):
    t, d = h.shape
    qd, kd = C_HEADS * HEAD_DIM, C_KV_HEADS * HEAD_DIM
    group = C_HEADS // C_KV_HEADS
    kv = lambda i: w_in[:, qd + i * kd:qd + (i + 1) * kd]
    wg = jnp.pad(w_in[:, qd + 6 * kd:].reshape(d, C_KV_HEADS, group * 3),
                 ((0, 0), (0, 0), (0, LANES - group * 3))).reshape(d, C_KV_HEADS * LANES)
    w = jnp.concatenate([w_in[:, :qd], _dup_cols(kv(2), C_KV_HEADS), _dup_cols(kv(4), C_KV_HEADS),
                         _dup_cols(kv(3), C_KV_HEADS), _dup_cols(kv(5), C_KV_HEADS),
                         kv(0), kv(1), wg], axis=1).astype(BF16)
    tab = _rope_tables(seq, ROT_DIM, HEAD_DIM, 0)
    kw2 = 2 * kd
    segs = [(0, qd, "rope", HEAD_DIM ** -0.5), (qd, 2 * kw2, "rope", 1.0), (qd + 2 * kw2, 2 * kw2, "plain", 1.0),
            (qd + 4 * kw2, 2 * kd, "plain", 1.0), (qd + 4 * kw2 + 2 * kd, C_KV_HEADS * LANES, "sigmoid", 1.0)]
    q, ksw, vsw, kvc, gates = _proj(h, w, tab, segs, [BF16, BF16, BF16, F32, F32], seq, ROT_DIM // 2)
    k_cmp = _compress(kvc, 0, pos_k, wk1, wk2, batch, seq)
    v_cmp = _compress(kvc, 1, pos_v, wv1, wv2, batch, seq)

    nc = seq // C_CMP_STRIDE
    nsb = seq // C_SEL_LEN
    tq = 128
    tk = min(512, seq)
    nq = seq // tq
    cs = np.arange(nc)[None, :] * C_CMP_STRIDE
    ss = np.arange(LANES)[:, None] * C_SEL_LEN
    overlap = ((cs <= ss + C_SEL_LEN - 1) & (ss <= cs + C_CMP_LEN - 1) & (np.arange(LANES)[:, None] < nsb))
    overlap = jnp.asarray(overlap, BF16)
    key_blk = (np.arange(seq) // C_SEL_LEN).reshape(seq // tk, tk, 1)
    expand = jnp.asarray(key_blk == np.arange(LANES)[None, None, :], BF16)

    per_bg = lambda width: pl.BlockSpec((seq, LANES), lambda b, g, i, width=width: (b, width + g))
    o = pl.pallas_call(
        functools.partial(_nsa_attn_kernel, tq=tq, tk=tk, seq=seq),
        grid=(batch, C_KV_HEADS, nq),
        in_specs=[pl.BlockSpec((tq, 2 * LANES), lambda b, g, i: (b * nq + i, g)),
                  per_bg(0), per_bg(0), per_bg(C_KV_HEADS), per_bg(C_KV_HEADS),
                  pl.BlockSpec((1, nc, LANES), lambda b, g, i: (b, 0, g)),
                  pl.BlockSpec((1, nc, LANES), lambda b, g, i: (b, 0, g)),
                  pl.BlockSpec((LANES, nc), lambda b, g, i: (0, 0)),
                  pl.BlockSpec((seq // tk, tk, LANES), lambda b, g, i: (0, 0, 0)),
                  pl.BlockSpec((tq, LANES), lambda b, g, i: (b * nq + i, g))],
        out_specs=pl.BlockSpec((tq, 2 * LANES), lambda b, g, i: (b * nq + i, g)),
        out_shape=jax.ShapeDtypeStruct((t, qd), BF16),
        scratch_shapes=[pltpu.VMEM((seq // tk, LANES, tk), BF16), pltpu.VMEM((seq // tq, LANES, tq), BF16),
                        pltpu.VMEM((LANES, nc), BF16)] + _flash_scratch(1, LANES, group * tq, tk),
        compiler_params=_params("parallel", "parallel", "arbitrary"),
        name="nsa_attention",
    )(q, ksw, vsw, ksw, vsw, k_cmp, v_cmp, overlap, expand, gates)
    return _outproj_ln(o, w_o.astype(BF16), h, ln_g, ln_b, alpha)


def _diff_attn_kernel(q_ref, k_ref, v_ref, lam_ref, sub_ref, o_ref, vt_ref, *state, tq, lam_init):
    qi = pl.program_id(2)

    n_heads = q_ref.shape[1] // LANES

    @pl.when(qi == 0)
    def _():
        for hd in range(n_heads):
            _transpose_chunks(v_ref.at[:, hd * LANES:(hd + 1) * LANES], vt_ref.at[hd], tq)

    lam = (jnp.exp(jnp.sum(lam_ref[0:1, :] * lam_ref[1:2, :], -1, keepdims=True))
           - jnp.exp(jnp.sum(lam_ref[2:3, :] * lam_ref[3:4, :], -1, keepdims=True)) + lam_init)
    top = lax.broadcasted_iota(jnp.int32, (LANES, tq), 0) < D_SUB
    qs_t = []
    for hd in range(n_heads):
        q_t = q_ref[:, hd * LANES:(hd + 1) * LANES].astype(F32).T
        qs_t.append(jnp.concatenate([jnp.where(top, q_t, 0.0), jnp.where(top, 0.0, q_t)], axis=1).astype(BF16))
    key = lax.broadcasted_iota(jnp.int32, (tq, tq), 0)
    qry = lax.broadcasted_iota(jnp.int32, (tq, tq), 1)
    causal = jnp.concatenate([key <= qry] * 2, axis=1)

    def scores(hd, c):
        return _dot(k_ref[pl.ds(pl.multiple_of(c * tq, tq), tq), hd * LANES:(hd + 1) * LANES], qs_t[hd])

    def values(hd, c):
        return vt_ref[hd, c]

    res = _flash_causal_t(scores, values, n_heads, qi, causal, state)
    for hd in range(n_heads):
        l, acc = res[hd]
        o = (acc / l).T
        o = o[:tq] - lam * o[tq:]
        o = o * lax.rsqrt(jnp.mean(o * o, -1, keepdims=True) + RMS_EPS) * sub_ref[...]
        o_ref[:, hd * LANES:(hd + 1) * LANES] = (o * (1.0 - lam_init)).astype(o_ref.dtype)


def _mixer_d(h, w_in, lq1, lk1, lq2, lk2, subln, w_o, ln_g, ln_b, alpha, layer_idx, batch, seq):
    t, d = h.shape
    qd = D_HEADS * 2 * D_SUB
    tab = _rope_tables(seq, ROT_DIM, D_SUB, 0)
    segs = [(0, qd, "rope", D_SUB ** -0.5), (qd, qd, "rope", 1.0), (2 * qd, qd, "plain", 1.0)]
    q, k, v = _proj(h, w_in.astype(BF16), tab, segs, [BF16, BF16, BF16], seq, ROT_DIM // 2)
    lam_init = 0.8 - 0.6 * math.exp(-0.3 * layer_idx)
    lam_in = jnp.stack([lq1, lk1, lq2, lk2]).astype(F32)
    tq = min(FLASH_ROWS, seq)
    nq = seq // tq
    hps = 2
    o = pl.pallas_call(
        functools.partial(_diff_attn_kernel, tq=tq, lam_init=lam_init),
        grid=(batch, D_HEADS // hps, nq),
        in_specs=[pl.BlockSpec((tq, hps * LANES), lambda b, hd, i: (b * nq + i, hd)),
                  pl.BlockSpec((seq, hps * LANES), lambda b, hd, i: (b, hd)),
                  pl.BlockSpec((seq, hps * LANES), lambda b, hd, i: (b, hd)),
                  pl.BlockSpec((4, D_SUB), lambda b, hd, i: (0, 0)),
                  pl.BlockSpec((1, 2 * D_SUB), lambda b, hd, i: (0, 0))],
        out_specs=pl.BlockSpec((tq, hps * LANES), lambda b, hd, i: (b * nq + i, hd)),
        out_shape=jax.ShapeDtypeStruct((t, qd), BF16),
        scratch_shapes=[pltpu.VMEM((hps, nq, LANES, tq), BF16)] + _flash_scratch(hps, LANES, 2 * tq, tq),
        compiler_params=_params("parallel", "parallel", "arbitrary"),
        name="diff_attention",
    )(q, k, v, lam_in, subln.reshape(1, -1))
    return _outproj_ln(o, w_o.astype(BF16), h, ln_g, ln_b, alpha)


def _router_kernel(x_ref, w_ref, o_ref):
    xb = x_ref[...].astype(BF16)
    logits = _dot(xb, w_ref[...])
    lg = logits[:, :LANES].T[:8]
    le = logits[:, LANES:].T[:M_EXPERTS]
    far = 4 * LANES

    def softmax(x, valid):
        m = jnp.max(jnp.where(valid, x, NEG), 0, keepdims=True)
        e = jnp.where(valid, jnp.exp(x - m), 0.0)
        return e / jnp.sum(e, 0, keepdims=True)

    def first_max(p, valid, row):
        top = jnp.max(jnp.where(valid, p, -1.0), 0, keepdims=True)
        idx = jnp.min(jnp.where(valid & (p == top), row, far), 0, keepdims=True)
        return top, idx

    g_row = lax.broadcasted_iota(jnp.int32, lg.shape, 0)
    e_row = lax.broadcasted_iota(jnp.int32, le.shape, 0)
    g_valid = g_row < M_GROUPS
    g_w, g_idx = first_max(softmax(lg, g_valid), g_valid, g_row)
    e_valid = (e_row >= g_idx * M_PER_GROUP) & (e_row < (g_idx + 1) * M_PER_GROUP)
    pe = softmax(le, e_valid)
    w0, i0 = first_max(pe, e_valid, e_row)
    w1, i1 = first_max(pe, e_valid & (e_row != i0), e_row)
    tot = w0 + w1
    out = jnp.where(g_row == 0, i0.astype(F32), 0.0)
    out = jnp.where(g_row == 1, i1.astype(F32), out)
    out = jnp.where(g_row == 2, g_w * w0 / tot, out)
    out = jnp.where(g_row == 3, g_w * w1 / tot, out)
    o_ref[...] = out


def _row_gather(src_hbm, idx_ref, buf, sem):
    for r in range(buf.shape[0]):
        pltpu.make_async_copy(src_hbm.at[pl.ds(idx_ref[0, 0, r], 1), :], buf.at[pl.ds(r, 1), :], sem).start()


def _row_gather_wait(src_hbm, buf, sem):
    pltpu.make_async_copy(src_hbm.at[pl.ds(0, buf.shape[0]), :], buf, sem).wait()


def _moe_kernel(eid_ref, used_ref, src0_ref, src_ref, h_hbm, wg_ref, wu_ref, wd_ref, o_ref,
                xbuf0, xbuf1, wgb, wub, wdb, gsem, *, tm):
    t = pl.program_id(0)
    used = used_ref[0]
    xbuf = (xbuf0, xbuf1)

    @pl.when(t == 0)
    def _():
        _row_gather(h_hbm, src0_ref, xbuf0, gsem.at[0])

    @pl.when((t < used) & ((t == 0) | (eid_ref[t] != eid_ref[jnp.maximum(t - 1, 0)])))
    def _():
        wgb[...] = wg_ref[0, 0].astype(BF16)
        wub[...] = wu_ref[0, 0].astype(BF16)
        wdb[...] = wd_ref[0, 0].astype(BF16)

    for s in range(2):
        @pl.when((t < used) & (t % 2 == s))
        def _(s=s):
            _row_gather_wait(h_hbm, xbuf[s], gsem.at[s])
            _row_gather(h_hbm, src_ref, xbuf[1 - s], gsem.at[1 - s])
            xb = xbuf[s][...].astype(BF16)
            gate = _dot(xb, wgb[...])
            up = _dot(xb, wub[...])
            hid = (gate * (1.0 / (1.0 + jnp.exp(-gate))) * up).astype(BF16)
            o_ref[...] = _dot(hid, wdb[...])

        @pl.when((t == used) & (t % 2 == s))
        def _(s=s):
            _row_gather_wait(h_hbm, xbuf[s], gsem.at[s])

    @pl.when(t >= used)
    def _():
        o_ref[...] = jnp.zeros(o_ref.shape, F32)


def _combine_ln_kernel(pos0_ref, pos_ref, h_ref, w_ref, g_ref, b_ref, ys_hbm, o_ref, ybuf0, ybuf1, sem, *,
                       tm, alpha):
    t = pl.program_id(0)
    ybuf = (ybuf0, ybuf1)

    @pl.when(t == 0)
    def _():
        _row_gather(ys_hbm, pos0_ref, ybuf0, sem.at[0])

    for s in range(2):
        @pl.when(t % 2 == s)
        def _(s=s):
            _row_gather_wait(ys_hbm, ybuf[s], sem.at[s])
            _row_gather(ys_hbm, pos_ref, ybuf[1 - s], sem.at[1 - s])
            w = w_ref[...]
            y = w[:, 0:1] * ybuf[s][:tm] + w[:, 1:2] * ybuf[s][tm:]
            o_ref[...] = _layer_norm(alpha * h_ref[...] + y, g_ref[...], b_ref[...])

            @pl.when(t == pl.num_programs(0) - 1)
            def _():
                _row_gather_wait(ys_hbm, ybuf[1 - s], sem.at[1 - s])


def _hier_moe_ln(h, w_group, w_expert, w_gate, w_up, w_down, layer, ln_g, ln_b, alpha):
    t, d = h.shape
    tm = MOE_ROWS
    wr = jnp.concatenate([jnp.pad(w_group, ((0, 0), (0, LANES - M_GROUPS))),
                          jnp.pad(w_expert, ((0, 0), (0, LANES - M_EXPERTS)))], axis=1).astype(BF16)
    routed = pl.pallas_call(
        _router_kernel,
        grid=(t // LN_ROWS,),
        in_specs=[pl.BlockSpec((LN_ROWS, d), lambda i: (i, 0)),
                  pl.BlockSpec((d, 2 * LANES), lambda i: (0, 0))],
        out_specs=pl.BlockSpec((8, LN_ROWS), lambda i: (0, i)),
        out_shape=jax.ShapeDtypeStruct((8, t), F32),
        compiler_params=_params("parallel"),
        name="moe_router",
    )(h, wr)

    n_rows = t * M_TOPK
    n_tiles = n_rows // tm + M_EXPERTS
    expert = routed[:M_TOPK].T.astype(jnp.int32).reshape(-1)
    weight = routed[M_TOPK:2 * M_TOPK].T
    row_ids = jnp.arange(n_rows, dtype=jnp.int32)
    e_sorted, order = lax.sort((expert, row_ids), num_keys=1, is_stable=True)
    experts = jnp.arange(M_EXPERTS, dtype=jnp.int32)[None, :]
    sizes = jnp.sum(expert[:, None] == experts, axis=0, dtype=jnp.int32)
    tiles = (sizes + tm - 1) // tm
    tile_end = jnp.cumsum(tiles)
    seg_start = jnp.cumsum(sizes) - sizes
    used = tile_end[-1]
    tile_ids = jnp.arange(n_tiles, dtype=jnp.int32)
    tile_eid = jnp.sum(jnp.minimum(tile_ids, used - 1)[:, None] >= tile_end[None, :], axis=1).astype(jnp.int32)
    tile_first = (tile_end - tiles)[tile_eid]
    lane_r = jnp.arange(tm, dtype=jnp.int32)[None, :]
    offs = (tile_ids - tile_first)[:, None] * tm + lane_r
    valid = (offs < sizes[tile_eid][:, None]) & (tile_ids < used)[:, None]
    row = order[jnp.clip(seg_start[tile_eid][:, None] + offs, 0, n_rows - 1)]
    src = jnp.where(valid, row // M_TOPK, 0).reshape(n_tiles, 1, tm)
    shift = (tile_end - tiles) * tm - seg_start
    p_sorted = row_ids + jnp.sum(jnp.where(e_sorted[:, None] == experts, shift[None, :], 0), axis=1)
    _, pos = lax.sort((order, p_sorted), num_keys=1)

    wspec = lambda shape: pl.BlockSpec((1, 1) + shape, lambda i, eid, used: (layer, eid[i], 0, 0))
    idx_spec = pl.BlockSpec((1, 1, tm), lambda i, eid, used: (i, 0, 0), memory_space=pltpu.SMEM)
    nxt_spec = pl.BlockSpec((1, 1, tm), lambda i, eid, used: (jnp.minimum(i + 1, n_tiles - 1), 0, 0),
                            memory_space=pltpu.SMEM)
    ys = pl.pallas_call(
        functools.partial(_moe_kernel, tm=tm),
        grid_spec=pltpu.PrefetchScalarGridSpec(
            num_scalar_prefetch=2,
            grid=(n_tiles,),
            in_specs=[idx_spec, nxt_spec,
                      pl.BlockSpec(memory_space=pl.ANY),
                      wspec((d, M_HIDDEN)), wspec((d, M_HIDDEN)), wspec((M_HIDDEN, d))],
            out_specs=pl.BlockSpec((tm, d), lambda i, eid, used: (i, 0)),
            scratch_shapes=[pltpu.VMEM((tm, d), F32), pltpu.VMEM((tm, d), F32),
                            pltpu.VMEM((d, M_HIDDEN), BF16), pltpu.VMEM((d, M_HIDDEN), BF16),
                            pltpu.VMEM((M_HIDDEN, d), BF16),
                            pltpu.SemaphoreType.DMA((2,))]),
        out_shape=jax.ShapeDtypeStruct((n_tiles * tm, d), F32),
        compiler_params=_params("arbitrary"),
        name="moe_experts",
    )(tile_eid, used.reshape(1).astype(jnp.int32), src, src, h, w_gate, w_up, w_down)

    tc = COMBINE_ROWS
    nt = t // tc
    pos_t = pos.reshape(nt, tc, M_TOPK).transpose(0, 2, 1).reshape(nt, 1, M_TOPK * tc)
    cur_spec = pl.BlockSpec((1, 1, M_TOPK * tc), lambda i: (i, 0, 0), memory_space=pltpu.SMEM)
    nxt_spec = pl.BlockSpec((1, 1, M_TOPK * tc), lambda i: (jnp.minimum(i + 1, nt - 1), 0, 0),
                            memory_space=pltpu.SMEM)
    return pl.pallas_call(
        functools.partial(_combine_ln_kernel, tm=tc, alpha=alpha),
        grid=(nt,),
        in_specs=[cur_spec, nxt_spec,
                  pl.BlockSpec((tc, d), lambda i: (i, 0)),
                  pl.BlockSpec((tc, M_TOPK), lambda i: (i, 0)),
                  pl.BlockSpec((1, d), lambda i: (0, 0)),
                  pl.BlockSpec((1, d), lambda i: (0, 0)),
                  pl.BlockSpec(memory_space=pl.ANY)],
        out_specs=pl.BlockSpec((tc, d), lambda i: (i, 0)),
        out_shape=jax.ShapeDtypeStruct((t, d), F32),
        scratch_shapes=[pltpu.VMEM((M_TOPK * tc, d), F32), pltpu.VMEM((M_TOPK * tc, d), F32),
                        pltpu.SemaphoreType.DMA((2,))],
        compiler_params=_params("arbitrary"),
        name="moe_combine_ln",
    )(pos_t, pos_t, h, weight, ln_g.reshape(1, d), ln_b.reshape(1, d), ys)


def kernel(x, a_w_in, a_sinks, a_w_o, b_w_down, b_q_norm, b_kv_norm, b_w_uq, b_w_ukv, b_w_o, c_w_in, c_pos_k, c_pos_v, c_wk1, c_wk2, c_wv1, c_wv2, c_w_o, d_w_in, d_lq1, d_lk1, d_lq2, d_lk2, d_subln, d_w_o, moe_w_group, moe_w_expert, moe_w_gate, moe_w_up, moe_w_down, ln_g, ln_b):
    batch, seq, d = x.shape
    depth = ln_g.shape[0]
    alpha = (2 * depth) ** 0.25
    h = x.reshape(batch * seq, d)
    for i in range(depth):
        kind, j = i % N_MIXERS, i // N_MIXERS
        g, b = ln_g[i, 0], ln_b[i, 0]
        if kind == 0:
            h = _mixer_a(h, a_w_in[j], a_sinks[j], a_w_o[j], g, b, alpha, batch, seq)
        elif kind == 1:
            h = _mixer_b(h, b_w_down[j], b_q_norm[j], b_kv_norm[j], b_w_uq[j], b_w_ukv[j], b_w_o[j],
                         g, b, alpha, batch, seq)
        elif kind == 2:
            h = _mixer_c(h, c_w_in[j], c_pos_k[j], c_pos_v[j], c_wk1[j], c_wk2[j], c_wv1[j], c_wv2[j],
                         c_w_o[j], g, b, alpha, batch, seq)
        else:
            h = _mixer_d(h, d_w_in[j], d_lq1[j], d_lk1[j], d_lq2[j], d_lk2[j], d_subln[j], d_w_o[j],
                         g, b, alpha, i, batch, seq)
        h = _hier_moe_ln(h, moe_w_group[i], moe_w_expert[i], moe_w_gate, moe_w_up, moe_w_down, i,
                         ln_g[i, 1], ln_b[i, 1], alpha)
    return h.reshape(batch, seq, d)
```

```python
import functools
import math

import numpy as np
import jax
import jax.numpy as jnp
from jax import lax
from jax.experimental import pallas as pl
from jax.experimental.pallas import tpu as pltpu

F32 = jnp.float32
BF16 = jnp.bfloat16

HEAD_DIM = 64
ROPE_THETA = 500000.0
ROT_DIM = HEAD_DIM // 4
A_HEADS, A_KV_HEADS, A_WINDOW = 16, 4, 128
B_HEADS, B_Q_RANK, B_KV_RANK, B_NOPE, B_ROPE, B_V = 16, 384, 256, 64, 32, 64
C_HEADS, C_KV_HEADS = 16, 4
C_CMP_LEN, C_CMP_STRIDE, C_CMP_HIDDEN = 32, 16, 128
C_SEL_LEN, C_N_SEL, C_WINDOW = 64, 16, 512
D_HEADS, D_SUB = 8, 64
M_GROUPS, M_PER_GROUP, M_TOPK, M_HIDDEN = 4, 8, 2, 512
M_EXPERTS = M_GROUPS * M_PER_GROUP
N_MIXERS = 4
LN_EPS = 1e-5
RMS_EPS = 1e-6

LANES = 128
NEG = -1e30
MAX_FLOOR = -1e20
VMEM_LIMIT = 48 * 1024 * 1024

PROJ_ROWS = 256
LN_ROWS = 512
MOE_ROWS = 256
FLASH_ROWS = 512


def _params(*sem):
    return pltpu.CompilerParams(dimension_semantics=sem, vmem_limit_bytes=VMEM_LIMIT)


def _dot(a, b):
    return jnp.dot(a, b, preferred_element_type=F32)


def _dot_t(a, b):
    return lax.dot_general(a, b, (((1,), (1,)), ((), ())), preferred_element_type=F32)


def _rope_tables(seq, rot_dim, period, off):
    half = rot_dim // 2
    inv_freq = 1.0 / (ROPE_THETA ** (jnp.arange(half, dtype=F32) * (2.0 / rot_dim)))
    ang = jnp.arange(seq, dtype=F32)[:, None] * inv_freq[None, :]
    cos, sin = jnp.cos(ang), jnp.sin(ang)
    lane = np.arange(LANES) % period - off
    first = (lane >= 0) & (lane < half)
    second = (lane >= half) & (lane < rot_dim)
    idx = np.where(first, lane, np.where(second, lane - half, 0))
    cg, sg = cos[:, idx], sin[:, idx]
    c = jnp.where(first | second, cg, 1.0)
    sa = jnp.where(first, -sg, 0.0)
    sb = jnp.where(second, sg, 0.0)
    return jnp.stack([c, sa, sb])


def _rope_block(x, tab_ref, half):
    return (x * tab_ref[0] + pltpu.roll(x, LANES - half, 1) * tab_ref[1]
            + pltpu.roll(x, half, 1) * tab_ref[2])


def _proj_kernel(x_ref, w_ref, tab_ref, *out_refs, segs, half):
    xb = x_ref[...].astype(BF16)
    for (start, width, kind, scale), o_ref in zip(segs, out_refs):
        acc = _dot(xb, w_ref[:, start:start + width])
        if kind == "rope":
            for c in range(width // LANES):
                y = _rope_block(acc[:, c * LANES:(c + 1) * LANES], tab_ref, half)
                if scale != 1.0:
                    y = y * scale
                o_ref[:, c * LANES:(c + 1) * LANES] = y.astype(o_ref.dtype)
        elif kind == "sigmoid":
            o_ref[...] = (1.0 / (1.0 + jnp.exp(-acc))).astype(o_ref.dtype)
        else:
            o_ref[...] = acc.astype(o_ref.dtype)


def _proj(x, w, tab, segs, dtypes, seq, half):
    t, k = x.shape
    n = w.shape[1]
    tm = PROJ_ROWS
    spb = seq // tm
    out_shape = [jax.ShapeDtypeStruct((t, s[1]), d) for s, d in zip(segs, dtypes)]
    return pl.pallas_call(
        functools.partial(_proj_kernel, segs=tuple(segs), half=half),
        grid=(t // tm,),
        in_specs=[pl.BlockSpec((tm, k), lambda i: (i, 0)),
                  pl.BlockSpec((k, n), lambda i: (0, 0)),
                  pl.BlockSpec((3, tm, LANES), lambda i: (0, i % spb, 0))],
        out_specs=[pl.BlockSpec((tm, s[1]), lambda i: (i, 0)) for s in segs],
        out_shape=out_shape,
        compiler_params=_params("parallel"),
        name="proj",
    )(x, w, tab)


def _layer_norm(z, g, b):
    mu = jnp.mean(z, -1, keepdims=True)
    zc = z - mu
    var = jnp.mean(zc * zc, -1, keepdims=True)
    return zc * lax.rsqrt(var + LN_EPS) * g + b


def _outproj_ln_kernel(o_ref, w_ref, h_ref, g_ref, b_ref, out_ref, *, alpha):
    y = _dot(o_ref[...], w_ref[...])
    out_ref[...] = _layer_norm(alpha * h_ref[...] + y, g_ref[...], b_ref[...])


def _outproj_ln(o, w, h, g, b, alpha):
    t, k = o.shape
    d = w.shape[1]
    tm = LN_ROWS
    return pl.pallas_call(
        functools.partial(_outproj_ln_kernel, alpha=alpha),
        grid=(t // tm,),
        in_specs=[pl.BlockSpec((tm, k), lambda i: (i, 0)),
                  pl.BlockSpec((k, d), lambda i: (0, 0)),
                  pl.BlockSpec((tm, d), lambda i: (i, 0)),
                  pl.BlockSpec((1, d), lambda i: (0, 0)),
                  pl.BlockSpec((1, d), lambda i: (0, 0))],
        out_specs=pl.BlockSpec((tm, d), lambda i: (i, 0)),
        out_shape=jax.ShapeDtypeStruct((t, d), F32),
        compiler_params=_params("parallel"),
        name="outproj_ln",
    )(o, w, h, g.reshape(1, d), b.reshape(1, d))


def _stack_group_queries(q_ref, rows):
    lo = lax.broadcasted_iota(jnp.int32, (rows, LANES), 1) < HEAD_DIM
    qa, qb = q_ref[:, :LANES], q_ref[:, LANES:]
    z = jnp.zeros_like(qa)
    return jnp.concatenate([jnp.where(lo, qa, z), jnp.where(lo, z, qa),
                            jnp.where(lo, qb, z), jnp.where(lo, z, qb)], axis=0)


def _pair(lo_val, hi_val):
    lo = lax.broadcasted_iota(jnp.int32, lo_val.shape, 1) < HEAD_DIM
    return jnp.where(lo, lo_val, hi_val)


def _transpose_chunks(src_ref, dst_ref, chunk):
    for c in range(src_ref.shape[0] // chunk):
        dst_ref[c] = src_ref[c * chunk:(c + 1) * chunk, :].astype(F32).T.astype(dst_ref.dtype)


def _softmax_update_t(s, valid, m, l, scale=None):
    if valid is not None:
        s = jnp.where(valid, s, NEG)
    m_new = jnp.maximum(m, jnp.max(s, 0, keepdims=True))
    if scale is None:
        a = jnp.exp(m - m_new)
        p = jnp.exp(s - m_new)
    else:
        a = jnp.exp((m - m_new) * scale)
        p = jnp.exp((s - m_new) * scale)
    return m_new, a * l + jnp.sum(p, 0, keepdims=True), a, p.astype(BF16)


def _flash_scratch(n_chains, dv, m_cols, tk):
    return [pltpu.VMEM((n_chains, 3, 1, m_cols), F32), pltpu.VMEM((n_chains, dv, m_cols), F32),
            pltpu.VMEM((n_chains, 2, tk, m_cols), F32), pltpu.VMEM((n_chains, 2, tk, m_cols), BF16)]


def _flash_causal_t(score_fn, value_fn, n_chains, qi, causal, state, scale=None):
    st_ref, acc_ref, s_ref, p_ref = state
    for ch in range(n_chains):
        st_ref[ch, 0] = jnp.full(st_ref.shape[2:], MAX_FLOOR, F32)
        st_ref[ch, 1] = jnp.zeros(st_ref.shape[2:], F32)
        st_ref[ch, 2] = jnp.ones(st_ref.shape[2:], F32)
        acc_ref[ch] = jnp.zeros(acc_ref.shape[1:], F32)
        p_ref[ch, 1] = jnp.zeros(p_ref.shape[2:], BF16)
        s_ref[ch, 0] = score_fn(ch, 0)

    def half(c, cur, valid, last):
        nxt = 1 - cur
        for ch in range(n_chains):
            m, l, a, p = _softmax_update_t(s_ref[ch, cur], valid, st_ref[ch, 0], st_ref[ch, 1], scale)
            if not last:
                s_ref[ch, nxt] = score_fn(ch, c + 1)
            acc = st_ref[ch, 2] * acc_ref[ch] + _dot(value_fn(ch, jnp.maximum(c - 1, 0)), p_ref[ch, nxt])
            if last:
                acc = a * acc + _dot(value_fn(ch, c), p)
            else:
                p_ref[ch, cur] = p
                st_ref[ch, 2] = a
            acc_ref[ch] = acc
            st_ref[ch, 0] = m
            st_ref[ch, 1] = l

    def pair(j, carry):
        half(2 * j, 0, None, False)
        half(2 * j + 1, 1, None, False)
        return carry

    lax.fori_loop(0, qi // 2, pair, 0)

    @pl.when(qi % 2 == 1)
    def _():
        half(qi - 1, 0, None, False)
        half(qi, 1, causal, True)

    @pl.when(qi % 2 == 0)
    def _():
        half(qi, 0, causal, True)

    return [(st_ref[ch, 1], acc_ref[ch]) for ch in range(n_chains)]


def _swa_kernel(sink_ref, q_ref, kp_ref, kc_ref, vp_ref, vc_ref, o_ref):
    n = pl.program_id(1)
    blk = A_WINDOW
    i = lax.broadcasted_iota(jnp.int32, (blk, 2 * blk), 0)
    j = lax.broadcasted_iota(jnp.int32, (blk, 2 * blk), 1)
    dist = blk + i - j
    valid = (dist >= 0) & (dist < A_WINDOW) & ((n - 1) * blk + j >= 0)
    group = A_HEADS // A_KV_HEADS
    for g in range(A_KV_HEADS):
        qs = _stack_group_queries(q_ref.at[:, g * 2 * LANES:(g + 1) * 2 * LANES], blk)
        k = jnp.concatenate([kp_ref[:, g * LANES:(g + 1) * LANES],
                             kc_ref[:, g * LANES:(g + 1) * LANES]], axis=0)
        v = jnp.concatenate([vp_ref[:, g * LANES:(g + 1) * LANES],
                             vc_ref[:, g * LANES:(g + 1) * LANES]], axis=0)
        s = _dot_t(qs, k)
        ps = []
        for r in range(group):
            sink = sink_ref[g * group + r]
            sr = jnp.where(valid, s[r * blk:(r + 1) * blk], NEG)
            m = jnp.maximum(jnp.max(sr, -1, keepdims=True), sink)
            e = jnp.where(valid, jnp.exp(sr - m), 0.0)
            p = e / (jnp.sum(e, -1, keepdims=True) + jnp.exp(sink - m))
            ps.append(p.astype(BF16))
        o = _dot(jnp.concatenate(ps, axis=0), v)
        o_ref[:, g * 2 * LANES:g * 2 * LANES + LANES] = _pair(o[:blk], o[blk:2 * blk]).astype(o_ref.dtype)
        o_ref[:, g * 2 * LANES + LANES:(g + 1) * 2 * LANES] = _pair(
            o[2 * blk:3 * blk], o[3 * blk:]).astype(o_ref.dtype)


def _swa_attention(q, kd, vd, sinks, batch, seq):
    t = q.shape[0]
    blk = A_WINDOW
    nb = seq // blk
    qd = A_HEADS * HEAD_DIM
    kw = A_KV_HEADS * LANES
    cur = lambda b, n, s: (b * nb + n, 0)
    prev = lambda b, n, s: (b * nb + jnp.maximum(n - 1, 0), 0)
    return pl.pallas_call(
        _swa_kernel,
        grid_spec=pltpu.PrefetchScalarGridSpec(
            num_scalar_prefetch=1,
            grid=(batch, nb),
            in_specs=[pl.BlockSpec((blk, qd), cur),
                      pl.BlockSpec((blk, kw), prev), pl.BlockSpec((blk, kw), cur),
                      pl.BlockSpec((blk, kw), prev), pl.BlockSpec((blk, kw), cur)],
            out_specs=pl.BlockSpec((blk, qd), cur)),
        out_shape=jax.ShapeDtypeStruct((t, qd), BF16),
        compiler_params=_params("parallel", "parallel"),
        name="swa_attention",
    )(sinks.astype(F32), q, kd, kd, vd, vd)


def _dup_cols(w, heads):
    k = w.shape[0]
    w4 = w.reshape(k, heads, 1, HEAD_DIM)
    return jnp.broadcast_to(w4, (k, heads, 2, HEAD_DIM)).reshape(k, heads * 2 * HEAD_DIM)


def _mixer_a(h, w_in, sinks, w_o, ln_g, ln_b, alpha, batch, seq):
    qd, kd = A_HEADS * HEAD_DIM, A_KV_HEADS * HEAD_DIM
    w = jnp.concatenate([w_in[:, :qd], _dup_cols(w_in[:, qd:qd + kd], A_KV_HEADS),
                         _dup_cols(w_in[:, qd + kd:], A_KV_HEADS)], axis=1).astype(BF16)
    tab = _rope_tables(seq, ROT_DIM, HEAD_DIM, 0)
    segs = [(0, qd, "rope", HEAD_DIM ** -0.5), (qd, 2 * kd, "rope", 1.0), (qd + 2 * kd, 2 * kd, "plain", 1.0)]
    q, k2, v2 = _proj(h, w, tab, segs, [BF16, BF16, BF16], seq, ROT_DIM // 2)
    o = _swa_attention(q, k2, v2, sinks, batch, seq)
    return _outproj_ln(o, w_o.astype(BF16), h, ln_g, ln_b, alpha)


def _mla_proj_kernel(x_ref, wd_ref, qn_ref, kvn_ref, wq_ref, wk_ref, wv_ref, tab_ref,
                     q_ref, k_ref, v_ref):
    half = B_ROPE // 2
    c = _dot(x_ref[...].astype(BF16), wd_ref[...])
    cq, ckv = c[:, :B_Q_RANK], c[:, B_Q_RANK:B_Q_RANK + B_KV_RANK]
    kr = _rope_block(c[:, B_Q_RANK + B_KV_RANK:], tab_ref, half)
    cqn = (cq * lax.rsqrt(jnp.mean(cq * cq, -1, keepdims=True) + RMS_EPS) * qn_ref[...]).astype(BF16)
    ckvn = (ckv * lax.rsqrt(jnp.mean(ckv * ckv, -1, keepdims=True) + RMS_EPS) * kvn_ref[...]).astype(BF16)
    q = _dot(cqn, wq_ref[...])
    kk = _dot(ckvn, wk_ref[...])
    for hd in range(B_HEADS):
        sl = slice(hd * LANES, (hd + 1) * LANES)
        q_ref[:, sl] = _rope_block(q[:, sl], tab_ref, half).astype(q_ref.dtype)
        k_ref[:, sl] = (kk[:, sl] + kr).astype(k_ref.dtype)
    v_ref[...] = _dot(ckvn, wv_ref[...]).astype(v_ref.dtype)


def _mla_attn_kernel(q_ref, k_ref, v_ref, o_ref, vt_ref, *state, tq, scale):
    qi = pl.program_id(2)

    @pl.when(qi == 0)
    def _():
        _transpose_chunks(v_ref, vt_ref, tq)

    q_t = [q_ref[:, hh * LANES:(hh + 1) * LANES].astype(F32).T.astype(BF16) for hh in range(2)]
    key = lax.broadcasted_iota(jnp.int32, (tq, tq), 0)
    qry = lax.broadcasted_iota(jnp.int32, (tq, tq), 1)
    causal = key <= qry

    def scores(hh, c):
        return _dot(k_ref[pl.ds(pl.multiple_of(c * tq, tq), tq), hh * LANES:(hh + 1) * LANES], q_t[hh])

    def values(hh, c):
        return vt_ref[c, hh * B_V:(hh + 1) * B_V, :]

    res = _flash_causal_t(scores, values, 2, qi, causal, state, scale)
    o_t = jnp.concatenate([acc / l for l, acc in res], axis=0)
    o_ref[...] = o_t.T.astype(o_ref.dtype)


def _mixer_b(h, w_down, q_norm, kv_norm, w_uq, w_ukv, w_o, ln_g, ln_b, alpha, batch, seq):
    t, d = h.shape
    dq = B_NOPE + B_ROPE
    pad = LANES - dq
    wd = jnp.concatenate([w_down[:, :B_Q_RANK + B_KV_RANK], jnp.zeros((d, B_NOPE), F32),
                          w_down[:, B_Q_RANK + B_KV_RANK:], jnp.zeros((d, pad), F32)], axis=1).astype(BF16)
    wq = jnp.pad(w_uq.reshape(B_Q_RANK, B_HEADS, dq), ((0, 0), (0, 0), (0, pad))
                 ).reshape(B_Q_RANK, B_HEADS * LANES).astype(BF16)
    wkv = w_ukv.reshape(B_KV_RANK, B_HEADS, B_NOPE + B_V)
    wk = jnp.pad(wkv[:, :, :B_NOPE], ((0, 0), (0, 0), (0, LANES - B_NOPE))
                 ).reshape(B_KV_RANK, B_HEADS * LANES).astype(BF16)
    wv = wkv[:, :, B_NOPE:].reshape(B_KV_RANK, B_HEADS * B_V).astype(BF16)
    tab = _rope_tables(seq, B_ROPE, LANES, B_NOPE)
    tm = PROJ_ROWS
    spb = seq // tm
    full = lambda a: pl.BlockSpec(a.shape, lambda i: (0,) * a.ndim)
    qn, kvn = q_norm.reshape(1, -1), kv_norm.reshape(1, -1)
    q, k, v = pl.pallas_call(
        _mla_proj_kernel,
        grid=(t // tm,),
        in_specs=[pl.BlockSpec((tm, d), lambda i: (i, 0)), full(wd), full(qn), full(kvn),
                  full(wq), full(wk), full(wv),
                  pl.BlockSpec((3, tm, LANES), lambda i: (0, i % spb, 0))],
        out_specs=[pl.BlockSpec((tm, B_HEADS * LANES), lambda i: (i, 0)),
                   pl.BlockSpec((tm, B_HEADS * LANES), lambda i: (i, 0)),
                   pl.BlockSpec((tm, B_HEADS * B_V), lambda i: (i, 0))],
        out_shape=[jax.ShapeDtypeStruct((t, B_HEADS * LANES), BF16),
                   jax.ShapeDtypeStruct((t, B_HEADS * LANES), BF16),
                   jax.ShapeDtypeStruct((t, B_HEADS * B_V), BF16)],
        compiler_params=_params("parallel"),
        name="mla_proj",
    )(h, wd, qn, kvn, wq, wk, wv, tab)

    tq = min(FLASH_ROWS, seq)
    nq = seq // tq
    o = pl.pallas_call(
        functools.partial(_mla_attn_kernel, tq=tq, scale=dq ** -0.5),
        grid=(batch, B_HEADS // 2, nq),
        in_specs=[pl.BlockSpec((tq, 2 * LANES), lambda b, p, i: (b * nq + i, p)),
                  pl.BlockSpec((seq, 2 * LANES), lambda b, p, i: (b, p)),
                  pl.BlockSpec((seq, LANES), lambda b, p, i: (b, p))],
        out_specs=pl.BlockSpec((tq, LANES), lambda b, p, i: (b * nq + i, p)),
        out_shape=jax.ShapeDtypeStruct((t, B_HEADS * B_V), BF16),
        scratch_shapes=[pltpu.VMEM((nq, LANES, tq), BF16)] + _flash_scratch(2, B_V, tq, tq),
        compiler_params=_params("parallel", "parallel", "arbitrary"),
        name="mla_attention",
    )(q, k, v)
    return _outproj_ln(o, w_o.astype(BF16), h, ln_g, ln_b, alpha)


def _gelu_tanh(x):
    return x * (0.5 * (1.0 + jnp.tanh(math.sqrt(2.0 / math.pi) * (x + 0.044715 * (x * x * x)))))


def _compress_kernel(x_ref, pe_ref, w1_ref, w2_ref, o_ref):
    n = x_ref.shape[0] // C_CMP_STRIDE
    a = b = None
    for l in range(C_CMP_STRIDE):
        y = x_ref[pl.ds(l, n, stride=C_CMP_STRIDE), :]
        ta = _dot((y + pe_ref[l:l + 1, :]).astype(BF16), w1_ref[l])
        tb = _dot((y + pe_ref[C_CMP_STRIDE + l:C_CMP_STRIDE + l + 1, :]).astype(BF16), w1_ref[C_CMP_STRIDE + l])
        a = ta if a is None else a + ta
        b = tb if b is None else b + tb
    hid = a + pltpu.roll(b, n - 1, 0)
    o_ref[0] = _dot(_gelu_tanh(hid).astype(BF16), w2_ref[...]).astype(o_ref.dtype)


def _compress(kvc, which, pe, w1, w2, batch, seq):
    n = seq // C_CMP_STRIDE
    hp = LANES // HEAD_DIM
    pairs = C_KV_HEADS // hp
    eye = jnp.eye(hp, dtype=F32)
    w1bd = jnp.einsum("lij,gh->lgihj", w1.reshape(C_CMP_LEN, HEAD_DIM, C_CMP_HIDDEN), eye)
    w1bd = w1bd.reshape(C_CMP_LEN, LANES, hp * C_CMP_HIDDEN).astype(BF16)
    w2bd = jnp.einsum("ij,gh->gihj", jnp.concatenate([w2, w2], axis=1), eye)
    w2bd = w2bd.reshape(hp * C_CMP_HIDDEN, hp * LANES).astype(BF16)
    return pl.pallas_call(
        _compress_kernel,
        grid=(batch, pairs),
        in_specs=[pl.BlockSpec((seq, LANES), lambda i, j: (i, which * pairs + j)),
                  pl.BlockSpec((C_CMP_LEN, LANES), lambda i, j: (0, 0)),
                  pl.BlockSpec(w1bd.shape, lambda i, j: (0, 0, 0)),
                  pl.BlockSpec(w2bd.shape, lambda i, j: (0, 0))],
        out_specs=pl.BlockSpec((1, n, hp * LANES), lambda i, j: (i, 0, j)),
        out_shape=jax.ShapeDtypeStruct((batch, n, C_KV_HEADS * LANES), BF16),
        compiler_params=_params("parallel", "parallel"),
        name="nsa_compress",
    )(kvc, jnp.tile(pe, (1, hp)), w1bd, w2bd)


def _nsa_attn_kernel(q_ref, ks_ref, vs_ref, kw_ref, vw_ref, kc_ref, vc_ref, ov_ref, ex_ref, gt_ref,
                     o_ref, vst_ref, vwt_ref, vct_ref, *state, tq, tk, seq):
    qi = pl.program_id(2)
    q0 = qi * tq
    group = C_HEADS // C_KV_HEADS
    n_sel_blocks = seq // C_SEL_LEN
    nc = kc_ref.shape[1]

    @pl.when(qi == 0)
    def _():
        _transpose_chunks(vs_ref, vst_ref, tk)
        _transpose_chunks(vw_ref, vwt_ref, tq)
        vct_ref[...] = vc_ref[0].astype(F32).T.astype(vct_ref.dtype)

    top = lax.broadcasted_iota(jnp.int32, (LANES, tq), 0) < HEAD_DIM
    qa_t = q_ref[:, :LANES].astype(F32).T
    qb_t = q_ref[:, LANES:].astype(F32).T
    qs_t = jnp.concatenate([jnp.where(top, qa_t, 0.0), jnp.where(top, 0.0, qa_t),
                            jnp.where(top, qb_t, 0.0), jnp.where(top, 0.0, qb_t)], axis=1).astype(BF16)
    rep = lambda a: jnp.concatenate([a] * group, axis=1)

    def masked_softmax(s, valid):
        s = jnp.where(valid, s, NEG)
        e = jnp.exp(s - jnp.maximum(jnp.max(s, 0, keepdims=True), MAX_FLOOR))
        return e / jnp.maximum(jnp.sum(e, 0, keepdims=True), 1e-30)

    t_c = q0 + lax.broadcasted_iota(jnp.int32, (nc, tq), 1)
    n_c = lax.broadcasted_iota(jnp.int32, (nc, tq), 0)
    c_valid = rep(n_c * C_CMP_STRIDE + (C_CMP_LEN - 1) <= t_c)
    p_cmp = masked_softmax(_dot(kc_ref[0], qs_t), c_valid).astype(BF16)
    o_cmp = _dot(vct_ref[...], p_cmp)
    imp4 = _dot(ov_ref[...], p_cmp)
    imp = imp4[:n_sel_blocks, :tq]
    for r in range(1, group):
        imp = imp + imp4[:n_sel_blocks, r * tq:(r + 1) * tq]

    blk = lax.broadcasted_iota(jnp.int32, (n_sel_blocks, tq), 0)
    cur = (q0 + lax.broadcasted_iota(jnp.int32, (n_sel_blocks, tq), 1)) // C_SEL_LEN
    imp = jnp.where((blk == 0) | (blk == cur) | (blk == cur - 1), jnp.inf, imp)
    imp = jnp.where(blk <= cur, imp, -jnp.inf)
    rank = jnp.zeros((n_sel_blocks, tq), F32)
    for i in range(n_sel_blocks):
        r_i = imp[i:i + 1, :]
        rank = rank + jnp.where(blk > i, jnp.where(r_i >= imp, 1.0, 0.0), jnp.where(r_i > imp, 1.0, 0.0))
    sel = jnp.where(rank < min(C_N_SEL, n_sel_blocks), 0.0, NEG)
    sel = jnp.concatenate([sel, jnp.zeros((LANES - n_sel_blocks, tq), F32)], axis=0)
    qs_aug = jnp.concatenate([qs_t, rep(sel).astype(BF16)], axis=0)

    span = C_WINDOW + tq
    start = pl.multiple_of(jnp.maximum(q0 - C_WINDOW, 0), tq)
    t_w = q0 + lax.broadcasted_iota(jnp.int32, (span, tq), 1)
    dist = t_w - (start + lax.broadcasted_iota(jnp.int32, (span, tq), 0))
    w_valid = rep((dist >= 0) & (dist < C_WINDOW))
    p_win = masked_softmax(_dot(kw_ref[pl.ds(start, span), :], qs_t), w_valid).astype(BF16)
    o_win = _dot(vwt_ref[start // tq], p_win[:tq])
    for cb in range(1, span // tq):
        o_win = o_win + _dot(vwt_ref[start // tq + cb], p_win[cb * tq:(cb + 1) * tq])

    t_k = q0 + lax.broadcasted_iota(jnp.int32, (tk, tq), 1)
    k_k = lax.broadcasted_iota(jnp.int32, (tk, tq), 0)

    def sel_scores(ch, c):
        base = pl.multiple_of(c * tk, tk)
        k_aug = jnp.concatenate([ks_ref[pl.ds(base, tk), :], ex_ref[c]], axis=1)
        return jnp.where(rep(base + k_k <= t_k), _dot(k_aug, qs_aug), NEG)

    (l, acc), = _flash_causal_t(sel_scores, lambda ch, c: vst_ref[c], 1, (q0 + tq - 1) // tk, None, state)
    o_slc = acc / jnp.maximum(l, 1e-30)

    g_t = gt_ref[...].T
    outs = []
    for r in range(group):
        cols = slice(r * tq, (r + 1) * tq)
        outs.append(o_cmp[:, cols] * g_t[3 * r:3 * r + 1] + o_slc[:, cols] * g_t[3 * r + 1:3 * r + 2]
                    + o_win[:, cols] * g_t[3 * r + 2:3 * r + 3])
    o_ref[:, :LANES] = jnp.where(top, outs[0], outs[1]).T.astype(o_ref.dtype)
    o_ref[:, LANES:] = jnp.where(top, outs[2], outs[3]).T.astype(o_ref.dtype)


def _mixer_c(h, w_in, pos_k, pos_v, wk1, wk2, wv1, wv2, w_o, ln_g, ln_b, alpha, batch, seq):
    t, d = h.shape
    qd, kd = C_HEADS * HEAD_DIM, C_KV_HEADS * HEAD_DIM
    group = C_HEADS // C_KV_HEADS
    kv = lambda i: w_in[:, qd + i * kd:qd + (i + 1) * kd]
    wg = jnp.pad(w_in[:, qd + 6 * kd:].reshape(d, C_KV_HEADS, group * 3),
                 ((0, 0), (0, 0), (0, LANES - group * 3))).reshape(d, C_KV_HEADS * LANES)
    w = jnp.concatenate([w_in[:, :qd], _dup_cols(kv(2), C_KV_HEADS), _dup_cols(kv(4), C_KV_HEADS),
                         _dup_cols(kv(3), C_KV_HEADS), _dup_cols(kv(5), C_KV_HEADS),
                         kv(0), kv(1), wg], axis=1).astype(BF16)
    tab = _rope_tables(seq, ROT_DIM, HEAD_DIM, 0)
    kw2 = 2 * kd
    segs = [(0, qd, "rope", HEAD_DIM ** -0.5), (qd, 2 * kw2, "rope", 1.0), (qd + 2 * kw2, 2 * kw2, "plain", 1.0),
            (qd + 4 * kw2, 2 * kd, "plain", 1.0), (qd + 4 * kw2 + 2 * kd, C_KV_HEADS * LANES, "sigmoid", 1.0)]
    q, ksw, vsw, kvc, gates = _proj(h, w, tab, segs, [BF16, BF16, BF16, F32, F32], seq, ROT_DIM // 2)
    k_cmp = _compress(kvc, 0, pos_k, wk1, wk2, batch, seq)
    v_cmp = _compress(kvc, 1, pos_v, wv1, wv2, batch, seq)

    nc = seq // C_CMP_STRIDE
    nsb = seq // C_SEL_LEN
    tq = 128
    tk = min(512, seq)
    nq = seq // tq
    cs = np.arange(nc)[None, :] * C_CMP_STRIDE
    ss = np.arange(LANES)[:, None] * C_SEL_LEN
    overlap = ((cs <= ss + C_SEL_LEN - 1) & (ss <= cs + C_CMP_LEN - 1) & (np.arange(LANES)[:, None] < nsb))
    overlap = jnp.asarray(overlap, BF16)
    key_blk = (np.arange(seq) // C_SEL_LEN).reshape(seq // tk, tk, 1)
    expand = jnp.asarray(key_blk == np.arange(LANES)[None, None, :], BF16)

    per_bg = lambda width: pl.BlockSpec((seq, LANES), lambda b, g, i, width=width: (b, width + g))
    o = pl.pallas_call(
        functools.partial(_nsa_attn_kernel, tq=tq, tk=tk, seq=seq),
        grid=(batch, C_KV_HEADS, nq),
        in_specs=[pl.BlockSpec((tq, 2 * LANES), lambda b, g, i: (b * nq + i, g)),
                  per_bg(0), per_bg(0), per_bg(C_KV_HEADS), per_bg(C_KV_HEADS),
                  pl.BlockSpec((1, nc, LANES), lambda b, g, i: (b, 0, g)),
                  pl.BlockSpec((1, nc, LANES), lambda b, g, i: (b, 0, g)),
                  pl.BlockSpec((LANES, nc), lambda b, g, i: (0, 0)),
                  pl.BlockSpec((seq // tk, tk, LANES), lambda b, g, i: (0, 0, 0)),
                  pl.BlockSpec((tq, LANES), lambda b, g, i: (b * nq + i, g))],
        out_specs=pl.BlockSpec((tq, 2 * LANES), lambda b, g, i: (b * nq + i, g)),
        out_shape=jax.ShapeDtypeStruct((t, qd), BF16),
        scratch_shapes=[pltpu.VMEM((seq // tk, LANES, tk), BF16), pltpu.VMEM((seq // tq, LANES, tq), BF16),
                        pltpu.VMEM((LANES, nc), BF16)] + _flash_scratch(1, LANES, group * tq, tk),
        compiler_params=_params("parallel", "parallel", "arbitrary"),
        name="nsa_attention",
    )(q, ksw, vsw, ksw, vsw, k_cmp, v_cmp, overlap, expand, gates)
    return _outproj_ln(o, w_o.astype(BF16), h, ln_g, ln_b, alpha)


def _diff_attn_kernel(q_ref, k_ref, v_ref, lam_ref, sub_ref, o_ref, vt_ref, *state, tq, lam_init):
    qi = pl.program_id(2)

    n_heads = q_ref.shape[1] // LANES

    @pl.when(qi == 0)
    def _():
        for hd in range(n_heads):
            _transpose_chunks(v_ref.at[:, hd * LANES:(hd + 1) * LANES], vt_ref.at[hd], tq)

    lam = (jnp.exp(jnp.sum(lam_ref[0:1, :] * lam_ref[1:2, :], -1, keepdims=True))
           - jnp.exp(jnp.sum(lam_ref[2:3, :] * lam_ref[3:4, :], -1, keepdims=True)) + lam_init)
    top = lax.broadcasted_iota(jnp.int32, (LANES, tq), 0) < D_SUB
    qs_t = []
    for hd in range(n_heads):
        q_t = q_ref[:, hd * LANES:(hd + 1) * LANES].astype(F32).T
        qs_t.append(jnp.concatenate([jnp.where(top, q_t, 0.0), jnp.where(top, 0.0, q_t)], axis=1).astype(BF16))
    key = lax.broadcasted_iota(jnp.int32, (tq, tq), 0)
    qry = lax.broadcasted_iota(jnp.int32, (tq, tq), 1)
    causal = jnp.concatenate([key <= qry] * 2, axis=1)

    def scores(hd, c):
        return _dot(k_ref[pl.ds(pl.multiple_of(c * tq, tq), tq), hd * LANES:(hd + 1) * LANES], qs_t[hd])

    def values(hd, c):
        return vt_ref[hd, c]

    res = _flash_causal_t(scores, values, n_heads, qi, causal, state)
    for hd in range(n_heads):
        l, acc = res[hd]
        o = (acc / l).T
        o = o[:tq] - lam * o[tq:]
        o = o * lax.rsqrt(jnp.mean(o * o, -1, keepdims=True) + RMS_EPS) * sub_ref[...]
        o_ref[:, hd * LANES:(hd + 1) * LANES] = (o * (1.0 - lam_init)).astype(o_ref.dtype)


def _mixer_d(h, w_in, lq1, lk1, lq2, lk2, subln, w_o, ln_g, ln_b, alpha, layer_idx, batch, seq):
    t, d = h.shape
    qd = D_HEADS * 2 * D_SUB
    tab = _rope_tables(seq, ROT_DIM, D_SUB, 0)
    segs = [(0, qd, "rope", D_SUB ** -0.5), (qd, qd, "rope", 1.0), (2 * qd, qd, "plain", 1.0)]
    q, k, v = _proj(h, w_in.astype(BF16), tab, segs, [BF16, BF16, BF16], seq, ROT_DIM // 2)
    lam_init = 0.8 - 0.6 * math.exp(-0.3 * layer_idx)
    lam_in = jnp.stack([lq1, lk1, lq2, lk2]).astype(F32)
    tq = min(FLASH_ROWS, seq)
    nq = seq // tq
    hps = 2
    o = pl.pallas_call(
        functools.partial(_diff_attn_kernel, tq=tq, lam_init=lam_init),
        grid=(batch, D_HEADS // hps, nq),
        in_specs=[pl.BlockSpec((tq, hps * LANES), lambda b, hd, i: (b * nq + i, hd)),
                  pl.BlockSpec((seq, hps * LANES), lambda b, hd, i: (b, hd)),
                  pl.BlockSpec((seq, hps * LANES), lambda b, hd, i: (b, hd)),
                  pl.BlockSpec((4, D_SUB), lambda b, hd, i: (0, 0)),
                  pl.BlockSpec((1, 2 * D_SUB), lambda b, hd, i: (0, 0))],
        out_specs=pl.BlockSpec((tq, hps * LANES), lambda b, hd, i: (b * nq + i, hd)),
        out_shape=jax.ShapeDtypeStruct((t, qd), BF16),
        scratch_shapes=[pltpu.VMEM((hps, nq, LANES, tq), BF16)] + _flash_scratch(hps, LANES, 2 * tq, tq),
        compiler_params=_params("parallel", "parallel", "arbitrary"),
        name="diff_attention",
    )(q, k, v, lam_in, subln.reshape(1, -1))
    return _outproj_ln(o, w_o.astype(BF16), h, ln_g, ln_b, alpha)


def _router_kernel(x_ref, w_ref, o_ref):
    xb = x_ref[...].astype(BF16)
    logits = _dot(xb, w_ref[...])
    lg = logits[:, :LANES].T[:8]
    le = logits[:, LANES:].T[:M_EXPERTS]
    far = 4 * LANES

    def softmax(x, valid):
        m = jnp.max(jnp.where(valid, x, NEG), 0, keepdims=True)
        e = jnp.where(valid, jnp.exp(x - m), 0.0)
        return e / jnp.sum(e, 0, keepdims=True)

    def first_max(p, valid, row):
        top = jnp.max(jnp.where(valid, p, -1.0), 0, keepdims=True)
        idx = jnp.min(jnp.where(valid & (p == top), row, far), 0, keepdims=True)
        return top, idx

    g_row = lax.broadcasted_iota(jnp.int32, lg.shape, 0)
    e_row = lax.broadcasted_iota(jnp.int32, le.shape, 0)
    g_valid = g_row < M_GROUPS
    g_w, g_idx = first_max(softmax(lg, g_valid), g_valid, g_row)
    e_valid = (e_row >= g_idx * M_PER_GROUP) & (e_row < (g_idx + 1) * M_PER_GROUP)
    pe = softmax(le, e_valid)
    w0, i0 = first_max(pe, e_valid, e_row)
    w1, i1 = first_max(pe, e_valid & (e_row != i0), e_row)
    tot = w0 + w1
    out = jnp.where(g_row == 0, i0.astype(F32), 0.0)
    out = jnp.where(g_row == 1, i1.astype(F32), out)
    out = jnp.where(g_row == 2, g_w * w0 / tot, out)
    out = jnp.where(g_row == 3, g_w * w1 / tot, out)
    o_ref[...] = out


SUBLANES = 8


def _token_gather(src_hbm, idx_ref, buf, sem):
    for r in range(buf.shape[0] // SUBLANES):
        pltpu.make_async_copy(src_hbm.at[pl.ds(pl.multiple_of(idx_ref[0, 0, r], SUBLANES), SUBLANES), :],
                              buf.at[pl.ds(r * SUBLANES, SUBLANES), :], sem).start()


def _row_gather_wait(src_hbm, buf, sem):
    pltpu.make_async_copy(src_hbm.at[pl.ds(0, buf.shape[0]), :], buf, sem).wait()


def _token_rows(buf, tm):
    return jnp.concatenate([buf[pl.ds(c, tm, stride=SUBLANES), :] for c in range(SUBLANES)], axis=1)


def _row_scatter(buf, idx_ref, dst_hbm, sem):
    for r in range(buf.shape[0]):
        pltpu.make_async_copy(buf.at[pl.ds(r, 1), :], dst_hbm.at[pl.ds(idx_ref[0, 0, r], 1), :], sem).start()


def _row_scatter_wait(buf, dst_hbm, sem):
    pltpu.make_async_copy(buf, dst_hbm.at[pl.ds(0, buf.shape[0]), :], sem).wait()


def _moe_kernel(eid_ref, used_ref, src0_ref, src_ref, dstp_ref, h_hbm, wg_ref, wu_ref, wd_ref, out_hbm,
                xbuf0, xbuf1, ybuf0, ybuf1, wgb, wub, wdb, gsem, ssem, *, tm):
    t = pl.program_id(0)
    used = used_ref[0]
    xbuf = (xbuf0, xbuf1)
    ybuf = (ybuf0, ybuf1)

    @pl.when(t == 0)
    def _():
        _token_gather(h_hbm, src0_ref, xbuf0, gsem.at[0])
        ybuf1[...] = jnp.zeros(ybuf1.shape, F32)
        fill = pltpu.make_async_copy(ybuf1, out_hbm.at[pl.ds(out_hbm.shape[0] - 2 * tm, tm), :], ssem.at[0])
        fill.start()
        fill.wait()

    @pl.when((t < used) & ((t == 0) | (eid_ref[t] != eid_ref[jnp.maximum(t - 1, 0)])))
    def _():
        wgb[...] = wg_ref[0, 0].astype(BF16)
        wub[...] = wu_ref[0, 0].astype(BF16)
        wdb[...] = wd_ref[0, 0].astype(BF16)

    for s in range(2):
        @pl.when((t < used) & (t % 2 == s))
        def _(s=s):
            _row_gather_wait(h_hbm, xbuf[s], gsem.at[s])

            @pl.when(t >= 1)
            def _():
                _row_scatter_wait(ybuf[s], out_hbm, ssem.at[s])

            _row_scatter(ybuf[1 - s], dstp_ref, out_hbm, ssem.at[1 - s])
            _token_gather(h_hbm, src_ref, xbuf[1 - s], gsem.at[1 - s])
            xb = _token_rows(xbuf[s], tm).astype(BF16)
            gate = _dot(xb, wgb[...])
            up = _dot(xb, wub[...])
            hid = (gate * (1.0 / (1.0 + jnp.exp(-gate))) * up).astype(BF16)
            ybuf[s][...] = _dot(hid, wdb[...])

        @pl.when((t == used) & (t % 2 == s))
        def _(s=s):
            _row_scatter(ybuf[1 - s], dstp_ref, out_hbm, ssem.at[1 - s])
            _row_scatter_wait(ybuf[s], out_hbm, ssem.at[s])
            _row_scatter_wait(ybuf[1 - s], out_hbm, ssem.at[1 - s])
            _row_gather_wait(h_hbm, xbuf[s], gsem.at[s])


def _combine_ln_kernel(h_ref, y0_ref, y1_ref, w_ref, g_ref, b_ref, o_ref, *, alpha):
    w = w_ref[...]
    y = w[:, 0:1] * y0_ref[...] + w[:, 1:2] * y1_ref[...]
    o_ref[...] = _layer_norm(alpha * h_ref[...] + y, g_ref[...], b_ref[...])


def _hier_moe_ln(h, w_group, w_expert, w_gate, w_up, w_down, layer, ln_g, ln_b, alpha):
    t, d = h.shape
    tm = MOE_ROWS
    wr = jnp.concatenate([jnp.pad(w_group, ((0, 0), (0, LANES - M_GROUPS))),
                          jnp.pad(w_expert, ((0, 0), (0, LANES - M_EXPERTS)))], axis=1).astype(BF16)
    routed = pl.pallas_call(
        _router_kernel,
        grid=(t // LN_ROWS,),
        in_specs=[pl.BlockSpec((LN_ROWS, d), lambda i: (i, 0)),
                  pl.BlockSpec((d, 2 * LANES), lambda i: (0, 0))],
        out_specs=pl.BlockSpec((8, LN_ROWS), lambda i: (0, i)),
        out_shape=jax.ShapeDtypeStruct((8, t), F32),
        compiler_params=_params("parallel"),
        name="moe_router",
    )(h, wr)

    n_rows = t * M_TOPK
    n_tiles = n_rows // tm + M_EXPERTS
    expert = routed[:M_TOPK].T.astype(jnp.int32).reshape(-1)
    weight = routed[M_TOPK:2 * M_TOPK].T
    order = jnp.argsort(expert).astype(jnp.int32)
    experts = jnp.arange(M_EXPERTS, dtype=jnp.int32)[None, :]
    sizes = jnp.sum(expert[:, None] == experts, axis=0, dtype=jnp.int32)
    tiles = (sizes + tm - 1) // tm
    tile_end = jnp.cumsum(tiles)
    seg_start = jnp.cumsum(sizes) - sizes
    used = tile_end[-1]
    tile_ids = jnp.arange(n_tiles, dtype=jnp.int32)
    tile_eid = jnp.sum(jnp.minimum(tile_ids, used - 1)[:, None] >= tile_end[None, :], axis=1).astype(jnp.int32)
    tile_first = (tile_end - tiles)[tile_eid]
    lane_r = jnp.arange(tm, dtype=jnp.int32)[None, :]
    offs = (tile_ids - tile_first)[:, None] * tm + lane_r
    valid = (offs < sizes[tile_eid][:, None]) & (tile_ids < used)[:, None]
    row = order[jnp.clip(seg_start[tile_eid][:, None] + offs, 0, n_rows - 1)]
    src = jnp.where(valid, (row // M_TOPK) * SUBLANES, 0).reshape(n_tiles, 1, tm)
    h_tok = h.reshape(t * SUBLANES, d // SUBLANES)
    trash = n_rows + (tile_ids % 2)[:, None] * tm + lane_r
    dst = jnp.where(valid, (row % M_TOPK) * t + row // M_TOPK, trash)
    dstp = jnp.concatenate([n_rows + tm + lane_r, dst], axis=0).reshape(n_tiles + 1, 1, tm)

    wspec = lambda shape: pl.BlockSpec((1, 1) + shape, lambda i, eid, used: (layer, eid[i], 0, 0))
    idx_spec = pl.BlockSpec((1, 1, tm), lambda i, eid, used: (i, 0, 0), memory_space=pltpu.SMEM)
    nxt_spec = pl.BlockSpec((1, 1, tm), lambda i, eid, used: (jnp.minimum(i + 1, n_tiles - 1), 0, 0),
                            memory_space=pltpu.SMEM)
    y = pl.pallas_call(
        functools.partial(_moe_kernel, tm=tm),
        grid_spec=pltpu.PrefetchScalarGridSpec(
            num_scalar_prefetch=2,
            grid=(n_tiles,),
            in_specs=[idx_spec, nxt_spec, idx_spec,
                      pl.BlockSpec(memory_space=pl.ANY),
                      wspec((d, M_HIDDEN)), wspec((d, M_HIDDEN)), wspec((M_HIDDEN, d))],
            out_specs=pl.BlockSpec(memory_space=pl.ANY),
            scratch_shapes=[pltpu.VMEM((tm * SUBLANES, d // SUBLANES), F32),
                            pltpu.VMEM((tm * SUBLANES, d // SUBLANES), F32),
                            pltpu.VMEM((tm, d), F32), pltpu.VMEM((tm, d), F32),
                            pltpu.VMEM((d, M_HIDDEN), BF16), pltpu.VMEM((d, M_HIDDEN), BF16),
                            pltpu.VMEM((M_HIDDEN, d), BF16),
                            pltpu.SemaphoreType.DMA((2,)), pltpu.SemaphoreType.DMA((2,))]),
        out_shape=jax.ShapeDtypeStruct((n_rows + 2 * tm, d), F32),
        compiler_params=_params("arbitrary"),
        name="moe_experts",
    )(tile_eid, used.reshape(1).astype(jnp.int32), src, src, dstp, h_tok, w_gate, w_up, w_down)

    tc = LN_ROWS
    nt = t // tc
    return pl.pallas_call(
        functools.partial(_combine_ln_kernel, alpha=alpha),
        grid=(nt,),
        in_specs=[pl.BlockSpec((tc, d), lambda i: (i, 0)),
                  pl.BlockSpec((tc, d), lambda i: (i, 0)),
                  pl.BlockSpec((tc, d), lambda i: (nt + i, 0)),
                  pl.BlockSpec((tc, M_TOPK), lambda i: (i, 0)),
                  pl.BlockSpec((1, d), lambda i: (0, 0)),
                  pl.BlockSpec((1, d), lambda i: (0, 0))],
        out_specs=pl.BlockSpec((tc, d), lambda i: (i, 0)),
        out_shape=jax.ShapeDtypeStruct((t, d), F32),
        compiler_params=_params("parallel"),
        name="moe_combine_ln",
    )(h, y, y, weight, ln_g.reshape(1, d), ln_b.reshape(1, d))


def kernel(x, a_w_in, a_sinks, a_w_o, b_w_down, b_q_norm, b_kv_norm, b_w_uq, b_w_ukv, b_w_o, c_w_in, c_pos_k, c_pos_v, c_wk1, c_wk2, c_wv1, c_wv2, c_w_o, d_w_in, d_lq1, d_lk1, d_lq2, d_lk2, d_subln, d_w_o, moe_w_group, moe_w_expert, moe_w_gate, moe_w_up, moe_w_down, ln_g, ln_b):
    batch, seq, d = x.shape
    depth = ln_g.shape[0]
    alpha = (2 * depth) ** 0.25
    h = x.reshape(batch * seq, d)
    for i in range(depth):
        kind, j = i % N_MIXERS, i // N_MIXERS
        g, b = ln_g[i, 0], ln_b[i, 0]
        if kind == 0:
            h = _mixer_a(h, a_w_in[j], a_sinks[j], a_w_o[j], g, b, alpha, batch, seq)
        elif kind == 1:
            h = _mixer_b(h, b_w_down[j], b_q_norm[j], b_kv_norm[j], b_w_uq[j], b_w_ukv[j], b_w_o[j],
                         g, b, alpha, batch, seq)
        elif kind == 2:
            h = _mixer_c(h, c_w_in[j], c_pos_k[j], c_pos_v[j], c_wk1[j], c_wk2[j], c_wv1[j], c_wv2[j],
                         c_w_o[j], g, b, alpha, batch, seq)
        else:
            h = _mixer_d(h, d_w_in[j], d_lq1[j], d_lk1[j], d_lq2[j], d_lk2[j], d_subln[j], d_w_o[j],
                         g, b, alpha, i, batch, seq)
        h = _hier_moe_ln(h, moe_w_group[i], moe_w_expert[i], moe_w_gate, moe_w_up, moe_w_down, i,
                         ln_g[i, 1], ln_b[i, 1], alpha)
    return h.reshape(batch, seq, d)
```

```python
import functools
import math

import numpy as np
import jax
import jax.numpy as jnp
from jax import lax
from jax.experimental import pallas as pl
from jax.experimental.pallas import tpu as pltpu

F32 = jnp.float32
BF16 = jnp.bfloat16

HEAD_DIM = 64
ROPE_THETA = 500000.0
ROT_DIM = HEAD_DIM // 4
A_HEADS, A_KV_HEADS, A_WINDOW = 16, 4, 128
B_HEADS, B_Q_RANK, B_KV_RANK, B_NOPE, B_ROPE, B_V = 16, 384, 256, 64, 32, 64
C_HEADS, C_KV_HEADS = 16, 4
C_CMP_LEN, C_CMP_STRIDE, C_CMP_HIDDEN = 32, 16, 128
C_SEL_LEN, C_N_SEL, C_WINDOW = 64, 16, 512
D_HEADS, D_SUB = 8, 64
M_GROUPS, M_PER_GROUP, M_TOPK, M_HIDDEN = 4, 8, 2, 512
M_EXPERTS = M_GROUPS * M_PER_GROUP
N_MIXERS = 4
LN_EPS = 1e-5
RMS_EPS = 1e-6

LANES = 128
NEG = -1e30
MAX_FLOOR = -1e20
VMEM_LIMIT = 48 * 1024 * 1024

PROJ_ROWS = 256
LN_ROWS = 512
MOE_ROWS = 128
FLASH_ROWS = 512


def _params(*sem):
    return pltpu.CompilerParams(dimension_semantics=sem, vmem_limit_bytes=VMEM_LIMIT)


def _dot(a, b):
    return jnp.dot(a, b, preferred_element_type=F32)


def _dot_t(a, b):
    return lax.dot_general(a, b, (((1,), (1,)), ((), ())), preferred_element_type=F32)


def _rope_tables(seq, rot_dim, period, off):
    half = rot_dim // 2
    inv_freq = 1.0 / (ROPE_THETA ** (jnp.arange(half, dtype=F32) * (2.0 / rot_dim)))
    ang = jnp.arange(seq, dtype=F32)[:, None] * inv_freq[None, :]
    cos, sin = jnp.cos(ang), jnp.sin(ang)
    lane = np.arange(LANES) % period - off
    first = (lane >= 0) & (lane < half)
    second = (lane >= half) & (lane < rot_dim)
    idx = np.where(first, lane, np.where(second, lane - half, 0))
    cg, sg = cos[:, idx], sin[:, idx]
    c = jnp.where(first | second, cg, 1.0)
    sa = jnp.where(first, -sg, 0.0)
    sb = jnp.where(second, sg, 0.0)
    return jnp.stack([c, sa, sb])


def _rope_block(x, tab_ref, half):
    return (x * tab_ref[0] + pltpu.roll(x, LANES - half, 1) * tab_ref[1]
            + pltpu.roll(x, half, 1) * tab_ref[2])


def _proj_kernel(x_ref, w_ref, tab_ref, *out_refs, segs, half):
    xb = x_ref[...].astype(BF16)
    for (start, width, kind, scale), o_ref in zip(segs, out_refs):
        acc = _dot(xb, w_ref[:, start:start + width])
        if kind == "rope":
            for c in range(width // LANES):
                y = _rope_block(acc[:, c * LANES:(c + 1) * LANES], tab_ref, half)
                if scale != 1.0:
                    y = y * scale
                o_ref[:, c * LANES:(c + 1) * LANES] = y.astype(o_ref.dtype)
        elif kind == "sigmoid":
            o_ref[...] = (1.0 / (1.0 + jnp.exp(-acc))).astype(o_ref.dtype)
        else:
            o_ref[...] = acc.astype(o_ref.dtype)


def _proj(x, w, tab, segs, dtypes, seq, half):
    t, k = x.shape
    n = w.shape[1]
    tm = PROJ_ROWS
    spb = seq // tm
    out_shape = [jax.ShapeDtypeStruct((t, s[1]), d) for s, d in zip(segs, dtypes)]
    return pl.pallas_call(
        functools.partial(_proj_kernel, segs=tuple(segs), half=half),
        grid=(t // tm,),
        in_specs=[pl.BlockSpec((tm, k), lambda i: (i, 0)),
                  pl.BlockSpec((k, n), lambda i: (0, 0)),
                  pl.BlockSpec((3, tm, LANES), lambda i: (0, i % spb, 0))],
        out_specs=[pl.BlockSpec((tm, s[1]), lambda i: (i, 0)) for s in segs],
        out_shape=out_shape,
        compiler_params=_params("parallel"),
        name="proj",
    )(x, w, tab)


def _layer_norm(z, g, b):
    mu = jnp.mean(z, -1, keepdims=True)
    zc = z - mu
    var = jnp.mean(zc * zc, -1, keepdims=True)
    return zc * lax.rsqrt(var + LN_EPS) * g + b


def _outproj_ln_kernel(o_ref, w_ref, h_ref, g_ref, b_ref, out_ref, *, alpha):
    y = _dot(o_ref[...], w_ref[...])
    out_ref[...] = _layer_norm(alpha * h_ref[...] + y, g_ref[...], b_ref[...])


def _outproj_ln(o, w, h, g, b, alpha):
    t, k = o.shape
    d = w.shape[1]
    tm = LN_ROWS
    return pl.pallas_call(
        functools.partial(_outproj_ln_kernel, alpha=alpha),
        grid=(t // tm,),
        in_specs=[pl.BlockSpec((tm, k), lambda i: (i, 0)),
                  pl.BlockSpec((k, d), lambda i: (0, 0)),
                  pl.BlockSpec((tm, d), lambda i: (i, 0)),
                  pl.BlockSpec((1, d), lambda i: (0, 0)),
                  pl.BlockSpec((1, d), lambda i: (0, 0))],
        out_specs=pl.BlockSpec((tm, d), lambda i: (i, 0)),
        out_shape=jax.ShapeDtypeStruct((t, d), F32),
        compiler_params=_params("parallel"),
        name="outproj_ln",
    )(o, w, h, g.reshape(1, d), b.reshape(1, d))


def _stack_group_queries(q_ref, rows):
    lo = lax.broadcasted_iota(jnp.int32, (rows, LANES), 1) < HEAD_DIM
    qa, qb = q_ref[:, :LANES], q_ref[:, LANES:]
    z = jnp.zeros_like(qa)
    return jnp.concatenate([jnp.where(lo, qa, z), jnp.where(lo, z, qa),
                            jnp.where(lo, qb, z), jnp.where(lo, z, qb)], axis=0)


def _pair(lo_val, hi_val):
    lo = lax.broadcasted_iota(jnp.int32, lo_val.shape, 1) < HEAD_DIM
    return jnp.where(lo, lo_val, hi_val)


def _transpose_chunks(src_ref, dst_ref, chunk):
    for c in range(src_ref.shape[0] // chunk):
        dst_ref[c] = src_ref[c * chunk:(c + 1) * chunk, :].astype(F32).T.astype(dst_ref.dtype)


def _softmax_update_t(s, valid, m, l, scale=None):
    if valid is not None:
        s = jnp.where(valid, s, NEG)
    m_new = jnp.maximum(m, jnp.max(s, 0, keepdims=True))
    if scale is None:
        a = jnp.exp(m - m_new)
        p = jnp.exp(s - m_new)
    else:
        a = jnp.exp((m - m_new) * scale)
        p = jnp.exp((s - m_new) * scale)
    return m_new, a * l + jnp.sum(p, 0, keepdims=True), a, p.astype(BF16)


def _flash_scratch(n_chains, dv, m_cols, tk):
    return [pltpu.VMEM((n_chains, 3, 1, m_cols), F32), pltpu.VMEM((n_chains, dv, m_cols), F32),
            pltpu.VMEM((n_chains, 2, tk, m_cols), F32), pltpu.VMEM((n_chains, 2, tk, m_cols), BF16)]


def _flash_causal_t(score_fn, value_fn, n_chains, qi, causal, state, scale=None):
    st_ref, acc_ref, s_ref, p_ref = state
    for ch in range(n_chains):
        st_ref[ch, 0] = jnp.full(st_ref.shape[2:], MAX_FLOOR, F32)
        st_ref[ch, 1] = jnp.zeros(st_ref.shape[2:], F32)
        st_ref[ch, 2] = jnp.ones(st_ref.shape[2:], F32)
        acc_ref[ch] = jnp.zeros(acc_ref.shape[1:], F32)
        p_ref[ch, 1] = jnp.zeros(p_ref.shape[2:], BF16)
        s_ref[ch, 0] = score_fn(ch, 0)

    def half(c, cur, valid, last):
        nxt = 1 - cur
        for ch in range(n_chains):
            m, l, a, p = _softmax_update_t(s_ref[ch, cur], valid, st_ref[ch, 0], st_ref[ch, 1], scale)
            if not last:
                s_ref[ch, nxt] = score_fn(ch, c + 1)
            acc = st_ref[ch, 2] * acc_ref[ch] + _dot(value_fn(ch, jnp.maximum(c - 1, 0)), p_ref[ch, nxt])
            if last:
                acc = a * acc + _dot(value_fn(ch, c), p)
            else:
                p_ref[ch, cur] = p
                st_ref[ch, 2] = a
            acc_ref[ch] = acc
            st_ref[ch, 0] = m
            st_ref[ch, 1] = l

    def pair(j, carry):
        half(2 * j, 0, None, False)
        half(2 * j + 1, 1, None, False)
        return carry

    lax.fori_loop(0, qi // 2, pair, 0)

    @pl.when(qi % 2 == 1)
    def _():
        half(qi - 1, 0, None, False)
        half(qi, 1, causal, True)

    @pl.when(qi % 2 == 0)
    def _():
        half(qi, 0, causal, True)

    return [(st_ref[ch, 1], acc_ref[ch]) for ch in range(n_chains)]


def _swa_kernel(sink_ref, q_ref, kp_ref, kc_ref, vp_ref, vc_ref, o_ref):
    n = pl.program_id(1)
    blk = A_WINDOW
    i = lax.broadcasted_iota(jnp.int32, (blk, 2 * blk), 0)
    j = lax.broadcasted_iota(jnp.int32, (blk, 2 * blk), 1)
    dist = blk + i - j
    valid = (dist >= 0) & (dist < A_WINDOW) & ((n - 1) * blk + j >= 0)
    group = A_HEADS // A_KV_HEADS
    for g in range(A_KV_HEADS):
        qs = _stack_group_queries(q_ref.at[:, g * 2 * LANES:(g + 1) * 2 * LANES], blk)
        k = jnp.concatenate([kp_ref[:, g * LANES:(g + 1) * LANES],
                             kc_ref[:, g * LANES:(g + 1) * LANES]], axis=0)
        v = jnp.concatenate([vp_ref[:, g * LANES:(g + 1) * LANES],
                             vc_ref[:, g * LANES:(g + 1) * LANES]], axis=0)
        s = _dot_t(qs, k)
        ps = []
        for r in range(group):
            sink = sink_ref[g * group + r]
            sr = jnp.where(valid, s[r * blk:(r + 1) * blk], NEG)
            m = jnp.maximum(jnp.max(sr, -1, keepdims=True), sink)
            e = jnp.where(valid, jnp.exp(sr - m), 0.0)
            p = e / (jnp.sum(e, -1, keepdims=True) + jnp.exp(sink - m))
            ps.append(p.astype(BF16))
        o = _dot(jnp.concatenate(ps, axis=0), v)
        o_ref[:, g * 2 * LANES:g * 2 * LANES + LANES] = _pair(o[:blk], o[blk:2 * blk]).astype(o_ref.dtype)
        o_ref[:, g * 2 * LANES + LANES:(g + 1) * 2 * LANES] = _pair(
            o[2 * blk:3 * blk], o[3 * blk:]).astype(o_ref.dtype)


def _swa_attention(q, kd, vd, sinks, batch, seq):
    t = q.shape[0]
    blk = A_WINDOW
    nb = seq // blk
    qd = A_HEADS * HEAD_DIM
    kw = A_KV_HEADS * LANES
    cur = lambda b, n, s: (b * nb + n, 0)
    prev = lambda b, n, s: (b * nb + jnp.maximum(n - 1, 0), 0)
    return pl.pallas_call(
        _swa_kernel,
        grid_spec=pltpu.PrefetchScalarGridSpec(
            num_scalar_prefetch=1,
            grid=(batch, nb),
            in_specs=[pl.BlockSpec((blk, qd), cur),
                      pl.BlockSpec((blk, kw), prev), pl.BlockSpec((blk, kw), cur),
                      pl.BlockSpec((blk, kw), prev), pl.BlockSpec((blk, kw), cur)],
            out_specs=pl.BlockSpec((blk, qd), cur)),
        out_shape=jax.ShapeDtypeStruct((t, qd), BF16),
        compiler_params=_params("parallel", "parallel"),
        name="swa_attention",
    )(sinks.astype(F32), q, kd, kd, vd, vd)


def _dup_cols(w, heads):
    k = w.shape[0]
    w4 = w.reshape(k, heads, 1, HEAD_DIM)
    return jnp.broadcast_to(w4, (k, heads, 2, HEAD_DIM)).reshape(k, heads * 2 * HEAD_DIM)


def _mixer_a(h, w_in, sinks, w_o, ln_g, ln_b, alpha, batch, seq):
    qd, kd = A_HEADS * HEAD_DIM, A_KV_HEADS * HEAD_DIM
    w = jnp.concatenate([w_in[:, :qd], _dup_cols(w_in[:, qd:qd + kd], A_KV_HEADS),
                         _dup_cols(w_in[:, qd + kd:], A_KV_HEADS)], axis=1).astype(BF16)
    tab = _rope_tables(seq, ROT_DIM, HEAD_DIM, 0)
    segs = [(0, qd, "rope", HEAD_DIM ** -0.5), (qd, 2 * kd, "rope", 1.0), (qd + 2 * kd, 2 * kd, "plain", 1.0)]
    q, k2, v2 = _proj(h, w, tab, segs, [BF16, BF16, BF16], seq, ROT_DIM // 2)
    o = _swa_attention(q, k2, v2, sinks, batch, seq)
    return _outproj_ln(o, w_o.astype(BF16), h, ln_g, ln_b, alpha)


def _mla_proj_kernel(x_ref, wd_ref, qn_ref, kvn_ref, wq_ref, wk_ref, wv_ref, tab_ref,
                     q_ref, k_ref, v_ref):
    half = B_ROPE // 2
    c = _dot(x_ref[...].astype(BF16), wd_ref[...])
    cq, ckv = c[:, :B_Q_RANK], c[:, B_Q_RANK:B_Q_RANK + B_KV_RANK]
    kr = _rope_block(c[:, B_Q_RANK + B_KV_RANK:], tab_ref, half)
    cqn = (cq * lax.rsqrt(jnp.mean(cq * cq, -1, keepdims=True) + RMS_EPS) * qn_ref[...]).astype(BF16)
    ckvn = (ckv * lax.rsqrt(jnp.mean(ckv * ckv, -1, keepdims=True) + RMS_EPS) * kvn_ref[...]).astype(BF16)
    q = _dot(cqn, wq_ref[...])
    kk = _dot(ckvn, wk_ref[...])
    for hd in range(B_HEADS):
        sl = slice(hd * LANES, (hd + 1) * LANES)
        q_ref[:, sl] = _rope_block(q[:, sl], tab_ref, half).astype(q_ref.dtype)
        k_ref[:, sl] = (kk[:, sl] + kr).astype(k_ref.dtype)
    v_ref[...] = _dot(ckvn, wv_ref[...]).astype(v_ref.dtype)


def _mla_attn_kernel(q_ref, k_ref, v_ref, o_ref, vt_ref, *state, tq, scale):
    qi = pl.program_id(2)

    @pl.when(qi == 0)
    def _():
        _transpose_chunks(v_ref, vt_ref, tq)

    q_t = [q_ref[:, hh * LANES:(hh + 1) * LANES].astype(F32).T.astype(BF16) for hh in range(2)]
    key = lax.broadcasted_iota(jnp.int32, (tq, tq), 0)
    qry = lax.broadcasted_iota(jnp.int32, (tq, tq), 1)
    causal = key <= qry

    def scores(hh, c):
        return _dot(k_ref[pl.ds(pl.multiple_of(c * tq, tq), tq), hh * LANES:(hh + 1) * LANES], q_t[hh])

    def values(hh, c):
        return vt_ref[c, hh * B_V:(hh + 1) * B_V, :]

    res = _flash_causal_t(scores, values, 2, qi, causal, state, scale)
    o_t = jnp.concatenate([acc / l for l, acc in res], axis=0)
    o_ref[...] = o_t.T.astype(o_ref.dtype)


def _mixer_b(h, w_down, q_norm, kv_norm, w_uq, w_ukv, w_o, ln_g, ln_b, alpha, batch, seq):
    t, d = h.shape
    dq = B_NOPE + B_ROPE
    pad = LANES - dq
    wd = jnp.concatenate([w_down[:, :B_Q_RANK + B_KV_RANK], jnp.zeros((d, B_NOPE), F32),
                          w_down[:, B_Q_RANK + B_KV_RANK:], jnp.zeros((d, pad), F32)], axis=1).astype(BF16)
    wq = jnp.pad(w_uq.reshape(B_Q_RANK, B_HEADS, dq), ((0, 0), (0, 0), (0, pad))
                 ).reshape(B_Q_RANK, B_HEADS * LANES).astype(BF16)
    wkv = w_ukv.reshape(B_KV_RANK, B_HEADS, B_NOPE + B_V)
    wk = jnp.pad(wkv[:, :, :B_NOPE], ((0, 0), (0, 0), (0, LANES - B_NOPE))
                 ).reshape(B_KV_RANK, B_HEADS * LANES).astype(BF16)
    wv = wkv[:, :, B_NOPE:].reshape(B_KV_RANK, B_HEADS * B_V).astype(BF16)
    tab = _rope_tables(seq, B_ROPE, LANES, B_NOPE)
    tm = PROJ_ROWS
    spb = seq // tm
    full = lambda a: pl.BlockSpec(a.shape, lambda i: (0,) * a.ndim)
    qn, kvn = q_norm.reshape(1, -1), kv_norm.reshape(1, -1)
    q, k, v = pl.pallas_call(
        _mla_proj_kernel,
        grid=(t // tm,),
        in_specs=[pl.BlockSpec((tm, d), lambda i: (i, 0)), full(wd), full(qn), full(kvn),
                  full(wq), full(wk), full(wv),
                  pl.BlockSpec((3, tm, LANES), lambda i: (0, i % spb, 0))],
        out_specs=[pl.BlockSpec((tm, B_HEADS * LANES), lambda i: (i, 0)),
                   pl.BlockSpec((tm, B_HEADS * LANES), lambda i: (i, 0)),
                   pl.BlockSpec((tm, B_HEADS * B_V), lambda i: (i, 0))],
        out_shape=[jax.ShapeDtypeStruct((t, B_HEADS * LANES), BF16),
                   jax.ShapeDtypeStruct((t, B_HEADS * LANES), BF16),
                   jax.ShapeDtypeStruct((t, B_HEADS * B_V), BF16)],
        compiler_params=_params("parallel"),
        name="mla_proj",
    )(h, wd, qn, kvn, wq, wk, wv, tab)

    tq = min(FLASH_ROWS, seq)
    nq = seq // tq
    o = pl.pallas_call(
        functools.partial(_mla_attn_kernel, tq=tq, scale=dq ** -0.5),
        grid=(batch, B_HEADS // 2, nq),
        in_specs=[pl.BlockSpec((tq, 2 * LANES), lambda b, p, i: (b * nq + i, p)),
                  pl.BlockSpec((seq, 2 * LANES), lambda b, p, i: (b, p)),
                  pl.BlockSpec((seq, LANES), lambda b, p, i: (b, p))],
        out_specs=pl.BlockSpec((tq, LANES), lambda b, p, i: (b * nq + i, p)),
        out_shape=jax.ShapeDtypeStruct((t, B_HEADS * B_V), BF16),
        scratch_shapes=[pltpu.VMEM((nq, LANES, tq), BF16)] + _flash_scratch(2, B_V, tq, tq),
        compiler_params=_params("parallel", "parallel", "arbitrary"),
        name="mla_attention",
    )(q, k, v)
    return _outproj_ln(o, w_o.astype(BF16), h, ln_g, ln_b, alpha)


def _gelu_tanh(x):
    return x * (0.5 * (1.0 + jnp.tanh(math.sqrt(2.0 / math.pi) * (x + 0.044715 * (x * x * x)))))


def _compress_kernel(x_ref, pe_ref, w1_ref, w2_ref, o_ref):
    n = x_ref.shape[0] // C_CMP_STRIDE
    a = b = None
    for l in range(C_CMP_STRIDE):
        y = x_ref[pl.ds(l, n, stride=C_CMP_STRIDE), :]
        ta = _dot((y + pe_ref[l:l + 1, :]).astype(BF16), w1_ref[l])
        tb = _dot((y + pe_ref[C_CMP_STRIDE + l:C_CMP_STRIDE + l + 1, :]).astype(BF16), w1_ref[C_CMP_STRIDE + l])
        a = ta if a is None else a + ta
        b = tb if b is None else b + tb
    hid = a + pltpu.roll(b, n - 1, 0)
    o_ref[0] = _dot(_gelu_tanh(hid).astype(BF16), w2_ref[...]).astype(o_ref.dtype)


def _compress(kvc, which, pe, w1, w2, batch, seq):
    n = seq // C_CMP_STRIDE
    hp = LANES // HEAD_DIM
    pairs = C_KV_HEADS // hp
    eye = jnp.eye(hp, dtype=F32)
    w1bd = jnp.einsum("lij,gh->lgihj", w1.reshape(C_CMP_LEN, HEAD_DIM, C_CMP_HIDDEN), eye)
    w1bd = w1bd.reshape(C_CMP_LEN, LANES, hp * C_CMP_HIDDEN).astype(BF16)
    w2bd = jnp.einsum("ij,gh->gihj", jnp.concatenate([w2, w2], axis=1), eye)
    w2bd = w2bd.reshape(hp * C_CMP_HIDDEN, hp * LANES).astype(BF16)
    return pl.pallas_call(
        _compress_kernel,
        grid=(batch, pairs),
        in_specs=[pl.BlockSpec((seq, LANES), lambda i, j: (i, which * pairs + j)),
                  pl.BlockSpec((C_CMP_LEN, LANES), lambda i, j: (0, 0)),
                  pl.BlockSpec(w1bd.shape, lambda i, j: (0, 0, 0)),
                  pl.BlockSpec(w2bd.shape, lambda i, j: (0, 0))],
        out_specs=pl.BlockSpec((1, n, hp * LANES), lambda i, j: (i, 0, j)),
        out_shape=jax.ShapeDtypeStruct((batch, n, C_KV_HEADS * LANES), BF16),
        compiler_params=_params("parallel", "parallel"),
        name="nsa_compress",
    )(kvc, jnp.tile(pe, (1, hp)), w1bd, w2bd)


def _nsa_attn_kernel(q_ref, ks_ref, vs_ref, kw_ref, vw_ref, kc_ref, vc_ref, ov_ref, ex_ref, gt_ref,
                     o_ref, vst_ref, vwt_ref, vct_ref, *state, tq, tk, seq):
    qi = pl.program_id(2)
    q0 = qi * tq
    group = C_HEADS // C_KV_HEADS
    n_sel_blocks = seq // C_SEL_LEN
    nc = kc_ref.shape[1]

    @pl.when(qi == 0)
    def _():
        _transpose_chunks(vs_ref, vst_ref, tk)
        _transpose_chunks(vw_ref, vwt_ref, tq)
        vct_ref[...] = vc_ref[0].astype(F32).T.astype(vct_ref.dtype)

    top = lax.broadcasted_iota(jnp.int32, (LANES, tq), 0) < HEAD_DIM
    qa_t = q_ref[:, :LANES].astype(F32).T
    qb_t = q_ref[:, LANES:].astype(F32).T
    qs_t = jnp.concatenate([jnp.where(top, qa_t, 0.0), jnp.where(top, 0.0, qa_t),
                            jnp.where(top, qb_t, 0.0), jnp.where(top, 0.0, qb_t)], axis=1).astype(BF16)
    rep = lambda a: jnp.concatenate([a] * group, axis=1)

    def masked_softmax(s, valid):
        s = jnp.where(valid, s, NEG)
        e = jnp.exp(s - jnp.maximum(jnp.max(s, 0, keepdims=True), MAX_FLOOR))
        return e / jnp.maximum(jnp.sum(e, 0, keepdims=True), 1e-30)

    t_c = q0 + lax.broadcasted_iota(jnp.int32, (nc, tq), 1)
    n_c = lax.broadcasted_iota(jnp.int32, (nc, tq), 0)
    c_valid = rep(n_c * C_CMP_STRIDE + (C_CMP_LEN - 1) <= t_c)
    p_cmp = masked_softmax(_dot(kc_ref[0], qs_t), c_valid).astype(BF16)
    o_cmp = _dot(vct_ref[...], p_cmp)
    imp4 = _dot(ov_ref[...], p_cmp)
    imp = imp4[:n_sel_blocks, :tq]
    for r in range(1, group):
        imp = imp + imp4[:n_sel_blocks, r * tq:(r + 1) * tq]

    blk = lax.broadcasted_iota(jnp.int32, (n_sel_blocks, tq), 0)
    cur = (q0 + lax.broadcasted_iota(jnp.int32, (n_sel_blocks, tq), 1)) // C_SEL_LEN
    imp = jnp.where((blk == 0) | (blk == cur) | (blk == cur - 1), jnp.inf, imp)
    imp = jnp.where(blk <= cur, imp, -jnp.inf)
    rank = jnp.zeros((n_sel_blocks, tq), F32)
    for i in range(n_sel_blocks):
        r_i = imp[i:i + 1, :]
        rank = rank + jnp.where(blk > i, jnp.where(r_i >= imp, 1.0, 0.0), jnp.where(r_i > imp, 1.0, 0.0))
    sel = jnp.where(rank < min(C_N_SEL, n_sel_blocks), 0.0, NEG)
    sel = jnp.concatenate([sel, jnp.zeros((LANES - n_sel_blocks, tq), F32)], axis=0)
    qs_aug = jnp.concatenate([qs_t, rep(sel).astype(BF16)], axis=0)

    span = C_WINDOW + tq
    start = pl.multiple_of(jnp.maximum(q0 - C_WINDOW, 0), tq)
    t_w = q0 + lax.broadcasted_iota(jnp.int32, (span, tq), 1)
    dist = t_w - (start + lax.broadcasted_iota(jnp.int32, (span, tq), 0))
    w_valid = rep((dist >= 0) & (dist < C_WINDOW))
    p_win = masked_softmax(_dot(kw_ref[pl.ds(start, span), :], qs_t), w_valid).astype(BF16)
    o_win = _dot(vwt_ref[start // tq], p_win[:tq])
    for cb in range(1, span // tq):
        o_win = o_win + _dot(vwt_ref[start // tq + cb], p_win[cb * tq:(cb + 1) * tq])

    t_k = q0 + lax.broadcasted_iota(jnp.int32, (tk, tq), 1)
    k_k = lax.broadcasted_iota(jnp.int32, (tk, tq), 0)

    def sel_scores(ch, c):
        base = pl.multiple_of(c * tk, tk)
        k_aug = jnp.concatenate([ks_ref[pl.ds(base, tk), :], ex_ref[c]], axis=1)
        return jnp.where(rep(base + k_k <= t_k), _dot(k_aug, qs_aug), NEG)

    (l, acc), = _flash_causal_t(sel_scores, lambda ch, c: vst_ref[c], 1, (q0 + tq - 1) // tk, None, state)
    o_slc = acc / jnp.maximum(l, 1e-30)

    g_t = gt_ref[...].T
    outs = []
    for r in range(group):
        cols = slice(r * tq, (r + 1) * tq)
        outs.append(o_cmp[:, cols] * g_t[3 * r:3 * r + 1] + o_slc[:, cols] * g_t[3 * r + 1:3 * r + 2]
                    + o_win[:, cols] * g_t[3 * r + 2:3 * r + 3])
    o_ref[:, :LANES] = jnp.where(top, outs[0], outs[1]).T.astype(o_ref.dtype)
    o_ref[:, LANES:] = jnp.where(top, outs[2], outs[3]).T.astype(o_ref.dtype)


def _mixer_c(h, w_in, pos_k, pos_v, wk1, wk2, wv1, wv2, w_o, ln_g, ln_b, alpha, batch, seq):
    t, d = h.shape
    qd, kd = C_HEADS * HEAD_DIM, C_KV_HEADS * HEAD_DIM
    group = C_HEADS // C_KV_HEADS
    kv = lambda i: w_in[:, qd + i * kd:qd + (i + 1) * kd]
    wg = jnp.pad(w_in[:, qd + 6 * kd:].reshape(d, C_KV_HEADS, group * 3),
                 ((0, 0), (0, 0), (0, LANES - group * 3))).reshape(d, C_KV_HEADS * LANES)
    w = jnp.concatenate([w_in[:, :qd], _dup_cols(kv(2), C_KV_HEADS), _dup_cols(kv(4), C_KV_HEADS),
                         _dup_cols(kv(3), C_KV_HEADS), _dup_cols(kv(5), C_KV_HEADS),
                         kv(0), kv(1), wg], axis=1).astype(BF16)
    tab = _rope_tables(seq, ROT_DIM, HEAD_DIM, 0)
    kw2 = 2 * kd
    segs = [(0, qd, "rope", HEAD_DIM ** -0.5), (qd, 2 * kw2, "rope", 1.0), (qd + 2 * kw2, 2 * kw2, "plain", 1.0),
            (qd + 4 * kw2, 2 * kd, "plain", 1.0), (qd + 4 * kw2 + 2 * kd, C_KV_HEADS * LANES, "sigmoid", 1.0)]
    q, ksw, vsw, kvc, gates = _proj(h, w, tab, segs, [BF16, BF16, BF16, F32, F32], seq, ROT_DIM // 2)
    k_cmp = _compress(kvc, 0, pos_k, wk1, wk2, batch, seq)
    v_cmp = _compress(kvc, 1, pos_v, wv1, wv2, batch, seq)

    nc = seq // C_CMP_STRIDE
    nsb = seq // C_SEL_LEN
    tq = 128
    tk = min(512, seq)
    nq = seq // tq
    cs = np.arange(nc)[None, :] * C_CMP_STRIDE
    ss = np.arange(LANES)[:, None] * C_SEL_LEN
    overlap = ((cs <= ss + C_SEL_LEN - 1) & (ss <= cs + C_CMP_LEN - 1) & (np.arange(LANES)[:, None] < nsb))
    overlap = jnp.asarray(overlap, BF16)
    key_blk = (np.arange(seq) // C_SEL_LEN).reshape(seq // tk, tk, 1)
    expand = jnp.asarray(key_blk == np.arange(LANES)[None, None, :], BF16)

    per_bg = lambda width: pl.BlockSpec((seq, LANES), lambda b, g, i, width=width: (b, width + g))
    o = pl.pallas_call(
        functools.partial(_nsa_attn_kernel, tq=tq, tk=tk, seq=seq),
        grid=(batch, C_KV_HEADS, nq),
        in_specs=[pl.BlockSpec((tq, 2 * LANES), lambda b, g, i: (b * nq + i, g)),
                  per_bg(0), per_bg(0), per_bg(C_KV_HEADS), per_bg(C_KV_HEADS),
                  pl.BlockSpec((1, nc, LANES), lambda b, g, i: (b, 0, g)),
                  pl.BlockSpec((1, nc, LANES), lambda b, g, i: (b, 0, g)),
                  pl.BlockSpec((LANES, nc), lambda b, g, i: (0, 0)),
                  pl.BlockSpec((seq // tk, tk, LANES), lambda b, g, i: (0, 0, 0)),
                  pl.BlockSpec((tq, LANES), lambda b, g, i: (b * nq + i, g))],
        out_specs=pl.BlockSpec((tq, 2 * LANES), lambda b, g, i: (b * nq + i, g)),
        out_shape=jax.ShapeDtypeStruct((t, qd), BF16),
        scratch_shapes=[pltpu.VMEM((seq // tk, LANES, tk), BF16), pltpu.VMEM((seq // tq, LANES, tq), BF16),
                        pltpu.VMEM((LANES, nc), BF16)] + _flash_scratch(1, LANES, group * tq, tk),
        compiler_params=_params("parallel", "parallel", "arbitrary"),
        name="nsa_attention",
    )(q, ksw, vsw, ksw, vsw, k_cmp, v_cmp, overlap, expand, gates)
    return _outproj_ln(o, w_o.astype(BF16), h, ln_g, ln_b, alpha)


def _diff_attn_kernel(q_ref, k_ref, v_ref, lam_ref, sub_ref, o_ref, vt_ref, *state, tq, lam_init):
    qi = pl.program_id(2)

    n_heads = q_ref.shape[1] // LANES

    @pl.when(qi == 0)
    def _():
        for hd in range(n_heads):
            _transpose_chunks(v_ref.at[:, hd * LANES:(hd + 1) * LANES], vt_ref.at[hd], tq)

    lam = (jnp.exp(jnp.sum(lam_ref[0:1, :] * lam_ref[1:2, :], -1, keepdims=True))
           - jnp.exp(jnp.sum(lam_ref[2:3, :] * lam_ref[3:4, :], -1, keepdims=True)) + lam_init)
    top = lax.broadcasted_iota(jnp.int32, (LANES, tq), 0) < D_SUB
    qs_t = []
    for hd in range(n_heads):
        q_t = q_ref[:, hd * LANES:(hd + 1) * LANES].astype(F32).T
        qs_t.append(jnp.concatenate([jnp.where(top, q_t, 0.0), jnp.where(top, 0.0, q_t)], axis=1).astype(BF16))
    key = lax.broadcasted_iota(jnp.int32, (tq, tq), 0)
    qry = lax.broadcasted_iota(jnp.int32, (tq, tq), 1)
    causal = jnp.concatenate([key <= qry] * 2, axis=1)

    def scores(hd, c):
        return _dot(k_ref[pl.ds(pl.multiple_of(c * tq, tq), tq), hd * LANES:(hd + 1) * LANES], qs_t[hd])

    def values(hd, c):
        return vt_ref[hd, c]

    res = _flash_causal_t(scores, values, n_heads, qi, causal, state)
    for hd in range(n_heads):
        l, acc = res[hd]
        o = (acc / l).T
        o = o[:tq] - lam * o[tq:]
        o = o * lax.rsqrt(jnp.mean(o * o, -1, keepdims=True) + RMS_EPS) * sub_ref[...]
        o_ref[:, hd * LANES:(hd + 1) * LANES] = (o * (1.0 - lam_init)).astype(o_ref.dtype)


def _mixer_d(h, w_in, lq1, lk1, lq2, lk2, subln, w_o, ln_g, ln_b, alpha, layer_idx, batch, seq):
    t, d = h.shape
    qd = D_HEADS * 2 * D_SUB
    tab = _rope_tables(seq, ROT_DIM, D_SUB, 0)
    segs = [(0, qd, "rope", D_SUB ** -0.5), (qd, qd, "rope", 1.0), (2 * qd, qd, "plain", 1.0)]
    q, k, v = _proj(h, w_in.astype(BF16), tab, segs, [BF16, BF16, BF16], seq, ROT_DIM // 2)
    lam_init = 0.8 - 0.6 * math.exp(-0.3 * layer_idx)
    lam_in = jnp.stack([lq1, lk1, lq2, lk2]).astype(F32)
    tq = min(FLASH_ROWS, seq)
    nq = seq // tq
    hps = 2
    o = pl.pallas_call(
        functools.partial(_diff_attn_kernel, tq=tq, lam_init=lam_init),
        grid=(batch, D_HEADS // hps, nq),
        in_specs=[pl.BlockSpec((tq, hps * LANES), lambda b, hd, i: (b * nq + i, hd)),
                  pl.BlockSpec((seq, hps * LANES), lambda b, hd, i: (b, hd)),
                  pl.BlockSpec((seq, hps * LANES), lambda b, hd, i: (b, hd)),
                  pl.BlockSpec((4, D_SUB), lambda b, hd, i: (0, 0)),
                  pl.BlockSpec((1, 2 * D_SUB), lambda b, hd, i: (0, 0))],
        out_specs=pl.BlockSpec((tq, hps * LANES), lambda b, hd, i: (b * nq + i, hd)),
        out_shape=jax.ShapeDtypeStruct((t, qd), BF16),
        scratch_shapes=[pltpu.VMEM((hps, nq, LANES, tq), BF16)] + _flash_scratch(hps, LANES, 2 * tq, tq),
        compiler_params=_params("parallel", "parallel", "arbitrary"),
        name="diff_attention",
    )(q, k, v, lam_in, subln.reshape(1, -1))
    return _outproj_ln(o, w_o.astype(BF16), h, ln_g, ln_b, alpha)


def _router_kernel(x_ref, w_ref, o_ref):
    xb = x_ref[...].astype(BF16)
    logits = _dot(xb, w_ref[...])
    lg = logits[:, :LANES].T[:8]
    le = logits[:, LANES:].T[:M_EXPERTS]
    far = 4 * LANES

    def softmax(x, valid):
        m = jnp.max(jnp.where(valid, x, NEG), 0, keepdims=True)
        e = jnp.where(valid, jnp.exp(x - m), 0.0)
        return e / jnp.sum(e, 0, keepdims=True)

    def first_max(p, valid, row):
        top = jnp.max(jnp.where(valid, p, -1.0), 0, keepdims=True)
        idx = jnp.min(jnp.where(valid & (p == top), row, far), 0, keepdims=True)
        return top, idx

    g_row = lax.broadcasted_iota(jnp.int32, lg.shape, 0)
    e_row = lax.broadcasted_iota(jnp.int32, le.shape, 0)
    g_valid = g_row < M_GROUPS
    g_w, g_idx = first_max(softmax(lg, g_valid), g_valid, g_row)
    e_valid = (e_row >= g_idx * M_PER_GROUP) & (e_row < (g_idx + 1) * M_PER_GROUP)
    pe = softmax(le, e_valid)
    w0, i0 = first_max(pe, e_valid, e_row)
    w1, i1 = first_max(pe, e_valid & (e_row != i0), e_row)
    tot = w0 + w1
    out = jnp.where(g_row == 0, i0.astype(F32), 0.0)
    out = jnp.where(g_row == 1, i1.astype(F32), out)
    out = jnp.where(g_row == 2, g_w * w0 / tot, out)
    out = jnp.where(g_row == 3, g_w * w1 / tot, out)
    o_ref[...] = out


def _row_gather(src_hbm, idx_ref, buf, sem):
    for r in range(buf.shape[0]):
        pltpu.make_async_copy(src_hbm.at[pl.ds(idx_ref[0, 0, r], 1), :], buf.at[pl.ds(r, 1), :], sem).start()


def _row_gather_wait(src_hbm, buf, sem):
    pltpu.make_async_copy(src_hbm.at[pl.ds(0, buf.shape[0]), :], buf, sem).wait()


def _row_scatter(buf, idx_ref, dst_hbm, sem):
    for r in range(buf.shape[0]):
        pltpu.make_async_copy(buf.at[pl.ds(r, 1), :], dst_hbm.at[pl.ds(idx_ref[0, 0, r], 1), :], sem).start()


def _row_scatter_wait(buf, dst_hbm, sem):
    pltpu.make_async_copy(buf, dst_hbm.at[pl.ds(0, buf.shape[0]), :], sem).wait()


def _moe_kernel(ea_ref, eb_ref, used_ref, src0_ref, src_ref, dstp_ref, wt_ref, h_hbm,
                wga_ref, wua_ref, wda_ref, wgb_ref, wub_ref, wdb_ref, out_hbm,
                xbuf0, xbuf1, ybuf0, ybuf1, ga, ua, da, gb, ub, db, gsem, ssem, *, tm):
    t = pl.program_id(0)
    used = used_ref[0]
    xbuf = (xbuf0, xbuf1)
    ybuf = (ybuf0, ybuf1)
    prev = jnp.maximum(t - 1, 0)

    @pl.when(t == 0)
    def _():
        _row_gather(h_hbm, src0_ref, xbuf0, gsem.at[0])
        ybuf1[...] = jnp.zeros(ybuf1.shape, F32)
        fill = pltpu.make_async_copy(ybuf1, out_hbm.at[pl.ds(out_hbm.shape[0] - 2 * tm, tm), :], ssem.at[0])
        fill.start()
        fill.wait()

    @pl.when((t < used) & ((t == 0) | (ea_ref[t] != ea_ref[prev])))
    def _():
        ga[...] = wga_ref[0, 0].astype(BF16)
        ua[...] = wua_ref[0, 0].astype(BF16)
        da[...] = wda_ref[0, 0].astype(BF16)

    @pl.when((t < used) & ((t == 0) | (eb_ref[t] != eb_ref[prev])))
    def _():
        gb[...] = wgb_ref[0, 0].astype(BF16)
        ub[...] = wub_ref[0, 0].astype(BF16)
        db[...] = wdb_ref[0, 0].astype(BF16)

    def expert(xb, g_ref, u_ref, d_ref):
        gate = _dot(xb, g_ref[...])
        up = _dot(xb, u_ref[...])
        hid = (gate * (1.0 / (1.0 + jnp.exp(-gate))) * up).astype(BF16)
        return _dot(hid, d_ref[...])

    for s in range(2):
        @pl.when((t < used) & (t % 2 == s))
        def _(s=s):
            _row_gather_wait(h_hbm, xbuf[s], gsem.at[s])

            @pl.when(t >= 1)
            def _():
                _row_scatter_wait(ybuf[s], out_hbm, ssem.at[s])

            _row_scatter(ybuf[1 - s], dstp_ref, out_hbm, ssem.at[1 - s])
            _row_gather(h_hbm, src_ref, xbuf[1 - s], gsem.at[1 - s])
            xb = xbuf[s][...].astype(BF16)
            w = wt_ref[...]
            ybuf[s][...] = w[:, 0:1] * expert(xb, ga, ua, da) + w[:, 1:2] * expert(xb, gb, ub, db)

        @pl.when((t == used) & (t % 2 == s))
        def _(s=s):
            _row_scatter(ybuf[1 - s], dstp_ref, out_hbm, ssem.at[1 - s])
            _row_scatter_wait(ybuf[s], out_hbm, ssem.at[s])
            _row_scatter_wait(ybuf[1 - s], out_hbm, ssem.at[1 - s])
            _row_gather_wait(h_hbm, xbuf[s], gsem.at[s])


def _add_ln_kernel(h_ref, y_ref, g_ref, b_ref, o_ref, *, alpha):
    o_ref[...] = _layer_norm(alpha * h_ref[...] + y_ref[...], g_ref[...], b_ref[...])


def _hier_moe_ln(h, w_group, w_expert, w_gate, w_up, w_down, layer, ln_g, ln_b, alpha):
    t, d = h.shape
    tm = MOE_ROWS
    wr = jnp.concatenate([jnp.pad(w_group, ((0, 0), (0, LANES - M_GROUPS))),
                          jnp.pad(w_expert, ((0, 0), (0, LANES - M_EXPERTS)))], axis=1).astype(BF16)
    routed = pl.pallas_call(
        _router_kernel,
        grid=(t // LN_ROWS,),
        in_specs=[pl.BlockSpec((LN_ROWS, d), lambda i: (i, 0)),
                  pl.BlockSpec((d, 2 * LANES), lambda i: (0, 0))],
        out_specs=pl.BlockSpec((8, LN_ROWS), lambda i: (0, i)),
        out_shape=jax.ShapeDtypeStruct((8, t), F32),
        compiler_params=_params("parallel"),
        name="moe_router",
    )(h, wr)

    i0, i1 = routed[0].astype(jnp.int32), routed[1].astype(jnp.int32)
    ea, eb = jnp.minimum(i0, i1), jnp.maximum(i0, i1)
    w_ab = jnp.where((i0 < i1)[:, None], routed[2:4].T, routed[3:1:-1].T)
    n_cls = M_EXPERTS * M_PER_GROUP
    pairs = M_GROUPS * (M_PER_GROUP * (M_PER_GROUP - 1) // 2)
    n_tiles = t // tm + pairs
    cls = ea * M_PER_GROUP + eb % M_PER_GROUP
    order = jnp.argsort(cls).astype(jnp.int32)
    sizes = jnp.sum(cls[:, None] == jnp.arange(n_cls, dtype=jnp.int32)[None, :], axis=0, dtype=jnp.int32)
    tiles = (sizes + tm - 1) // tm
    tile_end = jnp.cumsum(tiles)
    seg_start = jnp.cumsum(sizes) - sizes
    used = tile_end[-1]
    tile_ids = jnp.arange(n_tiles, dtype=jnp.int32)
    tile_cls = jnp.sum(jnp.minimum(tile_ids, used - 1)[:, None] >= tile_end[None, :], axis=1).astype(jnp.int32)
    tile_ea = tile_cls // M_PER_GROUP
    tile_eb = tile_ea // M_PER_GROUP * M_PER_GROUP + tile_cls % M_PER_GROUP
    tile_first = (tile_end - tiles)[tile_cls]
    lane_r = jnp.arange(tm, dtype=jnp.int32)[None, :]
    offs = (tile_ids - tile_first)[:, None] * tm + lane_r
    valid = (offs < sizes[tile_cls][:, None]) & (tile_ids < used)[:, None]
    tok = order[jnp.clip(seg_start[tile_cls][:, None] + offs, 0, t - 1)]
    src = jnp.where(valid, tok, 0).reshape(n_tiles, 1, tm)
    wrow = jnp.where(valid[:, :, None], w_ab[tok], 0.0).reshape(n_tiles * tm, M_TOPK)
    trash = t + (tile_ids % 2)[:, None] * tm + lane_r
    dst = jnp.where(valid, tok, trash)
    dstp = jnp.concatenate([t + tm + lane_r, dst], axis=0).reshape(n_tiles + 1, 1, tm)

    wspec = lambda shape, which: pl.BlockSpec(
        (1, 1) + shape, lambda i, ea, eb, used: (layer, (ea, eb)[which][i], 0, 0))
    idx_spec = pl.BlockSpec((1, 1, tm), lambda i, ea, eb, used: (i, 0, 0), memory_space=pltpu.SMEM)
    nxt_spec = pl.BlockSpec((1, 1, tm), lambda i, ea, eb, used: (jnp.minimum(i + 1, n_tiles - 1), 0, 0),
                            memory_space=pltpu.SMEM)
    wshapes = [(d, M_HIDDEN), (d, M_HIDDEN), (M_HIDDEN, d)]
    y = pl.pallas_call(
        functools.partial(_moe_kernel, tm=tm),
        grid_spec=pltpu.PrefetchScalarGridSpec(
            num_scalar_prefetch=3,
            grid=(n_tiles,),
            in_specs=[idx_spec, nxt_spec, idx_spec,
                      pl.BlockSpec((tm, M_TOPK), lambda i, ea, eb, used: (i, 0)),
                      pl.BlockSpec(memory_space=pl.ANY)]
                     + [wspec(s, 0) for s in wshapes] + [wspec(s, 1) for s in wshapes],
            out_specs=pl.BlockSpec(memory_space=pl.ANY),
            scratch_shapes=[pltpu.VMEM((tm, d), F32)] * 4 + [pltpu.VMEM(s, BF16) for s in wshapes] * 2
                           + [pltpu.SemaphoreType.DMA((2,)), pltpu.SemaphoreType.DMA((2,))]),
        out_shape=jax.ShapeDtypeStruct((t + 2 * tm, d), F32),
        compiler_params=_params("arbitrary"),
        name="moe_experts",
    )(tile_ea, tile_eb, used.reshape(1).astype(jnp.int32), src, src, dstp, wrow, h,
      w_gate, w_up, w_down, w_gate, w_up, w_down)

    tc = LN_ROWS
    return pl.pallas_call(
        functools.partial(_add_ln_kernel, alpha=alpha),
        grid=(t // tc,),
        in_specs=[pl.BlockSpec((tc, d), lambda i: (i, 0)),
                  pl.BlockSpec((tc, d), lambda i: (i, 0)),
                  pl.BlockSpec((1, d), lambda i: (0, 0)),
                  pl.BlockSpec((1, d), lambda i: (0, 0))],
        out_specs=pl.BlockSpec((tc, d), lambda i: (i, 0)),
        out_shape=jax.ShapeDtypeStruct((t, d), F32),
        compiler_params=_params("parallel"),
        name="moe_add_ln",
    )(h, y, ln_g.reshape(1, d), ln_b.reshape(1, d))


def kernel(x, a_w_in, a_sinks, a_w_o, b_w_down, b_q_norm, b_kv_norm, b_w_uq, b_w_ukv, b_w_o, c_w_in, c_pos_k, c_pos_v, c_wk1, c_wk2, c_wv1, c_wv2, c_w_o, d_w_in, d_lq1, d_lk1, d_lq2, d_lk2, d_subln, d_w_o, moe_w_group, moe_w_expert, moe_w_gate, moe_w_up, moe_w_down, ln_g, ln_b):
    batch, seq, d = x.shape
    depth = ln_g.shape[0]
    alpha = (2 * depth) ** 0.25
    h = x.reshape(batch * seq, d)
    for i in range(depth):
        kind, j = i % N_MIXERS, i // N_MIXERS
        g, b = ln_g[i, 0], ln_b[i, 0]
        if kind == 0:
            h = _mixer_a(h, a_w_in[j], a_sinks[j], a_w_o[j], g, b, alpha, batch, seq)
        elif kind == 1:
            h = _mixer_b(h, b_w_down[j], b_q_norm[j], b_kv_norm[j], b_w_uq[j], b_w_ukv[j], b_w_o[j],
                         g, b, alpha, batch, seq)
        elif kind == 2:
            h = _mixer_c(h, c_w_in[j], c_pos_k[j], c_pos_v[j], c_wk1[j], c_wk2[j], c_wv1[j], c_wv2[j],
                         c_w_o[j], g, b, alpha, batch, seq)
        else:
            h = _mixer_d(h, d_w_in[j], d_lq1[j], d_lk1[j], d_lq2[j], d_lk2[j], d_subln[j], d_w_o[j],
                         g, b, alpha, i, batch, seq)
        h = _hier_moe_ln(h, moe_w_group[i], moe_w_expert[i], moe_w_gate, moe_w_up, moe_w_down, i,
                         ln_g[i, 1], ln_b[i, 1], alpha)
    return h.reshape(batch, seq, d)
```

```python
import functools
import math

import numpy as np
import jax
import jax.numpy as jnp
from jax import lax
from jax.experimental import pallas as pl
from jax.experimental.pallas import tpu as pltpu

F32 = jnp.float32
BF16 = jnp.bfloat16

HEAD_DIM = 64
ROPE_THETA = 500000.0
ROT_DIM = HEAD_DIM // 4
A_HEADS, A_KV_HEADS, A_WINDOW = 16, 4, 128
B_HEADS, B_Q_RANK, B_KV_RANK, B_NOPE, B_ROPE, B_V = 16, 384, 256, 64, 32, 64
C_HEADS, C_KV_HEADS = 16, 4
C_CMP_LEN, C_CMP_STRIDE, C_CMP_HIDDEN = 32, 16, 128
C_SEL_LEN, C_N_SEL, C_WINDOW = 64, 16, 512
D_HEADS, D_SUB = 8, 64
M_GROUPS, M_PER_GROUP, M_TOPK, M_HIDDEN = 4, 8, 2, 512
M_EXPERTS = M_GROUPS * M_PER_GROUP
N_MIXERS = 4
LN_EPS = 1e-5
RMS_EPS = 1e-6

LANES = 128
NEG = -1e30
MAX_FLOOR = -1e20
VMEM_LIMIT = 48 * 1024 * 1024

PROJ_ROWS = 256
LN_ROWS = 512
MOE_ROWS = 256
FLASH_ROWS = 512


def _params(*sem):
    return pltpu.CompilerParams(dimension_semantics=sem, vmem_limit_bytes=VMEM_LIMIT)


def _dot(a, b):
    return jnp.dot(a, b, preferred_element_type=F32)


def _dot_t(a, b):
    return lax.dot_general(a, b, (((1,), (1,)), ((), ())), preferred_element_type=F32)


def _rope_tables(seq, rot_dim, period, off):
    half = rot_dim // 2
    inv_freq = 1.0 / (ROPE_THETA ** (jnp.arange(half, dtype=F32) * (2.0 / rot_dim)))
    ang = jnp.arange(seq, dtype=F32)[:, None] * inv_freq[None, :]
    cos, sin = jnp.cos(ang), jnp.sin(ang)
    lane = np.arange(LANES) % period - off
    first = (lane >= 0) & (lane < half)
    second = (lane >= half) & (lane < rot_dim)
    idx = np.where(first, lane, np.where(second, lane - half, 0))
    cg, sg = cos[:, idx], sin[:, idx]
    c = jnp.where(first | second, cg, 1.0)
    sa = jnp.where(first, -sg, 0.0)
    sb = jnp.where(second, sg, 0.0)
    return jnp.stack([c, sa, sb])


def _rope_block(x, tab_ref, half):
    return (x * tab_ref[0] + pltpu.roll(x, LANES - half, 1) * tab_ref[1]
            + pltpu.roll(x, half, 1) * tab_ref[2])


def _proj_kernel(x_ref, w_ref, tab_ref, *out_refs, segs, half):
    xb = x_ref[...].astype(BF16)
    for (start, width, kind, scale), o_ref in zip(segs, out_refs):
        acc = _dot(xb, w_ref[:, start:start + width])
        if kind == "rope":
            for c in range(width // LANES):
                y = _rope_block(acc[:, c * LANES:(c + 1) * LANES], tab_ref, half)
                if scale != 1.0:
                    y = y * scale
                o_ref[:, c * LANES:(c + 1) * LANES] = y.astype(o_ref.dtype)
        elif kind == "sigmoid":
            o_ref[...] = (1.0 / (1.0 + jnp.exp(-acc))).astype(o_ref.dtype)
        else:
            o_ref[...] = acc.astype(o_ref.dtype)


def _proj(x, w, tab, segs, dtypes, seq, half):
    t, k = x.shape
    n = w.shape[1]
    tm = PROJ_ROWS
    spb = seq // tm
    out_shape = [jax.ShapeDtypeStruct((t, s[1]), d) for s, d in zip(segs, dtypes)]
    return pl.pallas_call(
        functools.partial(_proj_kernel, segs=tuple(segs), half=half),
        grid=(t // tm,),
        in_specs=[pl.BlockSpec((tm, k), lambda i: (i, 0)),
                  pl.BlockSpec((k, n), lambda i: (0, 0)),
                  pl.BlockSpec((3, tm, LANES), lambda i: (0, i % spb, 0))],
        out_specs=[pl.BlockSpec((tm, s[1]), lambda i: (i, 0)) for s in segs],
        out_shape=out_shape,
        compiler_params=_params("parallel"),
        name="proj",
    )(x, w, tab)


def _layer_norm(z, g, b):
    mu = jnp.mean(z, -1, keepdims=True)
    zc = z - mu
    var = jnp.mean(zc * zc, -1, keepdims=True)
    return zc * lax.rsqrt(var + LN_EPS) * g + b


def _outproj_ln_kernel(o_ref, w_ref, h_ref, g_ref, b_ref, out_ref, *, alpha):
    y = _dot(o_ref[...], w_ref[...])
    out_ref[...] = _layer_norm(alpha * h_ref[...] + y, g_ref[...], b_ref[...])


def _outproj_ln(o, w, h, g, b, alpha):
    t, k = o.shape
    d = w.shape[1]
    tm = LN_ROWS
    return pl.pallas_call(
        functools.partial(_outproj_ln_kernel, alpha=alpha),
        grid=(t // tm,),
        in_specs=[pl.BlockSpec((tm, k), lambda i: (i, 0)),
                  pl.BlockSpec((k, d), lambda i: (0, 0)),
                  pl.BlockSpec((tm, d), lambda i: (i, 0)),
                  pl.BlockSpec((1, d), lambda i: (0, 0)),
                  pl.BlockSpec((1, d), lambda i: (0, 0))],
        out_specs=pl.BlockSpec((tm, d), lambda i: (i, 0)),
        out_shape=jax.ShapeDtypeStruct((t, d), F32),
        compiler_params=_params("parallel"),
        name="outproj_ln",
    )(o, w, h, g.reshape(1, d), b.reshape(1, d))


def _stack_group_queries(q_ref, rows):
    lo = lax.broadcasted_iota(jnp.int32, (rows, LANES), 1) < HEAD_DIM
    qa, qb = q_ref[:, :LANES], q_ref[:, LANES:]
    z = jnp.zeros_like(qa)
    return jnp.concatenate([jnp.where(lo, qa, z), jnp.where(lo, z, qa),
                            jnp.where(lo, qb, z), jnp.where(lo, z, qb)], axis=0)


def _pair(lo_val, hi_val):
    lo = lax.broadcasted_iota(jnp.int32, lo_val.shape, 1) < HEAD_DIM
    return jnp.where(lo, lo_val, hi_val)


def _transpose_chunks(src_ref, dst_ref, chunk):
    for c in range(src_ref.shape[0] // chunk):
        dst_ref[c] = src_ref[c * chunk:(c + 1) * chunk, :].astype(F32).T.astype(dst_ref.dtype)


def _softmax_update_t(s, valid, m, l, scale=None):
    if valid is not None:
        s = jnp.where(valid, s, NEG)
    m_new = jnp.maximum(m, jnp.max(s, 0, keepdims=True))
    if scale is None:
        a = jnp.exp(m - m_new)
        p = jnp.exp(s - m_new)
    else:
        a = jnp.exp((m - m_new) * scale)
        p = jnp.exp((s - m_new) * scale)
    return m_new, a * l + jnp.sum(p, 0, keepdims=True), a, p.astype(BF16)


def _flash_scratch(n_chains, dv, m_cols, tk):
    return [pltpu.VMEM((n_chains, 3, 1, m_cols), F32), pltpu.VMEM((n_chains, dv, m_cols), F32),
            pltpu.VMEM((n_chains, 2, tk, m_cols), F32), pltpu.VMEM((n_chains, 2, tk, m_cols), BF16)]


def _flash_causal_t(score_fn, value_fn, n_chains, qi, causal, state, scale=None):
    st_ref, acc_ref, s_ref, p_ref = state
    for ch in range(n_chains):
        st_ref[ch, 0] = jnp.full(st_ref.shape[2:], MAX_FLOOR, F32)
        st_ref[ch, 1] = jnp.zeros(st_ref.shape[2:], F32)
        st_ref[ch, 2] = jnp.ones(st_ref.shape[2:], F32)
        acc_ref[ch] = jnp.zeros(acc_ref.shape[1:], F32)
        p_ref[ch, 1] = jnp.zeros(p_ref.shape[2:], BF16)
        s_ref[ch, 0] = score_fn(ch, 0)

    def half(c, cur, valid, last):
        nxt = 1 - cur
        for ch in range(n_chains):
            m, l, a, p = _softmax_update_t(s_ref[ch, cur], valid, st_ref[ch, 0], st_ref[ch, 1], scale)
            if not last:
                s_ref[ch, nxt] = score_fn(ch, c + 1)
            acc = st_ref[ch, 2] * acc_ref[ch] + _dot(value_fn(ch, jnp.maximum(c - 1, 0)), p_ref[ch, nxt])
            if last:
                acc = a * acc + _dot(value_fn(ch, c), p)
            else:
                p_ref[ch, cur] = p
                st_ref[ch, 2] = a
            acc_ref[ch] = acc
            st_ref[ch, 0] = m
            st_ref[ch, 1] = l

    def pair(j, carry):
        half(2 * j, 0, None, False)
        half(2 * j + 1, 1, None, False)
        return carry

    lax.fori_loop(0, qi // 2, pair, 0)

    @pl.when(qi % 2 == 1)
    def _():
        half(qi - 1, 0, None, False)
        half(qi, 1, causal, True)

    @pl.when(qi % 2 == 0)
    def _():
        half(qi, 0, causal, True)

    return [(st_ref[ch, 1], acc_ref[ch]) for ch in range(n_chains)]


def _swa_kernel(sink_ref, q_ref, kp_ref, kc_ref, vp_ref, vc_ref, o_ref):
    n = pl.program_id(1)
    blk = A_WINDOW
    i = lax.broadcasted_iota(jnp.int32, (blk, 2 * blk), 0)
    j = lax.broadcasted_iota(jnp.int32, (blk, 2 * blk), 1)
    dist = blk + i - j
    valid = (dist >= 0) & (dist < A_WINDOW) & ((n - 1) * blk + j >= 0)
    group = A_HEADS // A_KV_HEADS
    for g in range(A_KV_HEADS):
        qs = _stack_group_queries(q_ref.at[:, g * 2 * LANES:(g + 1) * 2 * LANES], blk)
        k = jnp.concatenate([kp_ref[:, g * LANES:(g + 1) * LANES],
                             kc_ref[:, g * LANES:(g + 1) * LANES]], axis=0)
        v = jnp.concatenate([vp_ref[:, g * LANES:(g + 1) * LANES],
                             vc_ref[:, g * LANES:(g + 1) * LANES]], axis=0)
        s = _dot_t(qs, k)
        ps = []
        for r in range(group):
            sink = sink_ref[g * group + r]
            sr = jnp.where(valid, s[r * blk:(r + 1) * blk], NEG)
            m = jnp.maximum(jnp.max(sr, -1, keepdims=True), sink)
            e = jnp.where(valid, jnp.exp(sr - m), 0.0)
            p = e / (jnp.sum(e, -1, keepdims=True) + jnp.exp(sink - m))
            ps.append(p.astype(BF16))
        o = _dot(jnp.concatenate(ps, axis=0), v)
        o_ref[:, g * 2 * LANES:g * 2 * LANES + LANES] = _pair(o[:blk], o[blk:2 * blk]).astype(o_ref.dtype)
        o_ref[:, g * 2 * LANES + LANES:(g + 1) * 2 * LANES] = _pair(
            o[2 * blk:3 * blk], o[3 * blk:]).astype(o_ref.dtype)


def _swa_attention(q, kd, vd, sinks, batch, seq):
    t = q.shape[0]
    blk = A_WINDOW
    nb = seq // blk
    qd = A_HEADS * HEAD_DIM
    kw = A_KV_HEADS * LANES
    cur = lambda b, n, s: (b * nb + n, 0)
    prev = lambda b, n, s: (b * nb + jnp.maximum(n - 1, 0), 0)
    return pl.pallas_call(
        _swa_kernel,
        grid_spec=pltpu.PrefetchScalarGridSpec(
            num_scalar_prefetch=1,
            grid=(batch, nb),
            in_specs=[pl.BlockSpec((blk, qd), cur),
                      pl.BlockSpec((blk, kw), prev), pl.BlockSpec((blk, kw), cur),
                      pl.BlockSpec((blk, kw), prev), pl.BlockSpec((blk, kw), cur)],
            out_specs=pl.BlockSpec((blk, qd), cur)),
        out_shape=jax.ShapeDtypeStruct((t, qd), BF16),
        compiler_params=_params("parallel", "parallel"),
        name="swa_attention",
    )(sinks.astype(F32), q, kd, kd, vd, vd)


def _dup_cols(w, heads):
    k = w.shape[0]
    w4 = w.reshape(k, heads, 1, HEAD_DIM)
    return jnp.broadcast_to(w4, (k, heads, 2, HEAD_DIM)).reshape(k, heads * 2 * HEAD_DIM)


def _mixer_a(h, w_in, sinks, w_o, ln_g, ln_b, alpha, batch, seq):
    qd, kd = A_HEADS * HEAD_DIM, A_KV_HEADS * HEAD_DIM
    w = jnp.concatenate([w_in[:, :qd], _dup_cols(w_in[:, qd:qd + kd], A_KV_HEADS),
                         _dup_cols(w_in[:, qd + kd:], A_KV_HEADS)], axis=1).astype(BF16)
    tab = _rope_tables(seq, ROT_DIM, HEAD_DIM, 0)
    segs = [(0, qd, "rope", HEAD_DIM ** -0.5), (qd, 2 * kd, "rope", 1.0), (qd + 2 * kd, 2 * kd, "plain", 1.0)]
    q, k2, v2 = _proj(h, w, tab, segs, [BF16, BF16, BF16], seq, ROT_DIM // 2)
    o = _swa_attention(q, k2, v2, sinks, batch, seq)
    return _outproj_ln(o, w_o.astype(BF16), h, ln_g, ln_b, alpha)


def _mla_proj_kernel(x_ref, wd_ref, qn_ref, kvn_ref, wq_ref, wk_ref, wv_ref, tab_ref,
                     q_ref, k_ref, v_ref):
    half = B_ROPE // 2
    c = _dot(x_ref[...].astype(BF16), wd_ref[...])
    cq, ckv = c[:, :B_Q_RANK], c[:, B_Q_RANK:B_Q_RANK + B_KV_RANK]
    kr = _rope_block(c[:, B_Q_RANK + B_KV_RANK:], tab_ref, half)
    cqn = (cq * lax.rsqrt(jnp.mean(cq * cq, -1, keepdims=True) + RMS_EPS) * qn_ref[...]).astype(BF16)
    ckvn = (ckv * lax.rsqrt(jnp.mean(ckv * ckv, -1, keepdims=True) + RMS_EPS) * kvn_ref[...]).astype(BF16)
    q = _dot(cqn, wq_ref[...])
    kk = _dot(ckvn, wk_ref[...])
    for hd in range(B_HEADS):
        sl = slice(hd * LANES, (hd + 1) * LANES)
        q_ref[:, sl] = _rope_block(q[:, sl], tab_ref, half).astype(q_ref.dtype)
        k_ref[:, sl] = (kk[:, sl] + kr).astype(k_ref.dtype)
    v_ref[...] = _dot(ckvn, wv_ref[...]).astype(v_ref.dtype)


def _mla_attn_kernel(q_ref, k_ref, v_ref, o_ref, vt_ref, *state, tq, scale):
    qi = pl.program_id(2)
    n_heads = q_ref.shape[1] // LANES

    @pl.when(qi == 0)
    def _():
        for pr in range(n_heads // 2):
            _transpose_chunks(v_ref.at[:, pr * LANES:(pr + 1) * LANES], vt_ref.at[pr], tq)

    q_t = [q_ref[:, hh * LANES:(hh + 1) * LANES].astype(F32).T.astype(BF16) for hh in range(n_heads)]
    key = lax.broadcasted_iota(jnp.int32, (tq, tq), 0)
    qry = lax.broadcasted_iota(jnp.int32, (tq, tq), 1)
    causal = key <= qry

    def scores(hh, c):
        return _dot(k_ref[pl.ds(pl.multiple_of(c * tq, tq), tq), hh * LANES:(hh + 1) * LANES], q_t[hh])

    def values(hh, c):
        return vt_ref[hh // 2, c, (hh % 2) * B_V:(hh % 2 + 1) * B_V, :]

    res = _flash_causal_t(scores, values, n_heads, qi, causal, state, scale)
    for pr in range(n_heads // 2):
        o_t = jnp.concatenate([acc / l for l, acc in res[2 * pr:2 * pr + 2]], axis=0)
        o_ref[:, pr * LANES:(pr + 1) * LANES] = o_t.T.astype(o_ref.dtype)


def _mixer_b(h, w_down, q_norm, kv_norm, w_uq, w_ukv, w_o, ln_g, ln_b, alpha, batch, seq):
    t, d = h.shape
    dq = B_NOPE + B_ROPE
    pad = LANES - dq
    wd = jnp.concatenate([w_down[:, :B_Q_RANK + B_KV_RANK], jnp.zeros((d, B_NOPE), F32),
                          w_down[:, B_Q_RANK + B_KV_RANK:], jnp.zeros((d, pad), F32)], axis=1).astype(BF16)
    wq = jnp.pad(w_uq.reshape(B_Q_RANK, B_HEADS, dq), ((0, 0), (0, 0), (0, pad))
                 ).reshape(B_Q_RANK, B_HEADS * LANES).astype(BF16)
    wkv = w_ukv.reshape(B_KV_RANK, B_HEADS, B_NOPE + B_V)
    wk = jnp.pad(wkv[:, :, :B_NOPE], ((0, 0), (0, 0), (0, LANES - B_NOPE))
                 ).reshape(B_KV_RANK, B_HEADS * LANES).astype(BF16)
    wv = wkv[:, :, B_NOPE:].reshape(B_KV_RANK, B_HEADS * B_V).astype(BF16)
    tab = _rope_tables(seq, B_ROPE, LANES, B_NOPE)
    tm = PROJ_ROWS
    spb = seq // tm
    full = lambda a: pl.BlockSpec(a.shape, lambda i: (0,) * a.ndim)
    qn, kvn = q_norm.reshape(1, -1), kv_norm.reshape(1, -1)
    q, k, v = pl.pallas_call(
        _mla_proj_kernel,
        grid=(t // tm,),
        in_specs=[pl.BlockSpec((tm, d), lambda i: (i, 0)), full(wd), full(qn), full(kvn),
                  full(wq), full(wk), full(wv),
                  pl.BlockSpec((3, tm, LANES), lambda i: (0, i % spb, 0))],
        out_specs=[pl.BlockSpec((tm, B_HEADS * LANES), lambda i: (i, 0)),
                   pl.BlockSpec((tm, B_HEADS * LANES), lambda i: (i, 0)),
                   pl.BlockSpec((tm, B_HEADS * B_V), lambda i: (i, 0))],
        out_shape=[jax.ShapeDtypeStruct((t, B_HEADS * LANES), BF16),
                   jax.ShapeDtypeStruct((t, B_HEADS * LANES), BF16),
                   jax.ShapeDtypeStruct((t, B_HEADS * B_V), BF16)],
        compiler_params=_params("parallel"),
        name="mla_proj",
    )(h, wd, qn, kvn, wq, wk, wv, tab)

    tq = min(FLASH_ROWS, seq)
    nq = seq // tq
    hps = 4
    o = pl.pallas_call(
        functools.partial(_mla_attn_kernel, tq=tq, scale=dq ** -0.5),
        grid=(batch, B_HEADS // hps, nq),
        in_specs=[pl.BlockSpec((tq, hps * LANES), lambda b, p, i: (b * nq + i, p)),
                  pl.BlockSpec((seq, hps * LANES), lambda b, p, i: (b, p)),
                  pl.BlockSpec((seq, hps * B_V), lambda b, p, i: (b, p))],
        out_specs=pl.BlockSpec((tq, hps * B_V), lambda b, p, i: (b * nq + i, p)),
        out_shape=jax.ShapeDtypeStruct((t, B_HEADS * B_V), BF16),
        scratch_shapes=[pltpu.VMEM((hps // 2, nq, LANES, tq), BF16)] + _flash_scratch(hps, B_V, tq, tq),
        compiler_params=_params("parallel", "parallel", "arbitrary"),
        name="mla_attention",
    )(q, k, v)
    return _outproj_ln(o, w_o.astype(BF16), h, ln_g, ln_b, alpha)


def _gelu_tanh(x):
    return x * (0.5 * (1.0 + jnp.tanh(math.sqrt(2.0 / math.pi) * (x + 0.044715 * (x * x * x)))))


def _compress_kernel(x_ref, pe_ref, w1_ref, w2_ref, o_ref):
    n = x_ref.shape[0] // C_CMP_STRIDE
    a = b = None
    for l in range(C_CMP_STRIDE):
        y = x_ref[pl.ds(l, n, stride=C_CMP_STRIDE), :]
        ta = _dot((y + pe_ref[l:l + 1, :]).astype(BF16), w1_ref[l])
        tb = _dot((y + pe_ref[C_CMP_STRIDE + l:C_CMP_STRIDE + l + 1, :]).astype(BF16), w1_ref[C_CMP_STRIDE + l])
        a = ta if a is None else a + ta
        b = tb if b is None else b + tb
    hid = a + pltpu.roll(b, n - 1, 0)
    o_ref[0] = _dot(_gelu_tanh(hid).astype(BF16), w2_ref[...]).astype(o_ref.dtype)


def _compress(kvc, which, pe, w1, w2, batch, seq):
    n = seq // C_CMP_STRIDE
    hp = LANES // HEAD_DIM
    pairs = C_KV_HEADS // hp
    eye = jnp.eye(hp, dtype=F32)
    w1bd = jnp.einsum("lij,gh->lgihj", w1.reshape(C_CMP_LEN, HEAD_DIM, C_CMP_HIDDEN), eye)
    w1bd = w1bd.reshape(C_CMP_LEN, LANES, hp * C_CMP_HIDDEN).astype(BF16)
    w2bd = jnp.einsum("ij,gh->gihj", jnp.concatenate([w2, w2], axis=1), eye)
    w2bd = w2bd.reshape(hp * C_CMP_HIDDEN, hp * LANES).astype(BF16)
    return pl.pallas_call(
        _compress_kernel,
        grid=(batch, pairs),
        in_specs=[pl.BlockSpec((seq, LANES), lambda i, j: (i, which * pairs + j)),
                  pl.BlockSpec((C_CMP_LEN, LANES), lambda i, j: (0, 0)),
                  pl.BlockSpec(w1bd.shape, lambda i, j: (0, 0, 0)),
                  pl.BlockSpec(w2bd.shape, lambda i, j: (0, 0))],
        out_specs=pl.BlockSpec((1, n, hp * LANES), lambda i, j: (i, 0, j)),
        out_shape=jax.ShapeDtypeStruct((batch, n, C_KV_HEADS * LANES), BF16),
        compiler_params=_params("parallel", "parallel"),
        name="nsa_compress",
    )(kvc, jnp.tile(pe, (1, hp)), w1bd, w2bd)


def _nsa_attn_kernel(q_ref, ks_ref, vs_ref, kw_ref, vw_ref, kc_ref, vc_ref, ov_ref, ex_ref, gt_ref,
                     o_ref, vst_ref, vwt_ref, vct_ref, *state, tq, tk, seq):
    qi = pl.program_id(2)
    q0 = qi * tq
    group = C_HEADS // C_KV_HEADS
    n_sel_blocks = seq // C_SEL_LEN
    nc = kc_ref.shape[1]

    @pl.when(qi == 0)
    def _():
        _transpose_chunks(vs_ref, vst_ref, tk)
        _transpose_chunks(vw_ref, vwt_ref, tq)
        vct_ref[...] = vc_ref[0].astype(F32).T.astype(vct_ref.dtype)

    top = lax.broadcasted_iota(jnp.int32, (LANES, tq), 0) < HEAD_DIM
    qa_t = q_ref[:, :LANES].astype(F32).T
    qb_t = q_ref[:, LANES:].astype(F32).T
    qs_t = jnp.concatenate([jnp.where(top, qa_t, 0.0), jnp.where(top, 0.0, qa_t),
                            jnp.where(top, qb_t, 0.0), jnp.where(top, 0.0, qb_t)], axis=1).astype(BF16)
    rep = lambda a: jnp.concatenate([a] * group, axis=1)

    def masked_softmax(s, valid):
        s = jnp.where(valid, s, NEG)
        e = jnp.exp(s - jnp.maximum(jnp.max(s, 0, keepdims=True), MAX_FLOOR))
        return e / jnp.maximum(jnp.sum(e, 0, keepdims=True), 1e-30)

    t_c = q0 + lax.broadcasted_iota(jnp.int32, (nc, tq), 1)
    n_c = lax.broadcasted_iota(jnp.int32, (nc, tq), 0)
    c_valid = rep(n_c * C_CMP_STRIDE + (C_CMP_LEN - 1) <= t_c)
    p_cmp = masked_softmax(_dot(kc_ref[0], qs_t), c_valid).astype(BF16)
    o_cmp = _dot(vct_ref[...], p_cmp)
    imp4 = _dot(ov_ref[...], p_cmp)
    imp = imp4[:n_sel_blocks, :tq]
    for r in range(1, group):
        imp = imp + imp4[:n_sel_blocks, r * tq:(r + 1) * tq]

    blk = lax.broadcasted_iota(jnp.int32, (n_sel_blocks, tq), 0)
    cur = (q0 + lax.broadcasted_iota(jnp.int32, (n_sel_blocks, tq), 1)) // C_SEL_LEN
    imp = jnp.where((blk == 0) | (blk == cur) | (blk == cur - 1), jnp.inf, imp)
    imp = jnp.where(blk <= cur, imp, -jnp.inf)
    rank = jnp.zeros((n_sel_blocks, tq), F32)
    for i in range(n_sel_blocks):
        r_i = imp[i:i + 1, :]
        rank = rank + jnp.where(blk > i, jnp.where(r_i >= imp, 1.0, 0.0), jnp.where(r_i > imp, 1.0, 0.0))
    sel = jnp.where(rank < min(C_N_SEL, n_sel_blocks), 0.0, NEG)
    sel = jnp.concatenate([sel, jnp.zeros((LANES - n_sel_blocks, tq), F32)], axis=0)
    qs_aug = jnp.concatenate([qs_t, rep(sel).astype(BF16)], axis=0)

    span = C_WINDOW + tq
    start = pl.multiple_of(jnp.maximum(q0 - C_WINDOW, 0), tq)
    t_w = q0 + lax.broadcasted_iota(jnp.int32, (span, tq), 1)
    dist = t_w - (start + lax.broadcasted_iota(jnp.int32, (span, tq), 0))
    w_valid = rep((dist >= 0) & (dist < C_WINDOW))
    p_win = masked_softmax(_dot(kw_ref[pl.ds(start, span), :], qs_t), w_valid).astype(BF16)
    o_win = _dot(vwt_ref[start // tq], p_win[:tq])
    for cb in range(1, span // tq):
        o_win = o_win + _dot(vwt_ref[start // tq + cb], p_win[cb * tq:(cb + 1) * tq])

    t_k = q0 + lax.broadcasted_iota(jnp.int32, (tk, tq), 1)
    k_k = lax.broadcasted_iota(jnp.int32, (tk, tq), 0)

    def sel_scores(ch, c):
        base = pl.multiple_of(c * tk, tk)
        k_aug = jnp.concatenate([ks_ref[pl.ds(base, tk), :], ex_ref[c]], axis=1)
        return jnp.where(rep(base + k_k <= t_k), _dot(k_aug, qs_aug), NEG)

    (l, acc), = _flash_causal_t(sel_scores, lambda ch, c: vst_ref[c], 1, (q0 + tq - 1) // tk, None, state)
    o_slc = acc / jnp.maximum(l, 1e-30)

    g_t = gt_ref[...].T
    outs = []
    for r in range(group):
        cols = slice(r * tq, (r + 1) * tq)
        outs.append(o_cmp[:, cols] * g_t[3 * r:3 * r + 1] + o_slc[:, cols] * g_t[3 * r + 1:3 * r + 2]
                    + o_win[:, cols] * g_t[3 * r + 2:3 * r + 3])
    o_ref[:, :LANES] = jnp.where(top, outs[0], outs[1]).T.astype(o_ref.dtype)
    o_ref[:, LANES:] = jnp.where(top, outs[2], outs[3]).T.astype(o_ref.dtype)


def _mixer_c(h, w_in, pos_k, pos_v, wk1, wk2, wv1, wv2, w_o, ln_g, ln_b, alpha, batch, seq):
    t, d = h.shape
    qd, kd = C_HEADS * HEAD_DIM, C_KV_HEADS * HEAD_DIM
    group = C_HEADS // C_KV_HEADS
    kv = lambda i: w_in[:, qd + i * kd:qd + (i + 1) * kd]
    wg = jnp.pad(w_in[:, qd + 6 * kd:].reshape(d, C_KV_HEADS, group * 3),
                 ((0, 0), (0, 0), (0, LANES - group * 3))).reshape(d, C_KV_HEADS * LANES)
    w = jnp.concatenate([w_in[:, :qd], _dup_cols(kv(2), C_KV_HEADS), _dup_cols(kv(4), C_KV_HEADS),
                         _dup_cols(kv(3), C_KV_HEADS), _dup_cols(kv(5), C_KV_HEADS),
                         kv(0), kv(1), wg], axis=1).astype(BF16)
    tab = _rope_tables(seq, ROT_DIM, HEAD_DIM, 0)
    kw2 = 2 * kd
    segs = [(0, qd, "rope", HEAD_DIM ** -0.5), (qd, 2 * kw2, "rope", 1.0), (qd + 2 * kw2, 2 * kw2, "plain", 1.0),
            (qd + 4 * kw2, 2 * kd, "plain", 1.0), (qd + 4 * kw2 + 2 * kd, C_KV_HEADS * LANES, "sigmoid", 1.0)]
    q, ksw, vsw, kvc, gates = _proj(h, w, tab, segs, [BF16, BF16, BF16, F32, F32], seq, ROT_DIM // 2)
    k_cmp = _compress(kvc, 0, pos_k, wk1, wk2, batch, seq)
    v_cmp = _compress(kvc, 1, pos_v, wv1, wv2, batch, seq)

    nc = seq // C_CMP_STRIDE
    nsb = seq // C_SEL_LEN
    tq = 128
    tk = min(512, seq)
    nq = seq // tq
    cs = np.arange(nc)[None, :] * C_CMP_STRIDE
    ss = np.arange(LANES)[:, None] * C_SEL_LEN
    overlap = ((cs <= ss + C_SEL_LEN - 1) & (ss <= cs + C_CMP_LEN - 1) & (np.arange(LANES)[:, None] < nsb))
    overlap = jnp.asarray(overlap, BF16)
    key_blk = (np.arange(seq) // C_SEL_LEN).reshape(seq // tk, tk, 1)
    expand = jnp.asarray(key_blk == np.arange(LANES)[None, None, :], BF16)

    per_bg = lambda width: pl.BlockSpec((seq, LANES), lambda b, g, i, width=width: (b, width + g))
    o = pl.pallas_call(
        functools.partial(_nsa_attn_kernel, tq=tq, tk=tk, seq=seq),
        grid=(batch, C_KV_HEADS, nq),
        in_specs=[pl.BlockSpec((tq, 2 * LANES), lambda b, g, i: (b * nq + i, g)),
                  per_bg(0), per_bg(0), per_bg(C_KV_HEADS), per_bg(C_KV_HEADS),
                  pl.BlockSpec((1, nc, LANES), lambda b, g, i: (b, 0, g)),
                  pl.BlockSpec((1, nc, LANES), lambda b, g, i: (b, 0, g)),
                  pl.BlockSpec((LANES, nc), lambda b, g, i: (0, 0)),
                  pl.BlockSpec((seq // tk, tk, LANES), lambda b, g, i: (0, 0, 0)),
                  pl.BlockSpec((tq, LANES), lambda b, g, i: (b * nq + i, g))],
        out_specs=pl.BlockSpec((tq, 2 * LANES), lambda b, g, i: (b * nq + i, g)),
        out_shape=jax.ShapeDtypeStruct((t, qd), BF16),
        scratch_shapes=[pltpu.VMEM((seq // tk, LANES, tk), BF16), pltpu.VMEM((seq // tq, LANES, tq), BF16),
                        pltpu.VMEM((LANES, nc), BF16)] + _flash_scratch(1, LANES, group * tq, tk),
        compiler_params=_params("parallel", "parallel", "arbitrary"),
        name="nsa_attention",
    )(q, ksw, vsw, ksw, vsw, k_cmp, v_cmp, overlap, expand, gates)
    return _outproj_ln(o, w_o.astype(BF16), h, ln_g, ln_b, alpha)


def _diff_attn_kernel(q_ref, k_ref, v_ref, lam_ref, sub_ref, o_ref, vt_ref, *state, tq, lam_init):
    qi = pl.program_id(2)

    n_heads = q_ref.shape[1] // LANES

    @pl.when(qi == 0)
    def _():
        for hd in range(n_heads):
            _transpose_chunks(v_ref.at[:, hd * LANES:(hd + 1) * LANES], vt_ref.at[hd], tq)

    lam = (jnp.exp(jnp.sum(lam_ref[0:1, :] * lam_ref[1:2, :], -1, keepdims=True))
           - jnp.exp(jnp.sum(lam_ref[2:3, :] * lam_ref[3:4, :], -1, keepdims=True)) + lam_init)
    top = lax.broadcasted_iota(jnp.int32, (LANES, tq), 0) < D_SUB
    qs_t = []
    for hd in range(n_heads):
        q_t = q_ref[:, hd * LANES:(hd + 1) * LANES].astype(F32).T
        qs_t.append(jnp.concatenate([jnp.where(top, q_t, 0.0), jnp.where(top, 0.0, q_t)], axis=1).astype(BF16))
    key = lax.broadcasted_iota(jnp.int32, (tq, tq), 0)
    qry = lax.broadcasted_iota(jnp.int32, (tq, tq), 1)
    causal = jnp.concatenate([key <= qry] * 2, axis=1)

    def scores(hd, c):
        return _dot(k_ref[pl.ds(pl.multiple_of(c * tq, tq), tq), hd * LANES:(hd + 1) * LANES], qs_t[hd])

    def values(hd, c):
        return vt_ref[hd, c]

    res = _flash_causal_t(scores, values, n_heads, qi, causal, state)
    for hd in range(n_heads):
        l, acc = res[hd]
        o = (acc / l).T
        o = o[:tq] - lam * o[tq:]
        o = o * lax.rsqrt(jnp.mean(o * o, -1, keepdims=True) + RMS_EPS) * sub_ref[...]
        o_ref[:, hd * LANES:(hd + 1) * LANES] = (o * (1.0 - lam_init)).astype(o_ref.dtype)


def _mixer_d(h, w_in, lq1, lk1, lq2, lk2, subln, w_o, ln_g, ln_b, alpha, layer_idx, batch, seq):
    t, d = h.shape
    qd = D_HEADS * 2 * D_SUB
    tab = _rope_tables(seq, ROT_DIM, D_SUB, 0)
    segs = [(0, qd, "rope", D_SUB ** -0.5), (qd, qd, "rope", 1.0), (2 * qd, qd, "plain", 1.0)]
    q, k, v = _proj(h, w_in.astype(BF16), tab, segs, [BF16, BF16, BF16], seq, ROT_DIM // 2)
    lam_init = 0.8 - 0.6 * math.exp(-0.3 * layer_idx)
    lam_in = jnp.stack([lq1, lk1, lq2, lk2]).astype(F32)
    tq = min(FLASH_ROWS, seq)
    nq = seq // tq
    hps = 2
    o = pl.pallas_call(
        functools.partial(_diff_attn_kernel, tq=tq, lam_init=lam_init),
        grid=(batch, D_HEADS // hps, nq),
        in_specs=[pl.BlockSpec((tq, hps * LANES), lambda b, hd, i: (b * nq + i, hd)),
                  pl.BlockSpec((seq, hps * LANES), lambda b, hd, i: (b, hd)),
                  pl.BlockSpec((seq, hps * LANES), lambda b, hd, i: (b, hd)),
                  pl.BlockSpec((4, D_SUB), lambda b, hd, i: (0, 0)),
                  pl.BlockSpec((1, 2 * D_SUB), lambda b, hd, i: (0, 0))],
        out_specs=pl.BlockSpec((tq, hps * LANES), lambda b, hd, i: (b * nq + i, hd)),
        out_shape=jax.ShapeDtypeStruct((t, qd), BF16),
        scratch_shapes=[pltpu.VMEM((hps, nq, LANES, tq), BF16)] + _flash_scratch(hps, LANES, 2 * tq, tq),
        compiler_params=_params("parallel", "parallel", "arbitrary"),
        name="diff_attention",
    )(q, k, v, lam_in, subln.reshape(1, -1))
    return _outproj_ln(o, w_o.astype(BF16), h, ln_g, ln_b, alpha)


def _router_kernel(x_ref, w_ref, o_ref):
    xb = x_ref[...].astype(BF16)
    logits = _dot(xb, w_ref[...])
    lg = logits[:, :LANES].T[:8]
    le = logits[:, LANES:].T[:M_EXPERTS]
    far = 4 * LANES

    def softmax(x, valid):
        m = jnp.max(jnp.where(valid, x, NEG), 0, keepdims=True)
        e = jnp.where(valid, jnp.exp(x - m), 0.0)
        return e / jnp.sum(e, 0, keepdims=True)

    def first_max(p, valid, row):
        top = jnp.max(jnp.where(valid, p, -1.0), 0, keepdims=True)
        idx = jnp.min(jnp.where(valid & (p == top), row, far), 0, keepdims=True)
        return top, idx

    g_row = lax.broadcasted_iota(jnp.int32, lg.shape, 0)
    e_row = lax.broadcasted_iota(jnp.int32, le.shape, 0)
    g_valid = g_row < M_GROUPS
    g_w, g_idx = first_max(softmax(lg, g_valid), g_valid, g_row)
    e_valid = (e_row >= g_idx * M_PER_GROUP) & (e_row < (g_idx + 1) * M_PER_GROUP)
    pe = softmax(le, e_valid)
    w0, i0 = first_max(pe, e_valid, e_row)
    w1, i1 = first_max(pe, e_valid & (e_row != i0), e_row)
    tot = w0 + w1
    out = jnp.where(g_row == 0, i0.astype(F32), 0.0)
    out = jnp.where(g_row == 1, i1.astype(F32), out)
    out = jnp.where(g_row == 2, g_w * w0 / tot, out)
    out = jnp.where(g_row == 3, g_w * w1 / tot, out)
    o_ref[...] = out


def _row_gather(src_hbm, idx_ref, buf, sem):
    for r in range(buf.shape[0]):
        pltpu.make_async_copy(src_hbm.at[pl.ds(idx_ref[0, 0, r], 1), :], buf.at[pl.ds(r, 1), :], sem).start()


def _row_gather_wait(src_hbm, buf, sem):
    pltpu.make_async_copy(src_hbm.at[pl.ds(0, buf.shape[0]), :], buf, sem).wait()


def _row_scatter(buf, idx_ref, dst_hbm, sem):
    for r in range(buf.shape[0]):
        pltpu.make_async_copy(buf.at[pl.ds(r, 1), :], dst_hbm.at[pl.ds(idx_ref[0, 0, r], 1), :], sem).start()


def _row_scatter_wait(buf, dst_hbm, sem):
    pltpu.make_async_copy(buf, dst_hbm.at[pl.ds(0, buf.shape[0]), :], sem).wait()


def _moe_kernel(eid_ref, used_ref, src0_ref, src_ref, dstp_ref, h_hbm, wg_ref, wu_ref, wd_ref, out_hbm,
                xbuf0, xbuf1, ybuf0, ybuf1, wgb, wub, wdb, gsem, ssem, *, tm):
    t = pl.program_id(0)
    used = used_ref[0]
    xbuf = (xbuf0, xbuf1)
    ybuf = (ybuf0, ybuf1)

    @pl.when(t == 0)
    def _():
        _row_gather(h_hbm, src0_ref, xbuf0, gsem.at[0])
        ybuf1[...] = jnp.zeros(ybuf1.shape, F32)
        fill = pltpu.make_async_copy(ybuf1, out_hbm.at[pl.ds(out_hbm.shape[0] - 2 * tm, tm), :], ssem.at[0])
        fill.start()
        fill.wait()

    @pl.when((t < used) & ((t == 0) | (eid_ref[t] != eid_ref[jnp.maximum(t - 1, 0)])))
    def _():
        wgb[...] = wg_ref[0, 0].astype(BF16)
        wub[...] = wu_ref[0, 0].astype(BF16)
        wdb[...] = wd_ref[0, 0].astype(BF16)

    for s in range(2):
        @pl.when((t < used) & (t % 2 == s))
        def _(s=s):
            _row_gather_wait(h_hbm, xbuf[s], gsem.at[s])

            @pl.when(t >= 1)
            def _():
                _row_scatter_wait(ybuf[s], out_hbm, ssem.at[s])

            _row_scatter(ybuf[1 - s], dstp_ref, out_hbm, ssem.at[1 - s])
            _row_gather(h_hbm, src_ref, xbuf[1 - s], gsem.at[1 - s])
            xb = xbuf[s][...].astype(BF16)
            gate = _dot(xb, wgb[...])
            up = _dot(xb, wub[...])
            hid = (gate * (1.0 / (1.0 + jnp.exp(-gate))) * up).astype(BF16)
            ybuf[s][...] = _dot(hid, wdb[...])

        @pl.when((t == used) & (t % 2 == s))
        def _(s=s):
            _row_scatter(ybuf[1 - s], dstp_ref, out_hbm, ssem.at[1 - s])
            _row_scatter_wait(ybuf[s], out_hbm, ssem.at[s])
            _row_scatter_wait(ybuf[1 - s], out_hbm, ssem.at[1 - s])
            _row_gather_wait(h_hbm, xbuf[s], gsem.at[s])


def _combine_ln_kernel(h_ref, y0_ref, y1_ref, w_ref, g_ref, b_ref, o_ref, *, alpha):
    w = w_ref[...]
    y = w[:, 0:1] * y0_ref[...] + w[:, 1:2] * y1_ref[...]
    o_ref[...] = _layer_norm(alpha * h_ref[...] + y, g_ref[...], b_ref[...])


def _hier_moe_ln(h, w_group, w_expert, w_gate, w_up, w_down, layer, ln_g, ln_b, alpha):
    t, d = h.shape
    tm = MOE_ROWS
    wr = jnp.concatenate([jnp.pad(w_group, ((0, 0), (0, LANES - M_GROUPS))),
                          jnp.pad(w_expert, ((0, 0), (0, LANES - M_EXPERTS)))], axis=1).astype(BF16)
    routed = pl.pallas_call(
        _router_kernel,
        grid=(t // LN_ROWS,),
        in_specs=[pl.BlockSpec((LN_ROWS, d), lambda i: (i, 0)),
                  pl.BlockSpec((d, 2 * LANES), lambda i: (0, 0))],
        out_specs=pl.BlockSpec((8, LN_ROWS), lambda i: (0, i)),
        out_shape=jax.ShapeDtypeStruct((8, t), F32),
        compiler_params=_params("parallel"),
        name="moe_router",
    )(h, wr)

    n_rows = t * M_TOPK
    n_tiles = n_rows // tm + M_EXPERTS
    expert = routed[:M_TOPK].T.astype(jnp.int32).reshape(-1)
    weight = routed[M_TOPK:2 * M_TOPK].T
    order = jnp.argsort(expert).astype(jnp.int32)
    experts = jnp.arange(M_EXPERTS, dtype=jnp.int32)[None, :]
    sizes = jnp.sum(expert[:, None] == experts, axis=0, dtype=jnp.int32)
    tiles = (sizes + tm - 1) // tm
    tile_end = jnp.cumsum(tiles)
    seg_start = jnp.cumsum(sizes) - sizes
    used = tile_end[-1]
    tile_ids = jnp.arange(n_tiles, dtype=jnp.int32)
    tile_eid = jnp.sum(jnp.minimum(tile_ids, used - 1)[:, None] >= tile_end[None, :], axis=1).astype(jnp.int32)
    tile_first = (tile_end - tiles)[tile_eid]
    lane_r = jnp.arange(tm, dtype=jnp.int32)[None, :]
    offs = (tile_ids - tile_first)[:, None] * tm + lane_r
    valid = (offs < sizes[tile_eid][:, None]) & (tile_ids < used)[:, None]
    row = order[jnp.clip(seg_start[tile_eid][:, None] + offs, 0, n_rows - 1)]
    src = jnp.where(valid, row // M_TOPK, 0).reshape(n_tiles, 1, tm)
    trash = n_rows + (tile_ids % 2)[:, None] * tm + lane_r
    dst = jnp.where(valid, (row % M_TOPK) * t + row // M_TOPK, trash)
    dstp = jnp.concatenate([n_rows + tm + lane_r, dst], axis=0).reshape(n_tiles + 1, 1, tm)

    wspec = lambda shape: pl.BlockSpec((1, 1) + shape, lambda i, eid, used: (layer, eid[i], 0, 0))
    idx_spec = pl.BlockSpec((1, 1, tm), lambda i, eid, used: (i, 0, 0), memory_space=pltpu.SMEM)
    nxt_spec = pl.BlockSpec((1, 1, tm), lambda i, eid, used: (jnp.minimum(i + 1, n_tiles - 1), 0, 0),
                            memory_space=pltpu.SMEM)
    y = pl.pallas_call(
        functools.partial(_moe_kernel, tm=tm),
        grid_spec=pltpu.PrefetchScalarGridSpec(
            num_scalar_prefetch=2,
            grid=(n_tiles,),
            in_specs=[idx_spec, nxt_spec, idx_spec,
                      pl.BlockSpec(memory_space=pl.ANY),
                      wspec((d, M_HIDDEN)), wspec((d, M_HIDDEN)), wspec((M_HIDDEN, d))],
            out_specs=pl.BlockSpec(memory_space=pl.ANY),
            scratch_shapes=[pltpu.VMEM((tm, d), F32), pltpu.VMEM((tm, d), F32),
                            pltpu.VMEM((tm, d), F32), pltpu.VMEM((tm, d), F32),
                            pltpu.VMEM((d, M_HIDDEN), BF16), pltpu.VMEM((d, M_HIDDEN), BF16),
                            pltpu.VMEM((M_HIDDEN, d), BF16),
                            pltpu.SemaphoreType.DMA((2,)), pltpu.SemaphoreType.DMA((2,))]),
        out_shape=jax.ShapeDtypeStruct((n_rows + 2 * tm, d), F32),
        compiler_params=_params("arbitrary"),
        name="moe_experts",
    )(tile_eid, used.reshape(1).astype(jnp.int32), src, src, dstp, h, w_gate, w_up, w_down)

    tc = LN_ROWS
    nt = t // tc
    return pl.pallas_call(
        functools.partial(_combine_ln_kernel, alpha=alpha),
        grid=(nt,),
        in_specs=[pl.BlockSpec((tc, d), lambda i: (i, 0)),
                  pl.BlockSpec((tc, d), lambda i: (i, 0)),
                  pl.BlockSpec((tc, d), lambda i: (nt + i, 0)),
                  pl.BlockSpec((tc, M_TOPK), lambda i: (i, 0)),
                  pl.BlockSpec((1, d), lambda i: (0, 0)),
                  pl.BlockSpec((1, d), lambda i: (0, 0))],
        out_specs=pl.BlockSpec((tc, d), lambda i: (i, 0)),
        out_shape=jax.ShapeDtypeStruct((t, d), F32),
        compiler_params=_params("parallel"),
        name="moe_combine_ln",
    )(h, y, y, weight, ln_g.reshape(1, d), ln_b.reshape(1, d))


def kernel(x, a_w_in, a_sinks, a_w_o, b_w_down, b_q_norm, b_kv_norm, b_w_uq, b_w_ukv, b_w_o, c_w_in, c_pos_k, c_pos_v, c_wk1, c_wk2, c_wv1, c_wv2, c_w_o, d_w_in, d_lq1, d_lk1, d_lq2, d_lk2, d_subln, d_w_o, moe_w_group, moe_w_expert, moe_w_gate, moe_w_up, moe_w_down, ln_g, ln_b):
    batch, seq, d = x.shape
    depth = ln_g.shape[0]
    alpha = (2 * depth) ** 0.25
    h = x.reshape(batch * seq, d)
    for i in range(depth):
        kind, j = i % N_MIXERS, i // N_MIXERS
        g, b = ln_g[i, 0], ln_b[i, 0]
        if kind == 0:
            h = _mixer_a(h, a_w_in[j], a_sinks[j], a_w_o[j], g, b, alpha, batch, seq)
        elif kind == 1:
            h = _mixer_b(h, b_w_down[j], b_q_norm[j], b_kv_norm[j], b_w_uq[j], b_w_ukv[j], b_w_o[j],
                         g, b, alpha, batch, seq)
        elif kind == 2:
            h = _mixer_c(h, c_w_in[j], c_pos_k[j], c_pos_v[j], c_wk1[j], c_wk2[j], c_wv1[j], c_wv2[j],
                         c_w_o[j], g, b, alpha, batch, seq)
        else:
            h = _mixer_d(h, d_w_in[j], d_lq1[j], d_lk1[j], d_lq2[j], d_lk2[j], d_subln[j], d_w_o[j],
                         g, b, alpha, i, batch, seq)
        h = _hier_moe_ln(h, moe_w_group[i], moe_w_expert[i], moe_w_gate, moe_w_up, moe_w_down, i,
                         ln_g[i, 1], ln_b[i, 1], alpha)
    return h.reshape(batch, seq, d)
```

```python
import functools
import math

import numpy as np
import jax
import jax.numpy as jnp
from jax import lax
from jax.experimental import pallas as pl
from jax.experimental.pallas import tpu as pltpu

F32 = jnp.float32
BF16 = jnp.bfloat16

HEAD_DIM = 64
ROPE_THETA = 500000.0
ROT_DIM = HEAD_DIM // 4
A_HEADS, A_KV_HEADS, A_WINDOW = 16, 4, 128
B_HEADS, B_Q_RANK, B_KV_RANK, B_NOPE, B_ROPE, B_V = 16, 384, 256, 64, 32, 64
C_HEADS, C_KV_HEADS = 16, 4
C_CMP_LEN, C_CMP_STRIDE, C_CMP_HIDDEN = 32, 16, 128
C_SEL_LEN, C_N_SEL, C_WINDOW = 64, 16, 512
D_HEADS, D_SUB = 8, 64
M_GROUPS, M_PER_GROUP, M_TOPK, M_HIDDEN = 4, 8, 2, 512
M_EXPERTS = M_GROUPS * M_PER_GROUP
N_MIXERS = 4
LN_EPS = 1e-5
RMS_EPS = 1e-6

LANES = 128
NEG = -1e30
MAX_FLOOR = -1e20
VMEM_LIMIT = 48 * 1024 * 1024

PROJ_ROWS = 256
LN_ROWS = 512
MOE_ROWS = 256
FLASH_ROWS = 512


def _params(*sem):
    return pltpu.CompilerParams(dimension_semantics=sem, vmem_limit_bytes=VMEM_LIMIT)


def _dot(a, b):
    return jnp.dot(a, b, preferred_element_type=F32)


def _dot_t(a, b):
    return lax.dot_general(a, b, (((1,), (1,)), ((), ())), preferred_element_type=F32)


def _rope_tables(seq, rot_dim, period, off):
    half = rot_dim // 2
    inv_freq = 1.0 / (ROPE_THETA ** (jnp.arange(half, dtype=F32) * (2.0 / rot_dim)))
    ang = jnp.arange(seq, dtype=F32)[:, None] * inv_freq[None, :]
    cos, sin = jnp.cos(ang), jnp.sin(ang)
    lane = np.arange(LANES) % period - off
    first = (lane >= 0) & (lane < half)
    second = (lane >= half) & (lane < rot_dim)
    idx = np.where(first, lane, np.where(second, lane - half, 0))
    cg, sg = cos[:, idx], sin[:, idx]
    c = jnp.where(first | second, cg, 1.0)
    sa = jnp.where(first, -sg, 0.0)
    sb = jnp.where(second, sg, 0.0)
    return jnp.stack([c, sa, sb])


def _rope_block(x, tab_ref, half):
    return (x * tab_ref[0] + pltpu.roll(x, LANES - half, 1) * tab_ref[1]
            + pltpu.roll(x, half, 1) * tab_ref[2])


def _proj_kernel(x_ref, w_ref, tab_ref, *out_refs, segs, half):
    xb = x_ref[...].astype(BF16)
    for (start, width, kind, scale), o_ref in zip(segs, out_refs):
        acc = _dot(xb, w_ref[:, start:start + width])
        if kind == "rope":
            for c in range(width // LANES):
                y = _rope_block(acc[:, c * LANES:(c + 1) * LANES], tab_ref, half)
                if scale != 1.0:
                    y = y * scale
                o_ref[:, c * LANES:(c + 1) * LANES] = y.astype(o_ref.dtype)
        elif kind == "sigmoid":
            o_ref[...] = (1.0 / (1.0 + jnp.exp(-acc))).astype(o_ref.dtype)
        else:
            o_ref[...] = acc.astype(o_ref.dtype)


def _proj(x, w, tab, segs, dtypes, seq, half):
    t, k = x.shape
    n = w.shape[1]
    tm = PROJ_ROWS
    spb = seq // tm
    out_shape = [jax.ShapeDtypeStruct((t, s[1]), d) for s, d in zip(segs, dtypes)]
    return pl.pallas_call(
        functools.partial(_proj_kernel, segs=tuple(segs), half=half),
        grid=(t // tm,),
        in_specs=[pl.BlockSpec((tm, k), lambda i: (i, 0)),
                  pl.BlockSpec((k, n), lambda i: (0, 0)),
                  pl.BlockSpec((3, tm, LANES), lambda i: (0, i % spb, 0))],
        out_specs=[pl.BlockSpec((tm, s[1]), lambda i: (i, 0)) for s in segs],
        out_shape=out_shape,
        compiler_params=_params("parallel"),
        name="proj",
    )(x, w, tab)


def _layer_norm(z, g, b):
    mu = jnp.mean(z, -1, keepdims=True)
    zc = z - mu
    var = jnp.mean(zc * zc, -1, keepdims=True)
    return zc * lax.rsqrt(var + LN_EPS) * g + b


def _outproj_ln_kernel(o_ref, w_ref, h_ref, g_ref, b_ref, out_ref, *, alpha):
    y = _dot(o_ref[...], w_ref[...])
    out_ref[...] = _layer_norm(alpha * h_ref[...] + y, g_ref[...], b_ref[...])


def _outproj_ln(o, w, h, g, b, alpha):
    t, k = o.shape
    d = w.shape[1]
    tm = LN_ROWS
    return pl.pallas_call(
        functools.partial(_outproj_ln_kernel, alpha=alpha),
        grid=(t // tm,),
        in_specs=[pl.BlockSpec((tm, k), lambda i: (i, 0)),
                  pl.BlockSpec((k, d), lambda i: (0, 0)),
                  pl.BlockSpec((tm, d), lambda i: (i, 0)),
                  pl.BlockSpec((1, d), lambda i: (0, 0)),
                  pl.BlockSpec((1, d), lambda i: (0, 0))],
        out_specs=pl.BlockSpec((tm, d), lambda i: (i, 0)),
        out_shape=jax.ShapeDtypeStruct((t, d), F32),
        compiler_params=_params("parallel"),
        name="outproj_ln",
    )(o, w, h, g.reshape(1, d), b.reshape(1, d))


def _stack_group_queries(q_ref, rows):
    lo = lax.broadcasted_iota(jnp.int32, (rows, LANES), 1) < HEAD_DIM
    qa, qb = q_ref[:, :LANES], q_ref[:, LANES:]
    z = jnp.zeros_like(qa)
    return jnp.concatenate([jnp.where(lo, qa, z), jnp.where(lo, z, qa),
                            jnp.where(lo, qb, z), jnp.where(lo, z, qb)], axis=0)


def _pair(lo_val, hi_val):
    lo = lax.broadcasted_iota(jnp.int32, lo_val.shape, 1) < HEAD_DIM
    return jnp.where(lo, lo_val, hi_val)


def _transpose_chunks(src_ref, dst_ref, chunk):
    for c in range(src_ref.shape[0] // chunk):
        dst_ref[c] = src_ref[c * chunk:(c + 1) * chunk, :].astype(F32).T.astype(dst_ref.dtype)


def _softmax_update_t(s, valid, m, l, scale=None):
    if valid is not None:
        s = jnp.where(valid, s, NEG)
    m_new = jnp.maximum(m, jnp.max(s, 0, keepdims=True))
    if scale is None:
        a = jnp.exp(m - m_new)
        p = jnp.exp(s - m_new)
    else:
        a = jnp.exp((m - m_new) * scale)
        p = jnp.exp((s - m_new) * scale)
    return m_new, a * l + jnp.sum(p, 0, keepdims=True), a, p.astype(BF16)


def _flash_scratch(n_chains, dv, m_cols, tk):
    return [pltpu.VMEM((n_chains, 3, 1, m_cols), F32), pltpu.VMEM((n_chains, dv, m_cols), F32),
            pltpu.VMEM((n_chains, 2, tk, m_cols), F32), pltpu.VMEM((n_chains, 2, tk, m_cols), BF16)]


def _flash_causal_t(score_fn, value_fn, n_chains, qi, causal, state, scale=None):
    st_ref, acc_ref, s_ref, p_ref = state
    for ch in range(n_chains):
        st_ref[ch, 0] = jnp.full(st_ref.shape[2:], MAX_FLOOR, F32)
        st_ref[ch, 1] = jnp.zeros(st_ref.shape[2:], F32)
        st_ref[ch, 2] = jnp.ones(st_ref.shape[2:], F32)
        acc_ref[ch] = jnp.zeros(acc_ref.shape[1:], F32)
        p_ref[ch, 1] = jnp.zeros(p_ref.shape[2:], BF16)
        s_ref[ch, 0] = score_fn(ch, 0)

    def half(c, cur, valid, last):
        nxt = 1 - cur
        for ch in range(n_chains):
            m, l, a, p = _softmax_update_t(s_ref[ch, cur], valid, st_ref[ch, 0], st_ref[ch, 1], scale)
            if not last:
                s_ref[ch, nxt] = score_fn(ch, c + 1)
            acc = st_ref[ch, 2] * acc_ref[ch] + _dot(value_fn(ch, jnp.maximum(c - 1, 0)), p_ref[ch, nxt])
            if last:
                acc = a * acc + _dot(value_fn(ch, c), p)
            else:
                p_ref[ch, cur] = p
                st_ref[ch, 2] = a
            acc_ref[ch] = acc
            st_ref[ch, 0] = m
            st_ref[ch, 1] = l

    def pair(j, carry):
        half(2 * j, 0, None, False)
        half(2 * j + 1, 1, None, False)
        return carry

    lax.fori_loop(0, qi // 2, pair, 0)

    @pl.when(qi % 2 == 1)
    def _():
        half(qi - 1, 0, None, False)
        half(qi, 1, causal, True)

    @pl.when(qi % 2 == 0)
    def _():
        half(qi, 0, causal, True)

    return [(st_ref[ch, 1], acc_ref[ch]) for ch in range(n_chains)]


def _swa_kernel(sink_ref, q_ref, kp_ref, kc_ref, vp_ref, vc_ref, o_ref):
    n = pl.program_id(1)
    blk = A_WINDOW
    i = lax.broadcasted_iota(jnp.int32, (blk, 2 * blk), 0)
    j = lax.broadcasted_iota(jnp.int32, (blk, 2 * blk), 1)
    dist = blk + i - j
    valid = (dist >= 0) & (dist < A_WINDOW) & ((n - 1) * blk + j >= 0)
    group = A_HEADS // A_KV_HEADS
    for g in range(A_KV_HEADS):
        qs = _stack_group_queries(q_ref.at[:, g * 2 * LANES:(g + 1) * 2 * LANES], blk)
        k = jnp.concatenate([kp_ref[:, g * LANES:(g + 1) * LANES],
                             kc_ref[:, g * LANES:(g + 1) * LANES]], axis=0)
        v = jnp.concatenate([vp_ref[:, g * LANES:(g + 1) * LANES],
                             vc_ref[:, g * LANES:(g + 1) * LANES]], axis=0)
        s = _dot_t(qs, k)
        ps = []
        for r in range(group):
            sink = sink_ref[g * group + r]
            sr = jnp.where(valid, s[r * blk:(r + 1) * blk], NEG)
            m = jnp.maximum(jnp.max(sr, -1, keepdims=True), sink)
            e = jnp.where(valid, jnp.exp(sr - m), 0.0)
            p = e / (jnp.sum(e, -1, keepdims=True) + jnp.exp(sink - m))
            ps.append(p.astype(BF16))
        o = _dot(jnp.concatenate(ps, axis=0), v)
        o_ref[:, g * 2 * LANES:g * 2 * LANES + LANES] = _pair(o[:blk], o[blk:2 * blk]).astype(o_ref.dtype)
        o_ref[:, g * 2 * LANES + LANES:(g + 1) * 2 * LANES] = _pair(
            o[2 * blk:3 * blk], o[3 * blk:]).astype(o_ref.dtype)


def _swa_attention(q, kd, vd, sinks, batch, seq):
    t = q.shape[0]
    blk = A_WINDOW
    nb = seq // blk
    qd = A_HEADS * HEAD_DIM
    kw = A_KV_HEADS * LANES
    cur = lambda b, n, s: (b * nb + n, 0)
    prev = lambda b, n, s: (b * nb + jnp.maximum(n - 1, 0), 0)
    return pl.pallas_call(
        _swa_kernel,
        grid_spec=pltpu.PrefetchScalarGridSpec(
            num_scalar_prefetch=1,
            grid=(batch, nb),
            in_specs=[pl.BlockSpec((blk, qd), cur),
                      pl.BlockSpec((blk, kw), prev), pl.BlockSpec((blk, kw), cur),
                      pl.BlockSpec((blk, kw), prev), pl.BlockSpec((blk, kw), cur)],
            out_specs=pl.BlockSpec((blk, qd), cur)),
        out_shape=jax.ShapeDtypeStruct((t, qd), BF16),
        compiler_params=_params("parallel", "parallel"),
        name="swa_attention",
    )(sinks.astype(F32), q, kd, kd, vd, vd)


def _dup_cols(w, heads):
    k = w.shape[0]
    w4 = w.reshape(k, heads, 1, HEAD_DIM)
    return jnp.broadcast_to(w4, (k, heads, 2, HEAD_DIM)).reshape(k, heads * 2 * HEAD_DIM)


def _mixer_a(h, w_in, sinks, w_o, ln_g, ln_b, alpha, batch, seq):
    qd, kd = A_HEADS * HEAD_DIM, A_KV_HEADS * HEAD_DIM
    w = jnp.concatenate([w_in[:, :qd], _dup_cols(w_in[:, qd:qd + kd], A_KV_HEADS),
                         _dup_cols(w_in[:, qd + kd:], A_KV_HEADS)], axis=1).astype(BF16)
    tab = _rope_tables(seq, ROT_DIM, HEAD_DIM, 0)
    segs = [(0, qd, "rope", HEAD_DIM ** -0.5), (qd, 2 * kd, "rope", 1.0), (qd + 2 * kd, 2 * kd, "plain", 1.0)]
    q, k2, v2 = _proj(h, w, tab, segs, [BF16, BF16, BF16], seq, ROT_DIM // 2)
    o = _swa_attention(q, k2, v2, sinks, batch, seq)
    return _outproj_ln(o, w_o.astype(BF16), h, ln_g, ln_b, alpha)


def _mla_proj_kernel(x_ref, wd_ref, qn_ref, kvn_ref, wq_ref, wk_ref, wv_ref, tab_ref,
                     q_ref, k_ref, v_ref):
    half = B_ROPE // 2
    c = _dot(x_ref[...].astype(BF16), wd_ref[...])
    cq, ckv = c[:, :B_Q_RANK], c[:, B_Q_RANK:B_Q_RANK + B_KV_RANK]
    kr = _rope_block(c[:, B_Q_RANK + B_KV_RANK:], tab_ref, half)
    cqn = (cq * lax.rsqrt(jnp.mean(cq * cq, -1, keepdims=True) + RMS_EPS) * qn_ref[...]).astype(BF16)
    ckvn = (ckv * lax.rsqrt(jnp.mean(ckv * ckv, -1, keepdims=True) + RMS_EPS) * kvn_ref[...]).astype(BF16)
    q = _dot(cqn, wq_ref[...])
    kk = _dot(ckvn, wk_ref[...])
    for hd in range(B_HEADS):
        sl = slice(hd * LANES, (hd + 1) * LANES)
        q_ref[:, sl] = _rope_block(q[:, sl], tab_ref, half).astype(q_ref.dtype)
        k_ref[:, sl] = (kk[:, sl] + kr).astype(k_ref.dtype)
    v_ref[...] = _dot(ckvn, wv_ref[...]).astype(v_ref.dtype)


def _mla_attn_kernel(q_ref, k_ref, v_ref, o_ref, vt_ref, *state, tq, scale):
    qi = pl.program_id(2)
    n_heads = q_ref.shape[1] // LANES

    @pl.when(qi == 0)
    def _():
        for pr in range(n_heads // 2):
            _transpose_chunks(v_ref.at[:, pr * LANES:(pr + 1) * LANES], vt_ref.at[pr], tq)

    q_t = [q_ref[:, hh * LANES:(hh + 1) * LANES].astype(F32).T.astype(BF16) for hh in range(n_heads)]
    key = lax.broadcasted_iota(jnp.int32, (tq, tq), 0)
    qry = lax.broadcasted_iota(jnp.int32, (tq, tq), 1)
    causal = key <= qry

    def scores(hh, c):
        return _dot(k_ref[pl.ds(pl.multiple_of(c * tq, tq), tq), hh * LANES:(hh + 1) * LANES], q_t[hh])

    def values(hh, c):
        return vt_ref[hh // 2, c, (hh % 2) * B_V:(hh % 2 + 1) * B_V, :]

    res = _flash_causal_t(scores, values, n_heads, qi, causal, state, scale)
    for pr in range(n_heads // 2):
        o_t = jnp.concatenate([acc / l for l, acc in res[2 * pr:2 * pr + 2]], axis=0)
        o_ref[:, pr * LANES:(pr + 1) * LANES] = o_t.T.astype(o_ref.dtype)


def _mixer_b(h, w_down, q_norm, kv_norm, w_uq, w_ukv, w_o, ln_g, ln_b, alpha, batch, seq):
    t, d = h.shape
    dq = B_NOPE + B_ROPE
    pad = LANES - dq
    wd = jnp.concatenate([w_down[:, :B_Q_RANK + B_KV_RANK], jnp.zeros((d, B_NOPE), F32),
                          w_down[:, B_Q_RANK + B_KV_RANK:], jnp.zeros((d, pad), F32)], axis=1).astype(BF16)
    wq = jnp.pad(w_uq.reshape(B_Q_RANK, B_HEADS, dq), ((0, 0), (0, 0), (0, pad))
                 ).reshape(B_Q_RANK, B_HEADS * LANES).astype(BF16)
    wkv = w_ukv.reshape(B_KV_RANK, B_HEADS, B_NOPE + B_V)
    wk = jnp.pad(wkv[:, :, :B_NOPE], ((0, 0), (0, 0), (0, LANES - B_NOPE))
                 ).reshape(B_KV_RANK, B_HEADS * LANES).astype(BF16)
    wv = wkv[:, :, B_NOPE:].reshape(B_KV_RANK, B_HEADS * B_V).astype(BF16)
    tab = _rope_tables(seq, B_ROPE, LANES, B_NOPE)
    tm = PROJ_ROWS
    spb = seq // tm
    full = lambda a: pl.BlockSpec(a.shape, lambda i: (0,) * a.ndim)
    qn, kvn = q_norm.reshape(1, -1), kv_norm.reshape(1, -1)
    q, k, v = pl.pallas_call(
        _mla_proj_kernel,
        grid=(t // tm,),
        in_specs=[pl.BlockSpec((tm, d), lambda i: (i, 0)), full(wd), full(qn), full(kvn),
                  full(wq), full(wk), full(wv),
                  pl.BlockSpec((3, tm, LANES), lambda i: (0, i % spb, 0))],
        out_specs=[pl.BlockSpec((tm, B_HEADS * LANES), lambda i: (i, 0)),
                   pl.BlockSpec((tm, B_HEADS * LANES), lambda i: (i, 0)),
                   pl.BlockSpec((tm, B_HEADS * B_V), lambda i: (i, 0))],
        out_shape=[jax.ShapeDtypeStruct((t, B_HEADS * LANES), BF16),
                   jax.ShapeDtypeStruct((t, B_HEADS * LANES), BF16),
                   jax.ShapeDtypeStruct((t, B_HEADS * B_V), BF16)],
        compiler_params=_params("parallel"),
        name="mla_proj",
    )(h, wd, qn, kvn, wq, wk, wv, tab)

    tq = min(FLASH_ROWS, seq)
    nq = seq // tq
    hps = 4
    o = pl.pallas_call(
        functools.partial(_mla_attn_kernel, tq=tq, scale=dq ** -0.5),
        grid=(batch, B_HEADS // hps, nq),
        in_specs=[pl.BlockSpec((tq, hps * LANES), lambda b, p, i: (b * nq + i, p)),
                  pl.BlockSpec((seq, hps * LANES), lambda b, p, i: (b, p)),
                  pl.BlockSpec((seq, hps * B_V), lambda b, p, i: (b, p))],
        out_specs=pl.BlockSpec((tq, hps * B_V), lambda b, p, i: (b * nq + i, p)),
        out_shape=jax.ShapeDtypeStruct((t, B_HEADS * B_V), BF16),
        scratch_shapes=[pltpu.VMEM((hps // 2, nq, LANES, tq), BF16)] + _flash_scratch(hps, B_V, tq, tq),
        compiler_params=_params("parallel", "parallel", "arbitrary"),
        name="mla_attention",
    )(q, k, v)
    return _outproj_ln(o, w_o.astype(BF16), h, ln_g, ln_b, alpha)


def _gelu_tanh(x):
    return x * (0.5 * (1.0 + jnp.tanh(math.sqrt(2.0 / math.pi) * (x + 0.044715 * (x * x * x)))))


def _compress_kernel(x_ref, pe_ref, w1_ref, w2_ref, o_ref):
    n = x_ref.shape[0] // C_CMP_STRIDE
    a = b = None
    for l in range(C_CMP_STRIDE):
        y = x_ref[pl.ds(l, n, stride=C_CMP_STRIDE), :]
        ta = _dot((y + pe_ref[l:l + 1, :]).astype(BF16), w1_ref[l])
        tb = _dot((y + pe_ref[C_CMP_STRIDE + l:C_CMP_STRIDE + l + 1, :]).astype(BF16), w1_ref[C_CMP_STRIDE + l])
        a = ta if a is None else a + ta
        b = tb if b is None else b + tb
    hid = a + pltpu.roll(b, n - 1, 0)
    o_ref[0] = _dot(_gelu_tanh(hid).astype(BF16), w2_ref[...]).astype(o_ref.dtype)


def _compress(kvc, which, pe, w1, w2, batch, seq):
    n = seq // C_CMP_STRIDE
    hp = LANES // HEAD_DIM
    pairs = C_KV_HEADS // hp
    eye = jnp.eye(hp, dtype=F32)
    w1bd = jnp.einsum("lij,gh->lgihj", w1.reshape(C_CMP_LEN, HEAD_DIM, C_CMP_HIDDEN), eye)
    w1bd = w1bd.reshape(C_CMP_LEN, LANES, hp * C_CMP_HIDDEN).astype(BF16)
    w2bd = jnp.einsum("ij,gh->gihj", jnp.concatenate([w2, w2], axis=1), eye)
    w2bd = w2bd.reshape(hp * C_CMP_HIDDEN, hp * LANES).astype(BF16)
    return pl.pallas_call(
        _compress_kernel,
        grid=(batch, pairs),
        in_specs=[pl.BlockSpec((seq, LANES), lambda i, j: (i, which * pairs + j)),
                  pl.BlockSpec((C_CMP_LEN, LANES), lambda i, j: (0, 0)),
                  pl.BlockSpec(w1bd.shape, lambda i, j: (0, 0, 0)),
                  pl.BlockSpec(w2bd.shape, lambda i, j: (0, 0))],
        out_specs=pl.BlockSpec((1, n, hp * LANES), lambda i, j: (i, 0, j)),
        out_shape=jax.ShapeDtypeStruct((batch, n, C_KV_HEADS * LANES), BF16),
        compiler_params=_params("parallel", "parallel"),
        name="nsa_compress",
    )(kvc, jnp.tile(pe, (1, hp)), w1bd, w2bd)


def _nsa_attn_kernel(q_ref, ks_ref, vs_ref, kw_ref, vw_ref, kc_ref, vc_ref, ov_ref, ex_ref, gt_ref,
                     o_ref, vst_ref, vwt_ref, vct_ref, *state, tq, tk, seq):
    qi = pl.program_id(2)
    q0 = qi * tq
    group = C_HEADS // C_KV_HEADS
    n_sel_blocks = seq // C_SEL_LEN
    nc = kc_ref.shape[1]
    n_groups = ks_ref.shape[1] // LANES
    blk_of = lambda gg: slice(gg * LANES, (gg + 1) * LANES)

    @pl.when(qi == 0)
    def _():
        for gg in range(n_groups):
            _transpose_chunks(vs_ref.at[:, blk_of(gg)], vst_ref.at[gg], tk)
            _transpose_chunks(vw_ref.at[:, blk_of(gg)], vwt_ref.at[gg], tq)
            vct_ref[gg] = vc_ref[0, :, blk_of(gg)].astype(F32).T.astype(vct_ref.dtype)

    top = lax.broadcasted_iota(jnp.int32, (LANES, tq), 0) < HEAD_DIM
    rep = lambda a: jnp.concatenate([a] * group, axis=1)

    def masked_softmax(s, valid):
        s = jnp.where(valid, s, NEG)
        e = jnp.exp(s - jnp.maximum(jnp.max(s, 0, keepdims=True), MAX_FLOOR))
        return e / jnp.maximum(jnp.sum(e, 0, keepdims=True), 1e-30)

    t_c = q0 + lax.broadcasted_iota(jnp.int32, (nc, tq), 1)
    n_c = lax.broadcasted_iota(jnp.int32, (nc, tq), 0)
    c_valid = rep(n_c * C_CMP_STRIDE + (C_CMP_LEN - 1) <= t_c)
    blk = lax.broadcasted_iota(jnp.int32, (n_sel_blocks, tq), 0)
    cur = (q0 + lax.broadcasted_iota(jnp.int32, (n_sel_blocks, tq), 1)) // C_SEL_LEN
    span = C_WINDOW + tq
    start = pl.multiple_of(jnp.maximum(q0 - C_WINDOW, 0), tq)
    t_w = q0 + lax.broadcasted_iota(jnp.int32, (span, tq), 1)
    dist = t_w - (start + lax.broadcasted_iota(jnp.int32, (span, tq), 0))
    w_valid = rep((dist >= 0) & (dist < C_WINDOW))

    qs_aug, o_cmp, o_win = [], [], []
    for gg in range(n_groups):
        qa_t = q_ref[:, blk_of(2 * gg)].astype(F32).T
        qb_t = q_ref[:, blk_of(2 * gg + 1)].astype(F32).T
        qs_t = jnp.concatenate([jnp.where(top, qa_t, 0.0), jnp.where(top, 0.0, qa_t),
                                jnp.where(top, qb_t, 0.0), jnp.where(top, 0.0, qb_t)], axis=1).astype(BF16)

        p_cmp = masked_softmax(_dot(kc_ref[0, :, blk_of(gg)], qs_t), c_valid).astype(BF16)
        o_cmp.append(_dot(vct_ref[gg], p_cmp))
        imp4 = _dot(ov_ref[...], p_cmp)
        imp = imp4[:n_sel_blocks, :tq]
        for r in range(1, group):
            imp = imp + imp4[:n_sel_blocks, r * tq:(r + 1) * tq]

        imp = jnp.where((blk == 0) | (blk == cur) | (blk == cur - 1), jnp.inf, imp)
        imp = jnp.where(blk <= cur, imp, -jnp.inf)
        rank = jnp.zeros((n_sel_blocks, tq), F32)
        for i in range(n_sel_blocks):
            r_i = imp[i:i + 1, :]
            rank = rank + jnp.where(blk > i, jnp.where(r_i >= imp, 1.0, 0.0), jnp.where(r_i > imp, 1.0, 0.0))
        sel = jnp.where(rank < min(C_N_SEL, n_sel_blocks), 0.0, NEG)
        sel = jnp.concatenate([sel, jnp.zeros((LANES - n_sel_blocks, tq), F32)], axis=0)
        qs_aug.append(jnp.concatenate([qs_t, rep(sel).astype(BF16)], axis=0))

        p_win = masked_softmax(_dot(kw_ref[pl.ds(start, span), blk_of(gg)], qs_t), w_valid).astype(BF16)
        acc_w = _dot(vwt_ref[gg, start // tq], p_win[:tq])
        for cb in range(1, span // tq):
            acc_w = acc_w + _dot(vwt_ref[gg, start // tq + cb], p_win[cb * tq:(cb + 1) * tq])
        o_win.append(acc_w)

    t_k = q0 + lax.broadcasted_iota(jnp.int32, (tk, tq), 1)
    k_k = lax.broadcasted_iota(jnp.int32, (tk, tq), 0)

    def sel_scores(gg, c):
        base = pl.multiple_of(c * tk, tk)
        k_aug = jnp.concatenate([ks_ref[pl.ds(base, tk), blk_of(gg)], ex_ref[c]], axis=1)
        return jnp.where(rep(base + k_k <= t_k), _dot(k_aug, qs_aug[gg]), NEG)

    res = _flash_causal_t(sel_scores, lambda gg, c: vst_ref[gg, c], n_groups, (q0 + tq - 1) // tk, None, state)

    for gg in range(n_groups):
        l, acc = res[gg]
        o_slc = acc / jnp.maximum(l, 1e-30)
        g_t = gt_ref[:, blk_of(gg)].T
        outs = []
        for r in range(group):
            cols = slice(r * tq, (r + 1) * tq)
            outs.append(o_cmp[gg][:, cols] * g_t[3 * r:3 * r + 1] + o_slc[:, cols] * g_t[3 * r + 1:3 * r + 2]
                        + o_win[gg][:, cols] * g_t[3 * r + 2:3 * r + 3])
        o_ref[:, blk_of(2 * gg)] = jnp.where(top, outs[0], outs[1]).T.astype(o_ref.dtype)
        o_ref[:, blk_of(2 * gg + 1)] = jnp.where(top, outs[2], outs[3]).T.astype(o_ref.dtype)


def _mixer_c(h, w_in, pos_k, pos_v, wk1, wk2, wv1, wv2, w_o, ln_g, ln_b, alpha, batch, seq):
    t, d = h.shape
    qd, kd = C_HEADS * HEAD_DIM, C_KV_HEADS * HEAD_DIM
    group = C_HEADS // C_KV_HEADS
    kv = lambda i: w_in[:, qd + i * kd:qd + (i + 1) * kd]
    wg = jnp.pad(w_in[:, qd + 6 * kd:].reshape(d, C_KV_HEADS, group * 3),
                 ((0, 0), (0, 0), (0, LANES - group * 3))).reshape(d, C_KV_HEADS * LANES)
    w = jnp.concatenate([w_in[:, :qd], _dup_cols(kv(2), C_KV_HEADS), _dup_cols(kv(4), C_KV_HEADS),
                         _dup_cols(kv(3), C_KV_HEADS), _dup_cols(kv(5), C_KV_HEADS),
                         kv(0), kv(1), wg], axis=1).astype(BF16)
    tab = _rope_tables(seq, ROT_DIM, HEAD_DIM, 0)
    kw2 = 2 * kd
    segs = [(0, qd, "rope", HEAD_DIM ** -0.5), (qd, 2 * kw2, "rope", 1.0), (qd + 2 * kw2, 2 * kw2, "plain", 1.0),
            (qd + 4 * kw2, 2 * kd, "plain", 1.0), (qd + 4 * kw2 + 2 * kd, C_KV_HEADS * LANES, "sigmoid", 1.0)]
    q, ksw, vsw, kvc, gates = _proj(h, w, tab, segs, [BF16, BF16, BF16, F32, F32], seq, ROT_DIM // 2)
    k_cmp = _compress(kvc, 0, pos_k, wk1, wk2, batch, seq)
    v_cmp = _compress(kvc, 1, pos_v, wv1, wv2, batch, seq)

    nc = seq // C_CMP_STRIDE
    nsb = seq // C_SEL_LEN
    tq = 128
    tk = min(512, seq)
    nq = seq // tq
    cs = np.arange(nc)[None, :] * C_CMP_STRIDE
    ss = np.arange(LANES)[:, None] * C_SEL_LEN
    overlap = ((cs <= ss + C_SEL_LEN - 1) & (ss <= cs + C_CMP_LEN - 1) & (np.arange(LANES)[:, None] < nsb))
    overlap = jnp.asarray(overlap, BF16)
    key_blk = (np.arange(seq) // C_SEL_LEN).reshape(seq // tk, tk, 1)
    expand = jnp.asarray(key_blk == np.arange(LANES)[None, None, :], BF16)

    gps = 2
    steps = C_KV_HEADS // gps
    per_bg = lambda off: pl.BlockSpec((seq, gps * LANES), lambda b, g, i, off=off: (b, off + g))
    o = pl.pallas_call(
        functools.partial(_nsa_attn_kernel, tq=tq, tk=tk, seq=seq),
        grid=(batch, steps, nq),
        in_specs=[pl.BlockSpec((tq, gps * 2 * LANES), lambda b, g, i: (b * nq + i, g)),
                  per_bg(0), per_bg(0), per_bg(steps), per_bg(steps),
                  pl.BlockSpec((1, nc, gps * LANES), lambda b, g, i: (b, 0, g)),
                  pl.BlockSpec((1, nc, gps * LANES), lambda b, g, i: (b, 0, g)),
                  pl.BlockSpec((LANES, nc), lambda b, g, i: (0, 0)),
                  pl.BlockSpec((seq // tk, tk, LANES), lambda b, g, i: (0, 0, 0)),
                  pl.BlockSpec((tq, gps * LANES), lambda b, g, i: (b * nq + i, g))],
        out_specs=pl.BlockSpec((tq, gps * 2 * LANES), lambda b, g, i: (b * nq + i, g)),
        out_shape=jax.ShapeDtypeStruct((t, qd), BF16),
        scratch_shapes=[pltpu.VMEM((gps, seq // tk, LANES, tk), BF16), pltpu.VMEM((gps, seq // tq, LANES, tq), BF16),
                        pltpu.VMEM((gps, LANES, nc), BF16)] + _flash_scratch(gps, LANES, group * tq, tk),
        compiler_params=_params("parallel", "parallel", "arbitrary"),
        name="nsa_attention",
    )(q, ksw, vsw, ksw, vsw, k_cmp, v_cmp, overlap, expand, gates)
    return _outproj_ln(o, w_o.astype(BF16), h, ln_g, ln_b, alpha)


def _diff_attn_kernel(q_ref, k_ref, v_ref, lam_ref, sub_ref, o_ref, vt_ref, *state, tq, lam_init):
    qi = pl.program_id(2)

    n_heads = q_ref.shape[1] // LANES

    @pl.when(qi == 0)
    def _():
        for hd in range(n_heads):
            _transpose_chunks(v_ref.at[:, hd * LANES:(hd + 1) * LANES], vt_ref.at[hd], tq)

    lam = (jnp.exp(jnp.sum(lam_ref[0:1, :] * lam_ref[1:2, :], -1, keepdims=True))
           - jnp.exp(jnp.sum(lam_ref[2:3, :] * lam_ref[3:4, :], -1, keepdims=True)) + lam_init)
    top = lax.broadcasted_iota(jnp.int32, (LANES, tq), 0) < D_SUB
    qs_t = []
    for hd in range(n_heads):
        q_t = q_ref[:, hd * LANES:(hd + 1) * LANES].astype(F32).T
        qs_t.append(jnp.concatenate([jnp.where(top, q_t, 0.0), jnp.where(top, 0.0, q_t)], axis=1).astype(BF16))
    key = lax.broadcasted_iota(jnp.int32, (tq, tq), 0)
    qry = lax.broadcasted_iota(jnp.int32, (tq, tq), 1)
    causal = jnp.concatenate([key <= qry] * 2, axis=1)

    def scores(hd, c):
        return _dot(k_ref[pl.ds(pl.multiple_of(c * tq, tq), tq), hd * LANES:(hd + 1) * LANES], qs_t[hd])

    def values(hd, c):
        return vt_ref[hd, c]

    res = _flash_causal_t(scores, values, n_heads, qi, causal, state)
    for hd in range(n_heads):
        l, acc = res[hd]
        o = (acc / l).T
        o = o[:tq] - lam * o[tq:]
        o = o * lax.rsqrt(jnp.mean(o * o, -1, keepdims=True) + RMS_EPS) * sub_ref[...]
        o_ref[:, hd * LANES:(hd + 1) * LANES] = (o * (1.0 - lam_init)).astype(o_ref.dtype)


def _mixer_d(h, w_in, lq1, lk1, lq2, lk2, subln, w_o, ln_g, ln_b, alpha, layer_idx, batch, seq):
    t, d = h.shape
    qd = D_HEADS * 2 * D_SUB
    tab = _rope_tables(seq, ROT_DIM, D_SUB, 0)
    segs = [(0, qd, "rope", D_SUB ** -0.5), (qd, qd, "rope", 1.0), (2 * qd, qd, "plain", 1.0)]
    q, k, v = _proj(h, w_in.astype(BF16), tab, segs, [BF16, BF16, BF16], seq, ROT_DIM // 2)
    lam_init = 0.8 - 0.6 * math.exp(-0.3 * layer_idx)
    lam_in = jnp.stack([lq1, lk1, lq2, lk2]).astype(F32)
    tq = min(FLASH_ROWS, seq)
    nq = seq // tq
    hps = 2
    o = pl.pallas_call(
        functools.partial(_diff_attn_kernel, tq=tq, lam_init=lam_init),
        grid=(batch, D_HEADS // hps, nq),
        in_specs=[pl.BlockSpec((tq, hps * LANES), lambda b, hd, i: (b * nq + i, hd)),
                  pl.BlockSpec((seq, hps * LANES), lambda b, hd, i: (b, hd)),
                  pl.BlockSpec((seq, hps * LANES), lambda b, hd, i: (b, hd)),
                  pl.BlockSpec((4, D_SUB), lambda b, hd, i: (0, 0)),
                  pl.BlockSpec((1, 2 * D_SUB), lambda b, hd, i: (0, 0))],
        out_specs=pl.BlockSpec((tq, hps * LANES), lambda b, hd, i: (b * nq + i, hd)),
        out_shape=jax.ShapeDtypeStruct((t, qd), BF16),
        scratch_shapes=[pltpu.VMEM((hps, nq, LANES, tq), BF16)] + _flash_scratch(hps, LANES, 2 * tq, tq),
        compiler_params=_params("parallel", "parallel", "arbitrary"),
        name="diff_attention",
    )(q, k, v, lam_in, subln.reshape(1, -1))
    return _outproj_ln(o, w_o.astype(BF16), h, ln_g, ln_b, alpha)


def _router_kernel(x_ref, w_ref, o_ref):
    xb = x_ref[...].astype(BF16)
    logits = _dot(xb, w_ref[...])
    lg = logits[:, :LANES].T[:8]
    le = logits[:, LANES:].T[:M_EXPERTS]
    far = 4 * LANES

    def softmax(x, valid):
        m = jnp.max(jnp.where(valid, x, NEG), 0, keepdims=True)
        e = jnp.where(valid, jnp.exp(x - m), 0.0)
        return e / jnp.sum(e, 0, keepdims=True)

    def first_max(p, valid, row):
        top = jnp.max(jnp.where(valid, p, -1.0), 0, keepdims=True)
        idx = jnp.min(jnp.where(valid & (p == top), row, far), 0, keepdims=True)
        return top, idx

    g_row = lax.broadcasted_iota(jnp.int32, lg.shape, 0)
    e_row = lax.broadcasted_iota(jnp.int32, le.shape, 0)
    g_valid = g_row < M_GROUPS
    g_w, g_idx = first_max(softmax(lg, g_valid), g_valid, g_row)
    e_valid = (e_row >= g_idx * M_PER_GROUP) & (e_row < (g_idx + 1) * M_PER_GROUP)
    pe = softmax(le, e_valid)
    w0, i0 = first_max(pe, e_valid, e_row)
    w1, i1 = first_max(pe, e_valid & (e_row != i0), e_row)
    tot = w0 + w1
    out = jnp.where(g_row == 0, i0.astype(F32), 0.0)
    out = jnp.where(g_row == 1, i1.astype(F32), out)
    out = jnp.where(g_row == 2, g_w * w0 / tot, out)
    out = jnp.where(g_row == 3, g_w * w1 / tot, out)
    o_ref[...] = out


def _row_gather(src_hbm, idx_ref, buf, sem):
    for r in range(buf.shape[0]):
        pltpu.make_async_copy(src_hbm.at[pl.ds(idx_ref[0, 0, r], 1), :], buf.at[pl.ds(r, 1), :], sem).start()


def _row_gather_wait(src_hbm, buf, sem):
    pltpu.make_async_copy(src_hbm.at[pl.ds(0, buf.shape[0]), :], buf, sem).wait()


def _row_scatter(buf, idx_ref, dst_hbm, sem):
    for r in range(buf.shape[0]):
        pltpu.make_async_copy(buf.at[pl.ds(r, 1), :], dst_hbm.at[pl.ds(idx_ref[0, 0, r], 1), :], sem).start()


def _row_scatter_wait(buf, dst_hbm, sem):
    pltpu.make_async_copy(buf, dst_hbm.at[pl.ds(0, buf.shape[0]), :], sem).wait()


def _moe_kernel(eid_ref, used_ref, src0_ref, src_ref, dstp_ref, h_hbm, wg_ref, wu_ref, wd_ref, out_hbm,
                xbuf0, xbuf1, ybuf0, ybuf1, wgb, wub, wdb, gsem, ssem, *, tm):
    t = pl.program_id(0)
    used = used_ref[0]
    xbuf = (xbuf0, xbuf1)
    ybuf = (ybuf0, ybuf1)

    @pl.when(t == 0)
    def _():
        _row_gather(h_hbm, src0_ref, xbuf0, gsem.at[0])
        ybuf1[...] = jnp.zeros(ybuf1.shape, F32)
        fill = pltpu.make_async_copy(ybuf1, out_hbm.at[pl.ds(out_hbm.shape[0] - 2 * tm, tm), :], ssem.at[0])
        fill.start()
        fill.wait()

    @pl.when((t < used) & ((t == 0) | (eid_ref[t] != eid_ref[jnp.maximum(t - 1, 0)])))
    def _():
        wgb[...] = wg_ref[0, 0].astype(BF16)
        wub[...] = wu_ref[0, 0].astype(BF16)
        wdb[...] = wd_ref[0, 0].astype(BF16)

    for s in range(2):
        @pl.when((t < used) & (t % 2 == s))
        def _(s=s):
            _row_gather_wait(h_hbm, xbuf[s], gsem.at[s])

            @pl.when(t >= 1)
            def _():
                _row_scatter_wait(ybuf[s], out_hbm, ssem.at[s])

            _row_scatter(ybuf[1 - s], dstp_ref, out_hbm, ssem.at[1 - s])
            _row_gather(h_hbm, src_ref, xbuf[1 - s], gsem.at[1 - s])
            xb = xbuf[s][...].astype(BF16)
            gate = _dot(xb, wgb[...])
            up = _dot(xb, wub[...])
            hid = (gate * (1.0 / (1.0 + jnp.exp(-gate))) * up).astype(BF16)
            ybuf[s][...] = _dot(hid, wdb[...])

        @pl.when((t == used) & (t % 2 == s))
        def _(s=s):
            _row_scatter(ybuf[1 - s], dstp_ref, out_hbm, ssem.at[1 - s])
            _row_scatter_wait(ybuf[s], out_hbm, ssem.at[s])
            _row_scatter_wait(ybuf[1 - s], out_hbm, ssem.at[1 - s])
            _row_gather_wait(h_hbm, xbuf[s], gsem.at[s])


def _combine_ln_kernel(h_ref, y0_ref, y1_ref, w_ref, g_ref, b_ref, o_ref, *, alpha):
    w = w_ref[...]
    y = w[:, 0:1] * y0_ref[...] + w[:, 1:2] * y1_ref[...]
    o_ref[...] = _layer_norm(alpha * h_ref[...] + y, g_ref[...], b_ref[...])


def _hier_moe_ln(h, w_group, w_expert, w_gate, w_up, w_down, layer, ln_g, ln_b, alpha):
    t, d = h.shape
    tm = MOE_ROWS
    wr = jnp.concatenate([jnp.pad(w_group, ((0, 0), (0, LANES - M_GROUPS))),
                          jnp.pad(w_expert, ((0, 0), (0, LANES - M_EXPERTS)))], axis=1).astype(BF16)
    routed = pl.pallas_call(
        _router_kernel,
        grid=(t // LN_ROWS,),
        in_specs=[pl.BlockSpec((LN_ROWS, d), lambda i: (i, 0)),
                  pl.BlockSpec((d, 2 * LANES), lambda i: (0, 0))],
        out_specs=pl.BlockSpec((8, LN_ROWS), lambda i: (0, i)),
        out_shape=jax.ShapeDtypeStruct((8, t), F32),
        compiler_params=_params("parallel"),
        name="moe_router",
    )(h, wr)

    n_rows = t * M_TOPK
    n_tiles = n_rows // tm + M_EXPERTS
    expert = routed[:M_TOPK].T.astype(jnp.int32).reshape(-1)
    weight = routed[M_TOPK:2 * M_TOPK].T
    order = jnp.argsort(expert).astype(jnp.int32)
    experts = jnp.arange(M_EXPERTS, dtype=jnp.int32)[None, :]
    sizes = jnp.sum(expert[:, None] == experts, axis=0, dtype=jnp.int32)
    tiles = (sizes + tm - 1) // tm
    tile_end = jnp.cumsum(tiles)
    seg_start = jnp.cumsum(sizes) - sizes
    used = tile_end[-1]
    tile_ids = jnp.arange(n_tiles, dtype=jnp.int32)
    tile_eid = jnp.sum(jnp.minimum(tile_ids, used - 1)[:, None] >= tile_end[None, :], axis=1).astype(jnp.int32)
    tile_first = (tile_end - tiles)[tile_eid]
    lane_r = jnp.arange(tm, dtype=jnp.int32)[None, :]
    offs = (tile_ids - tile_first)[:, None] * tm + lane_r
    valid = (offs < sizes[tile_eid][:, None]) & (tile_ids < used)[:, None]
    row = order[jnp.clip(seg_start[tile_eid][:, None] + offs, 0, n_rows - 1)]
    src = jnp.where(valid, row // M_TOPK, 0).reshape(n_tiles, 1, tm)
    trash = n_rows + (tile_ids % 2)[:, None] * tm + lane_r
    dst = jnp.where(valid, (row % M_TOPK) * t + row // M_TOPK, trash)
    dstp = jnp.concatenate([n_rows + tm + lane_r, dst], axis=0).reshape(n_tiles + 1, 1, tm)

    wspec = lambda shape: pl.BlockSpec((1, 1) + shape, lambda i, eid, used: (layer, eid[i], 0, 0))
    idx_spec = pl.BlockSpec((1, 1, tm), lambda i, eid, used: (i, 0, 0), memory_space=pltpu.SMEM)
    nxt_spec = pl.BlockSpec((1, 1, tm), lambda i, eid, used: (jnp.minimum(i + 1, n_tiles - 1), 0, 0),
                            memory_space=pltpu.SMEM)
    y = pl.pallas_call(
        functools.partial(_moe_kernel, tm=tm),
        grid_spec=pltpu.PrefetchScalarGridSpec(
            num_scalar_prefetch=2,
            grid=(n_tiles,),
            in_specs=[idx_spec, nxt_spec, idx_spec,
                      pl.BlockSpec(memory_space=pl.ANY),
                      wspec((d, M_HIDDEN)), wspec((d, M_HIDDEN)), wspec((M_HIDDEN, d))],
            out_specs=pl.BlockSpec(memory_space=pl.ANY),
            scratch_shapes=[pltpu.VMEM((tm, d), F32), pltpu.VMEM((tm, d), F32),
                            pltpu.VMEM((tm, d), F32), pltpu.VMEM((tm, d), F32),
                            pltpu.VMEM((d, M_HIDDEN), BF16), pltpu.VMEM((d, M_HIDDEN), BF16),
                            pltpu.VMEM((M_HIDDEN, d), BF16),
                            pltpu.SemaphoreType.DMA((2,)), pltpu.SemaphoreType.DMA((2,))]),
        out_shape=jax.ShapeDtypeStruct((n_rows + 2 * tm, d), F32),
        compiler_params=_params("arbitrary"),
        name="moe_experts",
    )(tile_eid, used.reshape(1).astype(jnp.int32), src, src, dstp, h, w_gate, w_up, w_down)

    tc = LN_ROWS
    nt = t // tc
    return pl.pallas_call(
        functools.partial(_combine_ln_kernel, alpha=alpha),
        grid=(nt,),
        in_specs=[pl.BlockSpec((tc, d), lambda i: (i, 0)),
                  pl.BlockSpec((tc, d), lambda i: (i, 0)),
                  pl.BlockSpec((tc, d), lambda i: (nt + i, 0)),
                  pl.BlockSpec((tc, M_TOPK), lambda i: (i, 0)),
                  pl.BlockSpec((1, d), lambda i: (0, 0)),
                  pl.BlockSpec((1, d), lambda i: (0, 0))],
        out_specs=pl.BlockSpec((tc, d), lambda i: (i, 0)),
        out_shape=jax.ShapeDtypeStruct((t, d), F32),
        compiler_params=_params("parallel"),
        name="moe_combine_ln",
    )(h, y, y, weight, ln_g.reshape(1, d), ln_b.reshape(1, d))


def kernel(x, a_w_in, a_sinks, a_w_o, b_w_down, b_q_norm, b_kv_norm, b_w_uq, b_w_ukv, b_w_o, c_w_in, c_pos_k, c_pos_v, c_wk1, c_wk2, c_wv1, c_wv2, c_w_o, d_w_in, d_lq1, d_lk1, d_lq2, d_lk2, d_subln, d_w_o, moe_w_group, moe_w_expert, moe_w_gate, moe_w_up, moe_w_down, ln_g, ln_b):
    batch, seq, d = x.shape
    depth = ln_g.shape[0]
    alpha = (2 * depth) ** 0.25
    h = x.reshape(batch * seq, d)
    for i in range(depth):
        kind, j = i % N_MIXERS, i // N_MIXERS
        g, b = ln_g[i, 0], ln_b[i, 0]
        if kind == 0:
            h = _mixer_a(h, a_w_in[j], a_sinks[j], a_w_o[j], g, b, alpha, batch, seq)
        elif kind == 1:
            h = _mixer_b(h, b_w_down[j], b_q_norm[j], b_kv_norm[j], b_w_uq[j], b_w_ukv[j], b_w_o[j],
                         g, b, alpha, batch, seq)
        elif kind == 2:
            h = _mixer_c(h, c_w_in[j], c_pos_k[j], c_pos_v[j], c_wk1[j], c_wk2[j], c_wv1[j], c_wv2[j],
                         c_w_o[j], g, b, alpha, batch, seq)
        else:
            h = _mixer_d(h, d_w_in[j], d_lq1[j], d_lk1[j], d_lq2[j], d_lk2[j], d_subln[j], d_w_o[j],
                         g, b, alpha, i, batch, seq)
        h = _hier_moe_ln(h, moe_w_group[i], moe_w_expert[i], moe_w_gate, moe_w_up, moe_w_down, i,
                         ln_g[i, 1], ln_b[i, 1], alpha)
    return h.reshape(batch, seq, d)
```

```python
import functools
import math

import numpy as np
import jax
import jax.numpy as jnp
from jax import lax
from jax.experimental import pallas as pl
from jax.experimental.pallas import tpu as pltpu

F32 = jnp.float32
BF16 = jnp.bfloat16

HEAD_DIM = 64
ROPE_THETA = 500000.0
ROT_DIM = HEAD_DIM // 4
A_HEADS, A_KV_HEADS, A_WINDOW = 16, 4, 128
B_HEADS, B_Q_RANK, B_KV_RANK, B_NOPE, B_ROPE, B_V = 16, 384, 256, 64, 32, 64
C_HEADS, C_KV_HEADS = 16, 4
C_CMP_LEN, C_CMP_STRIDE, C_CMP_HIDDEN = 32, 16, 128
C_SEL_LEN, C_N_SEL, C_WINDOW = 64, 16, 512
D_HEADS, D_SUB = 8, 64
M_GROUPS, M_PER_GROUP, M_TOPK, M_HIDDEN = 4, 8, 2, 512
M_EXPERTS = M_GROUPS * M_PER_GROUP
N_MIXERS = 4
LN_EPS = 1e-5
RMS_EPS = 1e-6

LANES = 128
NEG = -1e30
MAX_FLOOR = -1e20
VMEM_LIMIT = 56 * 1024 * 1024

PROJ_ROWS = 512
LN_ROWS = 512
MOE_ROWS = 256
FLASH_ROWS = 512


def _params(*sem):
    return pltpu.CompilerParams(dimension_semantics=sem, vmem_limit_bytes=VMEM_LIMIT)


def _dot(a, b):
    return jnp.dot(a, b, preferred_element_type=F32)


def _dot_t(a, b):
    return lax.dot_general(a, b, (((1,), (1,)), ((), ())), preferred_element_type=F32)


def _rope_tables(seq, rot_dim, period, off):
    half = rot_dim // 2
    inv_freq = 1.0 / (ROPE_THETA ** (jnp.arange(half, dtype=F32) * (2.0 / rot_dim)))
    ang = jnp.arange(seq, dtype=F32)[:, None] * inv_freq[None, :]
    cos, sin = jnp.cos(ang), jnp.sin(ang)
    lane = np.arange(LANES) % period - off
    first = (lane >= 0) & (lane < half)
    second = (lane >= half) & (lane < rot_dim)
    idx = np.where(first, lane, np.where(second, lane - half, 0))
    cg, sg = cos[:, idx], sin[:, idx]
    c = jnp.where(first | second, cg, 1.0)
    sa = jnp.where(first, -sg, 0.0)
    sb = jnp.where(second, sg, 0.0)
    return jnp.stack([c, sa, sb])


def _rope_block(x, tab_ref, half):
    return (x * tab_ref[0] + pltpu.roll(x, LANES - half, 1) * tab_ref[1]
            + pltpu.roll(x, half, 1) * tab_ref[2])


def _proj_kernel(x_ref, w_ref, tab_ref, *out_refs, segs, half):
    xb = x_ref[...].astype(BF16)
    for (start, width, kind, scale), o_ref in zip(segs, out_refs):
        acc = _dot(xb, w_ref[:, start:start + width])
        if kind == "rope":
            for c in range(width // LANES):
                y = _rope_block(acc[:, c * LANES:(c + 1) * LANES], tab_ref, half)
                if scale != 1.0:
                    y = y * scale
                o_ref[:, c * LANES:(c + 1) * LANES] = y.astype(o_ref.dtype)
        elif kind == "sigmoid":
            o_ref[...] = (1.0 / (1.0 + jnp.exp(-acc))).astype(o_ref.dtype)
        else:
            o_ref[...] = acc.astype(o_ref.dtype)


def _proj(x, w, tab, segs, dtypes, seq, half):
    t, k = x.shape
    n = w.shape[1]
    tm = PROJ_ROWS
    spb = seq // tm
    out_shape = [jax.ShapeDtypeStruct((t, s[1]), d) for s, d in zip(segs, dtypes)]
    return pl.pallas_call(
        functools.partial(_proj_kernel, segs=tuple(segs), half=half),
        grid=(t // tm,),
        in_specs=[pl.BlockSpec((tm, k), lambda i: (i, 0)),
                  pl.BlockSpec((k, n), lambda i: (0, 0)),
                  pl.BlockSpec((3, tm, LANES), lambda i: (0, i % spb, 0))],
        out_specs=[pl.BlockSpec((tm, s[1]), lambda i: (i, 0)) for s in segs],
        out_shape=out_shape,
        compiler_params=_params("parallel"),
        name="proj",
    )(x, w, tab)


def _layer_norm(z, g, b):
    mu = jnp.mean(z, -1, keepdims=True)
    zc = z - mu
    var = jnp.mean(zc * zc, -1, keepdims=True)
    return zc * lax.rsqrt(var + LN_EPS) * g + b


def _outproj_ln_kernel(o_ref, w_ref, h_ref, g_ref, b_ref, out_ref, *, alpha):
    y = _dot(o_ref[...], w_ref[...])
    out_ref[...] = _layer_norm(alpha * h_ref[...] + y, g_ref[...], b_ref[...])


def _outproj_ln(o, w, h, g, b, alpha):
    t, k = o.shape
    d = w.shape[1]
    tm = LN_ROWS
    return pl.pallas_call(
        functools.partial(_outproj_ln_kernel, alpha=alpha),
        grid=(t // tm,),
        in_specs=[pl.BlockSpec((tm, k), lambda i: (i, 0)),
                  pl.BlockSpec((k, d), lambda i: (0, 0)),
                  pl.BlockSpec((tm, d), lambda i: (i, 0)),
                  pl.BlockSpec((1, d), lambda i: (0, 0)),
                  pl.BlockSpec((1, d), lambda i: (0, 0))],
        out_specs=pl.BlockSpec((tm, d), lambda i: (i, 0)),
        out_shape=jax.ShapeDtypeStruct((t, d), F32),
        compiler_params=_params("parallel"),
        name="outproj_ln",
    )(o, w, h, g.reshape(1, d), b.reshape(1, d))


def _stack_group_queries(q_ref, rows):
    lo = lax.broadcasted_iota(jnp.int32, (rows, LANES), 1) < HEAD_DIM
    qa, qb = q_ref[:, :LANES], q_ref[:, LANES:]
    z = jnp.zeros_like(qa)
    return jnp.concatenate([jnp.where(lo, qa, z), jnp.where(lo, z, qa),
                            jnp.where(lo, qb, z), jnp.where(lo, z, qb)], axis=0)


def _pair(lo_val, hi_val):
    lo = lax.broadcasted_iota(jnp.int32, lo_val.shape, 1) < HEAD_DIM
    return jnp.where(lo, lo_val, hi_val)


def _transpose_chunks(src_ref, dst_ref, chunk):
    for c in range(src_ref.shape[0] // chunk):
        dst_ref[c] = src_ref[c * chunk:(c + 1) * chunk, :].astype(F32).T.astype(dst_ref.dtype)


def _softmax_update_t(s, valid, m, l, scale=None):
    if valid is not None:
        s = jnp.where(valid, s, NEG)
    m_new = jnp.maximum(m, jnp.max(s, 0, keepdims=True))
    if scale is None:
        a = jnp.exp(m - m_new)
        p = jnp.exp(s - m_new)
    else:
        a = jnp.exp((m - m_new) * scale)
        p = jnp.exp((s - m_new) * scale)
    return m_new, a * l + jnp.sum(p, 0, keepdims=True), a, p.astype(BF16)


def _flash_scratch(n_chains, dv, m_cols, tk):
    return [pltpu.VMEM((n_chains, 3, 1, m_cols), F32), pltpu.VMEM((n_chains, dv, m_cols), F32),
            pltpu.VMEM((n_chains, 2, tk, m_cols), F32), pltpu.VMEM((n_chains, 2, tk, m_cols), BF16)]


def _flash_causal_t(score_fn, value_fn, n_chains, qi, causal, state, scale=None):
    st_ref, acc_ref, s_ref, p_ref = state
    for ch in range(n_chains):
        st_ref[ch, 0] = jnp.full(st_ref.shape[2:], MAX_FLOOR, F32)
        st_ref[ch, 1] = jnp.zeros(st_ref.shape[2:], F32)
        st_ref[ch, 2] = jnp.ones(st_ref.shape[2:], F32)
        acc_ref[ch] = jnp.zeros(acc_ref.shape[1:], F32)
        p_ref[ch, 1] = jnp.zeros(p_ref.shape[2:], BF16)
        s_ref[ch, 0] = score_fn(ch, 0)

    def half(c, cur, valid, last):
        nxt = 1 - cur
        for ch in range(n_chains):
            m, l, a, p = _softmax_update_t(s_ref[ch, cur], valid, st_ref[ch, 0], st_ref[ch, 1], scale)
            if not last:
                s_ref[ch, nxt] = score_fn(ch, c + 1)
            acc = st_ref[ch, 2] * acc_ref[ch] + _dot(value_fn(ch, jnp.maximum(c - 1, 0)), p_ref[ch, nxt])
            if last:
                acc = a * acc + _dot(value_fn(ch, c), p)
            else:
                p_ref[ch, cur] = p
                st_ref[ch, 2] = a
            acc_ref[ch] = acc
            st_ref[ch, 0] = m
            st_ref[ch, 1] = l

    def pair(j, carry):
        half(2 * j, 0, None, False)
        half(2 * j + 1, 1, None, False)
        return carry

    lax.fori_loop(0, qi // 2, pair, 0)

    @pl.when(qi % 2 == 1)
    def _():
        half(qi - 1, 0, None, False)
        half(qi, 1, causal, True)

    @pl.when(qi % 2 == 0)
    def _():
        half(qi, 0, causal, True)

    return [(st_ref[ch, 1], acc_ref[ch]) for ch in range(n_chains)]


def _swa_kernel(sink_ref, q_ref, kp_ref, kc_ref, vp_ref, vc_ref, o_ref):
    n = pl.program_id(1)
    blk = A_WINDOW
    i = lax.broadcasted_iota(jnp.int32, (blk, 2 * blk), 0)
    j = lax.broadcasted_iota(jnp.int32, (blk, 2 * blk), 1)
    dist = blk + i - j
    valid = (dist >= 0) & (dist < A_WINDOW) & ((n - 1) * blk + j >= 0)
    group = A_HEADS // A_KV_HEADS
    for g in range(A_KV_HEADS):
        qs = _stack_group_queries(q_ref.at[:, g * 2 * LANES:(g + 1) * 2 * LANES], blk)
        k = jnp.concatenate([kp_ref[:, g * LANES:(g + 1) * LANES],
                             kc_ref[:, g * LANES:(g + 1) * LANES]], axis=0)
        v = jnp.concatenate([vp_ref[:, g * LANES:(g + 1) * LANES],
                             vc_ref[:, g * LANES:(g + 1) * LANES]], axis=0)
        s = _dot_t(qs, k)
        ps = []
        for r in range(group):
            sink = sink_ref[g * group + r]
            sr = jnp.where(valid, s[r * blk:(r + 1) * blk], NEG)
            m = jnp.maximum(jnp.max(sr, -1, keepdims=True), sink)
            e = jnp.where(valid, jnp.exp(sr - m), 0.0)
            p = e / (jnp.sum(e, -1, keepdims=True) + jnp.exp(sink - m))
            ps.append(p.astype(BF16))
        o = _dot(jnp.concatenate(ps, axis=0), v)
        o_ref[:, g * 2 * LANES:g * 2 * LANES + LANES] = _pair(o[:blk], o[blk:2 * blk]).astype(o_ref.dtype)
        o_ref[:, g * 2 * LANES + LANES:(g + 1) * 2 * LANES] = _pair(
            o[2 * blk:3 * blk], o[3 * blk:]).astype(o_ref.dtype)


def _swa_attention(q, kd, vd, sinks, batch, seq):
    t = q.shape[0]
    blk = A_WINDOW
    nb = seq // blk
    qd = A_HEADS * HEAD_DIM
    kw = A_KV_HEADS * LANES
    cur = lambda b, n, s: (b * nb + n, 0)
    prev = lambda b, n, s: (b * nb + jnp.maximum(n - 1, 0), 0)
    return pl.pallas_call(
        _swa_kernel,
        grid_spec=pltpu.PrefetchScalarGridSpec(
            num_scalar_prefetch=1,
            grid=(batch, nb),
            in_specs=[pl.BlockSpec((blk, qd), cur),
                      pl.BlockSpec((blk, kw), prev), pl.BlockSpec((blk, kw), cur),
                      pl.BlockSpec((blk, kw), prev), pl.BlockSpec((blk, kw), cur)],
            out_specs=pl.BlockSpec((blk, qd), cur)),
        out_shape=jax.ShapeDtypeStruct((t, qd), BF16),
        compiler_params=_params("parallel", "parallel"),
        name="swa_attention",
    )(sinks.astype(F32), q, kd, kd, vd, vd)


def _dup_cols(w, heads):
    k = w.shape[0]
    w4 = w.reshape(k, heads, 1, HEAD_DIM)
    return jnp.broadcast_to(w4, (k, heads, 2, HEAD_DIM)).reshape(k, heads * 2 * HEAD_DIM)


def _mixer_a(h, w_in, sinks, w_o, ln_g, ln_b, alpha, batch, seq):
    qd, kd = A_HEADS * HEAD_DIM, A_KV_HEADS * HEAD_DIM
    w = jnp.concatenate([w_in[:, :qd], _dup_cols(w_in[:, qd:qd + kd], A_KV_HEADS),
                         _dup_cols(w_in[:, qd + kd:], A_KV_HEADS)], axis=1).astype(BF16)
    tab = _rope_tables(seq, ROT_DIM, HEAD_DIM, 0)
    segs = [(0, qd, "rope", HEAD_DIM ** -0.5), (qd, 2 * kd, "rope", 1.0), (qd + 2 * kd, 2 * kd, "plain", 1.0)]
    q, k2, v2 = _proj(h, w, tab, segs, [BF16, BF16, BF16], seq, ROT_DIM // 2)
    o = _swa_attention(q, k2, v2, sinks, batch, seq)
    return _outproj_ln(o, w_o.astype(BF16), h, ln_g, ln_b, alpha)


def _mla_proj_kernel(x_ref, wd_ref, qn_ref, kvn_ref, wq_ref, wk_ref, wv_ref, tab_ref,
                     q_ref, k_ref, v_ref):
    half = B_ROPE // 2
    c = _dot(x_ref[...].astype(BF16), wd_ref[...])
    cq, ckv = c[:, :B_Q_RANK], c[:, B_Q_RANK:B_Q_RANK + B_KV_RANK]
    kr = _rope_block(c[:, B_Q_RANK + B_KV_RANK:], tab_ref, half)
    cqn = (cq * lax.rsqrt(jnp.mean(cq * cq, -1, keepdims=True) + RMS_EPS) * qn_ref[...]).astype(BF16)
    ckvn = (ckv * lax.rsqrt(jnp.mean(ckv * ckv, -1, keepdims=True) + RMS_EPS) * kvn_ref[...]).astype(BF16)
    q = _dot(cqn, wq_ref[...])
    kk = _dot(ckvn, wk_ref[...])
    for hd in range(B_HEADS):
        sl = slice(hd * LANES, (hd + 1) * LANES)
        q_ref[:, sl] = _rope_block(q[:, sl], tab_ref, half).astype(q_ref.dtype)
        k_ref[:, sl] = (kk[:, sl] + kr).astype(k_ref.dtype)
    v_ref[...] = _dot(ckvn, wv_ref[...]).astype(v_ref.dtype)


def _mla_attn_kernel(q_ref, k_ref, v_ref, o_ref, vt_ref, *state, tq, scale):
    qi = pl.program_id(2)
    n_heads = q_ref.shape[1] // LANES

    @pl.when(qi == 0)
    def _():
        for pr in range(n_heads // 2):
            _transpose_chunks(v_ref.at[:, pr * LANES:(pr + 1) * LANES], vt_ref.at[pr], tq)

    q_t = [q_ref[:, hh * LANES:(hh + 1) * LANES].astype(F32).T.astype(BF16) for hh in range(n_heads)]
    key = lax.broadcasted_iota(jnp.int32, (tq, tq), 0)
    qry = lax.broadcasted_iota(jnp.int32, (tq, tq), 1)
    causal = key <= qry

    def scores(hh, c):
        return _dot(k_ref[pl.ds(pl.multiple_of(c * tq, tq), tq), hh * LANES:(hh + 1) * LANES], q_t[hh])

    def values(hh, c):
        return vt_ref[hh // 2, c, (hh % 2) * B_V:(hh % 2 + 1) * B_V, :]

    res = _flash_causal_t(scores, values, n_heads, qi, causal, state, scale)
    for pr in range(n_heads // 2):
        o_t = jnp.concatenate([acc / l for l, acc in res[2 * pr:2 * pr + 2]], axis=0)
        o_ref[:, pr * LANES:(pr + 1) * LANES] = o_t.T.astype(o_ref.dtype)


def _mixer_b(h, w_down, q_norm, kv_norm, w_uq, w_ukv, w_o, ln_g, ln_b, alpha, batch, seq):
    t, d = h.shape
    dq = B_NOPE + B_ROPE
    pad = LANES - dq
    wd = jnp.concatenate([w_down[:, :B_Q_RANK + B_KV_RANK], jnp.zeros((d, B_NOPE), F32),
                          w_down[:, B_Q_RANK + B_KV_RANK:], jnp.zeros((d, pad), F32)], axis=1).astype(BF16)
    wq = jnp.pad(w_uq.reshape(B_Q_RANK, B_HEADS, dq), ((0, 0), (0, 0), (0, pad))
                 ).reshape(B_Q_RANK, B_HEADS * LANES).astype(BF16)
    wkv = w_ukv.reshape(B_KV_RANK, B_HEADS, B_NOPE + B_V)
    wk = jnp.pad(wkv[:, :, :B_NOPE], ((0, 0), (0, 0), (0, LANES - B_NOPE))
                 ).reshape(B_KV_RANK, B_HEADS * LANES).astype(BF16)
    wv = wkv[:, :, B_NOPE:].reshape(B_KV_RANK, B_HEADS * B_V).astype(BF16)
    tab = _rope_tables(seq, B_ROPE, LANES, B_NOPE)
    tm = PROJ_ROWS
    spb = seq // tm
    full = lambda a: pl.BlockSpec(a.shape, lambda i: (0,) * a.ndim)
    qn, kvn = q_norm.reshape(1, -1), kv_norm.reshape(1, -1)
    q, k, v = pl.pallas_call(
        _mla_proj_kernel,
        grid=(t // tm,),
        in_specs=[pl.BlockSpec((tm, d), lambda i: (i, 0)), full(wd), full(qn), full(kvn),
                  full(wq), full(wk), full(wv),
                  pl.BlockSpec((3, tm, LANES), lambda i: (0, i % spb, 0))],
        out_specs=[pl.BlockSpec((tm, B_HEADS * LANES), lambda i: (i, 0)),
                   pl.BlockSpec((tm, B_HEADS * LANES), lambda i: (i, 0)),
                   pl.BlockSpec((tm, B_HEADS * B_V), lambda i: (i, 0))],
        out_shape=[jax.ShapeDtypeStruct((t, B_HEADS * LANES), BF16),
                   jax.ShapeDtypeStruct((t, B_HEADS * LANES), BF16),
                   jax.ShapeDtypeStruct((t, B_HEADS * B_V), BF16)],
        compiler_params=_params("parallel"),
        name="mla_proj",
    )(h, wd, qn, kvn, wq, wk, wv, tab)

    tq = min(FLASH_ROWS, seq)
    nq = seq // tq
    hps = 4
    o = pl.pallas_call(
        functools.partial(_mla_attn_kernel, tq=tq, scale=dq ** -0.5),
        grid=(batch, B_HEADS // hps, nq),
        in_specs=[pl.BlockSpec((tq, hps * LANES), lambda b, p, i: (b * nq + i, p)),
                  pl.BlockSpec((seq, hps * LANES), lambda b, p, i: (b, p)),
                  pl.BlockSpec((seq, hps * B_V), lambda b, p, i: (b, p))],
        out_specs=pl.BlockSpec((tq, hps * B_V), lambda b, p, i: (b * nq + i, p)),
        out_shape=jax.ShapeDtypeStruct((t, B_HEADS * B_V), BF16),
        scratch_shapes=[pltpu.VMEM((hps // 2, nq, LANES, tq), BF16)] + _flash_scratch(hps, B_V, tq, tq),
        compiler_params=_params("parallel", "parallel", "arbitrary"),
        name="mla_attention",
    )(q, k, v)
    return _outproj_ln(o, w_o.astype(BF16), h, ln_g, ln_b, alpha)


def _gelu_tanh(x):
    return x * (0.5 * (1.0 + jnp.tanh(math.sqrt(2.0 / math.pi) * (x + 0.044715 * (x * x * x)))))


def _compress_kernel(x_ref, pe_ref, w1_ref, w2_ref, o_ref):
    n = x_ref.shape[0] // C_CMP_STRIDE
    a = b = None
    for l in range(C_CMP_STRIDE):
        y = x_ref[pl.ds(l, n, stride=C_CMP_STRIDE), :]
        ta = _dot((y + pe_ref[l:l + 1, :]).astype(BF16), w1_ref[l])
        tb = _dot((y + pe_ref[C_CMP_STRIDE + l:C_CMP_STRIDE + l + 1, :]).astype(BF16), w1_ref[C_CMP_STRIDE + l])
        a = ta if a is None else a + ta
        b = tb if b is None else b + tb
    hid = a + pltpu.roll(b, n - 1, 0)
    o_ref[0] = _dot(_gelu_tanh(hid).astype(BF16), w2_ref[...]).astype(o_ref.dtype)


def _compress(kvc, which, pe, w1, w2, batch, seq):
    n = seq // C_CMP_STRIDE
    hp = LANES // HEAD_DIM
    pairs = C_KV_HEADS // hp
    eye = jnp.eye(hp, dtype=F32)
    w1bd = jnp.einsum("lij,gh->lgihj", w1.reshape(C_CMP_LEN, HEAD_DIM, C_CMP_HIDDEN), eye)
    w1bd = w1bd.reshape(C_CMP_LEN, LANES, hp * C_CMP_HIDDEN).astype(BF16)
    w2bd = jnp.einsum("ij,gh->gihj", jnp.concatenate([w2, w2], axis=1), eye)
    w2bd = w2bd.reshape(hp * C_CMP_HIDDEN, hp * LANES).astype(BF16)
    return pl.pallas_call(
        _compress_kernel,
        grid=(batch, pairs),
        in_specs=[pl.BlockSpec((seq, LANES), lambda i, j: (i, which * pairs + j)),
                  pl.BlockSpec((C_CMP_LEN, LANES), lambda i, j: (0, 0)),
                  pl.BlockSpec(w1bd.shape, lambda i, j: (0, 0, 0)),
                  pl.BlockSpec(w2bd.shape, lambda i, j: (0, 0))],
        out_specs=pl.BlockSpec((1, n, hp * LANES), lambda i, j: (i, 0, j)),
        out_shape=jax.ShapeDtypeStruct((batch, n, C_KV_HEADS * LANES), BF16),
        compiler_params=_params("parallel", "parallel"),
        name="nsa_compress",
    )(kvc, jnp.tile(pe, (1, hp)), w1bd, w2bd)


def _nsa_attn_kernel(q_ref, ks_ref, vs_ref, kw_ref, vw_ref, kc_ref, vc_ref, ov_ref, ex_ref, gt_ref,
                     o_ref, vst_ref, vwt_ref, vct_ref, *state, tq, tk, seq):
    qi = pl.program_id(2)
    q0 = qi * tq
    group = C_HEADS // C_KV_HEADS
    n_sel_blocks = seq // C_SEL_LEN
    nc = kc_ref.shape[1]
    n_groups = ks_ref.shape[1] // LANES
    blk_of = lambda gg: slice(gg * LANES, (gg + 1) * LANES)

    @pl.when(qi == 0)
    def _():
        for gg in range(n_groups):
            _transpose_chunks(vs_ref.at[:, blk_of(gg)], vst_ref.at[gg], tk)
            _transpose_chunks(vw_ref.at[:, blk_of(gg)], vwt_ref.at[gg], tq)
            vct_ref[gg] = vc_ref[0, :, blk_of(gg)].astype(F32).T.astype(vct_ref.dtype)

    top = lax.broadcasted_iota(jnp.int32, (LANES, tq), 0) < HEAD_DIM
    rep = lambda a: jnp.concatenate([a] * group, axis=1)

    def masked_softmax(s, valid):
        s = jnp.where(valid, s, NEG)
        e = jnp.exp(s - jnp.maximum(jnp.max(s, 0, keepdims=True), MAX_FLOOR))
        return e / jnp.maximum(jnp.sum(e, 0, keepdims=True), 1e-30)

    t_c = q0 + lax.broadcasted_iota(jnp.int32, (nc, tq), 1)
    n_c = lax.broadcasted_iota(jnp.int32, (nc, tq), 0)
    c_valid = rep(n_c * C_CMP_STRIDE + (C_CMP_LEN - 1) <= t_c)
    blk = lax.broadcasted_iota(jnp.int32, (n_sel_blocks, tq), 0)
    cur = (q0 + lax.broadcasted_iota(jnp.int32, (n_sel_blocks, tq), 1)) // C_SEL_LEN
    span = C_WINDOW + tq
    start = pl.multiple_of(jnp.maximum(q0 - C_WINDOW, 0), tq)
    t_w = q0 + lax.broadcasted_iota(jnp.int32, (span, tq), 1)
    dist = t_w - (start + lax.broadcasted_iota(jnp.int32, (span, tq), 0))
    w_valid = rep((dist >= 0) & (dist < C_WINDOW))

    qs_aug, o_cmp, o_win = [], [], []
    for gg in range(n_groups):
        qa_t = q_ref[:, blk_of(2 * gg)].astype(F32).T
        qb_t = q_ref[:, blk_of(2 * gg + 1)].astype(F32).T
        qs_t = jnp.concatenate([jnp.where(top, qa_t, 0.0), jnp.where(top, 0.0, qa_t),
                                jnp.where(top, qb_t, 0.0), jnp.where(top, 0.0, qb_t)], axis=1).astype(BF16)

        p_cmp = masked_softmax(_dot(kc_ref[0, :, blk_of(gg)], qs_t), c_valid).astype(BF16)
        o_cmp.append(_dot(vct_ref[gg], p_cmp))
        imp4 = _dot(ov_ref[...], p_cmp)
        imp = imp4[:n_sel_blocks, :tq]
        for r in range(1, group):
            imp = imp + imp4[:n_sel_blocks, r * tq:(r + 1) * tq]

        imp = jnp.where((blk == 0) | (blk == cur) | (blk == cur - 1), jnp.inf, imp)
        imp = jnp.where(blk <= cur, imp, -jnp.inf)
        rank = jnp.zeros((n_sel_blocks, tq), F32)
        for i in range(n_sel_blocks):
            r_i = imp[i:i + 1, :]
            rank = rank + jnp.where(blk > i, jnp.where(r_i >= imp, 1.0, 0.0), jnp.where(r_i > imp, 1.0, 0.0))
        sel = jnp.where(rank < min(C_N_SEL, n_sel_blocks), 0.0, NEG)
        sel = jnp.concatenate([sel, jnp.zeros((LANES - n_sel_blocks, tq), F32)], axis=0)
        qs_aug.append(jnp.concatenate([qs_t, rep(sel).astype(BF16)], axis=0))

        p_win = masked_softmax(_dot(kw_ref[pl.ds(start, span), blk_of(gg)], qs_t), w_valid).astype(BF16)
        acc_w = _dot(vwt_ref[gg, start // tq], p_win[:tq])
        for cb in range(1, span // tq):
            acc_w = acc_w + _dot(vwt_ref[gg, start // tq + cb], p_win[cb * tq:(cb + 1) * tq])
        o_win.append(acc_w)

    t_k = q0 + lax.broadcasted_iota(jnp.int32, (tk, tq), 1)
    k_k = lax.broadcasted_iota(jnp.int32, (tk, tq), 0)

    def sel_scores(gg, c):
        base = pl.multiple_of(c * tk, tk)
        k_aug = jnp.concatenate([ks_ref[pl.ds(base, tk), blk_of(gg)], ex_ref[c]], axis=1)
        return jnp.where(rep(base + k_k <= t_k), _dot(k_aug, qs_aug[gg]), NEG)

    res = _flash_causal_t(sel_scores, lambda gg, c: vst_ref[gg, c], n_groups, (q0 + tq - 1) // tk, None, state)

    for gg in range(n_groups):
        l, acc = res[gg]
        o_slc = acc / jnp.maximum(l, 1e-30)
        g_t = gt_ref[:, blk_of(gg)].T
        outs = []
        for r in range(group):
            cols = slice(r * tq, (r + 1) * tq)
            outs.append(o_cmp[gg][:, cols] * g_t[3 * r:3 * r + 1] + o_slc[:, cols] * g_t[3 * r + 1:3 * r + 2]
                        + o_win[gg][:, cols] * g_t[3 * r + 2:3 * r + 3])
        o_ref[:, blk_of(2 * gg)] = jnp.where(top, outs[0], outs[1]).T.astype(o_ref.dtype)
        o_ref[:, blk_of(2 * gg + 1)] = jnp.where(top, outs[2], outs[3]).T.astype(o_ref.dtype)


def _mixer_c(h, w_in, pos_k, pos_v, wk1, wk2, wv1, wv2, w_o, ln_g, ln_b, alpha, batch, seq):
    t, d = h.shape
    qd, kd = C_HEADS * HEAD_DIM, C_KV_HEADS * HEAD_DIM
    group = C_HEADS // C_KV_HEADS
    kv = lambda i: w_in[:, qd + i * kd:qd + (i + 1) * kd]
    wg = jnp.pad(w_in[:, qd + 6 * kd:].reshape(d, C_KV_HEADS, group * 3),
                 ((0, 0), (0, 0), (0, LANES - group * 3))).reshape(d, C_KV_HEADS * LANES)
    w = jnp.concatenate([w_in[:, :qd], _dup_cols(kv(2), C_KV_HEADS), _dup_cols(kv(4), C_KV_HEADS),
                         _dup_cols(kv(3), C_KV_HEADS), _dup_cols(kv(5), C_KV_HEADS),
                         kv(0), kv(1), wg], axis=1).astype(BF16)
    tab = _rope_tables(seq, ROT_DIM, HEAD_DIM, 0)
    kw2 = 2 * kd
    segs = [(0, qd, "rope", HEAD_DIM ** -0.5), (qd, 2 * kw2, "rope", 1.0), (qd + 2 * kw2, 2 * kw2, "plain", 1.0),
            (qd + 4 * kw2, 2 * kd, "plain", 1.0), (qd + 4 * kw2 + 2 * kd, C_KV_HEADS * LANES, "sigmoid", 1.0)]
    q, ksw, vsw, kvc, gates = _proj(h, w, tab, segs, [BF16, BF16, BF16, F32, F32], seq, ROT_DIM // 2)
    k_cmp = _compress(kvc, 0, pos_k, wk1, wk2, batch, seq)
    v_cmp = _compress(kvc, 1, pos_v, wv1, wv2, batch, seq)

    nc = seq // C_CMP_STRIDE
    nsb = seq // C_SEL_LEN
    tq = 128
    tk = min(512, seq)
    nq = seq // tq
    cs = np.arange(nc)[None, :] * C_CMP_STRIDE
    ss = np.arange(LANES)[:, None] * C_SEL_LEN
    overlap = ((cs <= ss + C_SEL_LEN - 1) & (ss <= cs + C_CMP_LEN - 1) & (np.arange(LANES)[:, None] < nsb))
    overlap = jnp.asarray(overlap, BF16)
    key_blk = (np.arange(seq) // C_SEL_LEN).reshape(seq // tk, tk, 1)
    expand = jnp.asarray(key_blk == np.arange(LANES)[None, None, :], BF16)

    gps = 2
    steps = C_KV_HEADS // gps
    per_bg = lambda off: pl.BlockSpec((seq, gps * LANES), lambda b, g, i, off=off: (b, off + g))
    o = pl.pallas_call(
        functools.partial(_nsa_attn_kernel, tq=tq, tk=tk, seq=seq),
        grid=(batch, steps, nq),
        in_specs=[pl.BlockSpec((tq, gps * 2 * LANES), lambda b, g, i: (b * nq + i, g)),
                  per_bg(0), per_bg(0), per_bg(steps), per_bg(steps),
                  pl.BlockSpec((1, nc, gps * LANES), lambda b, g, i: (b, 0, g)),
                  pl.BlockSpec((1, nc, gps * LANES), lambda b, g, i: (b, 0, g)),
                  pl.BlockSpec((LANES, nc), lambda b, g, i: (0, 0)),
                  pl.BlockSpec((seq // tk, tk, LANES), lambda b, g, i: (0, 0, 0)),
                  pl.BlockSpec((tq, gps * LANES), lambda b, g, i: (b * nq + i, g))],
        out_specs=pl.BlockSpec((tq, gps * 2 * LANES), lambda b, g, i: (b * nq + i, g)),
        out_shape=jax.ShapeDtypeStruct((t, qd), BF16),
        scratch_shapes=[pltpu.VMEM((gps, seq // tk, LANES, tk), BF16), pltpu.VMEM((gps, seq // tq, LANES, tq), BF16),
                        pltpu.VMEM((gps, LANES, nc), BF16)] + _flash_scratch(gps, LANES, group * tq, tk),
        compiler_params=_params("parallel", "parallel", "arbitrary"),
        name="nsa_attention",
    )(q, ksw, vsw, ksw, vsw, k_cmp, v_cmp, overlap, expand, gates)
    return _outproj_ln(o, w_o.astype(BF16), h, ln_g, ln_b, alpha)


def _diff_attn_kernel(q_ref, k_ref, v_ref, lam_ref, sub_ref, o_ref, vt_ref, *state, tq, lam_init):
    qi = pl.program_id(2)

    n_heads = q_ref.shape[1] // LANES

    @pl.when(qi == 0)
    def _():
        for hd in range(n_heads):
            _transpose_chunks(v_ref.at[:, hd * LANES:(hd + 1) * LANES], vt_ref.at[hd], tq)

    lam = (jnp.exp(jnp.sum(lam_ref[0:1, :] * lam_ref[1:2, :], -1, keepdims=True))
           - jnp.exp(jnp.sum(lam_ref[2:3, :] * lam_ref[3:4, :], -1, keepdims=True)) + lam_init)
    top = lax.broadcasted_iota(jnp.int32, (LANES, tq), 0) < D_SUB
    qs_t = []
    for hd in range(n_heads):
        q_t = q_ref[:, hd * LANES:(hd + 1) * LANES].astype(F32).T
        qs_t.append(jnp.concatenate([jnp.where(top, q_t, 0.0), jnp.where(top, 0.0, q_t)], axis=1).astype(BF16))
    key = lax.broadcasted_iota(jnp.int32, (tq, tq), 0)
    qry = lax.broadcasted_iota(jnp.int32, (tq, tq), 1)
    causal = jnp.concatenate([key <= qry] * 2, axis=1)

    def scores(hd, c):
        return _dot(k_ref[pl.ds(pl.multiple_of(c * tq, tq), tq), hd * LANES:(hd + 1) * LANES], qs_t[hd])

    def values(hd, c):
        return vt_ref[hd, c]

    res = _flash_causal_t(scores, values, n_heads, qi, causal, state)
    for hd in range(n_heads):
        l, acc = res[hd]
        o = (acc / l).T
        o = o[:tq] - lam * o[tq:]
        o = o * lax.rsqrt(jnp.mean(o * o, -1, keepdims=True) + RMS_EPS) * sub_ref[...]
        o_ref[:, hd * LANES:(hd + 1) * LANES] = (o * (1.0 - lam_init)).astype(o_ref.dtype)


def _mixer_d(h, w_in, lq1, lk1, lq2, lk2, subln, w_o, ln_g, ln_b, alpha, layer_idx, batch, seq):
    t, d = h.shape
    qd = D_HEADS * 2 * D_SUB
    tab = _rope_tables(seq, ROT_DIM, D_SUB, 0)
    segs = [(0, qd, "rope", D_SUB ** -0.5), (qd, qd, "rope", 1.0), (2 * qd, qd, "plain", 1.0)]
    q, k, v = _proj(h, w_in.astype(BF16), tab, segs, [BF16, BF16, BF16], seq, ROT_DIM // 2)
    lam_init = 0.8 - 0.6 * math.exp(-0.3 * layer_idx)
    lam_in = jnp.stack([lq1, lk1, lq2, lk2]).astype(F32)
    tq = min(FLASH_ROWS, seq)
    nq = seq // tq
    hps = 4
    o = pl.pallas_call(
        functools.partial(_diff_attn_kernel, tq=tq, lam_init=lam_init),
        grid=(batch, D_HEADS // hps, nq),
        in_specs=[pl.BlockSpec((tq, hps * LANES), lambda b, hd, i: (b * nq + i, hd)),
                  pl.BlockSpec((seq, hps * LANES), lambda b, hd, i: (b, hd)),
                  pl.BlockSpec((seq, hps * LANES), lambda b, hd, i: (b, hd)),
                  pl.BlockSpec((4, D_SUB), lambda b, hd, i: (0, 0)),
                  pl.BlockSpec((1, 2 * D_SUB), lambda b, hd, i: (0, 0))],
        out_specs=pl.BlockSpec((tq, hps * LANES), lambda b, hd, i: (b * nq + i, hd)),
        out_shape=jax.ShapeDtypeStruct((t, qd), BF16),
        scratch_shapes=[pltpu.VMEM((hps, nq, LANES, tq), BF16)] + _flash_scratch(hps, LANES, 2 * tq, tq),
        compiler_params=_params("parallel", "parallel", "arbitrary"),
        name="diff_attention",
    )(q, k, v, lam_in, subln.reshape(1, -1))
    return _outproj_ln(o, w_o.astype(BF16), h, ln_g, ln_b, alpha)


def _router_kernel(x_ref, w_ref, o_ref):
    xb = x_ref[...].astype(BF16)
    logits = _dot(xb, w_ref[...])
    lg = logits[:, :LANES].T[:8]
    le = logits[:, LANES:].T[:M_EXPERTS]
    far = 4 * LANES

    def softmax(x, valid):
        m = jnp.max(jnp.where(valid, x, NEG), 0, keepdims=True)
        e = jnp.where(valid, jnp.exp(x - m), 0.0)
        return e / jnp.sum(e, 0, keepdims=True)

    def first_max(p, valid, row):
        top = jnp.max(jnp.where(valid, p, -1.0), 0, keepdims=True)
        idx = jnp.min(jnp.where(valid & (p == top), row, far), 0, keepdims=True)
        return top, idx

    g_row = lax.broadcasted_iota(jnp.int32, lg.shape, 0)
    e_row = lax.broadcasted_iota(jnp.int32, le.shape, 0)
    g_valid = g_row < M_GROUPS
    g_w, g_idx = first_max(softmax(lg, g_valid), g_valid, g_row)
    e_valid = (e_row >= g_idx * M_PER_GROUP) & (e_row < (g_idx + 1) * M_PER_GROUP)
    pe = softmax(le, e_valid)
    w0, i0 = first_max(pe, e_valid, e_row)
    w1, i1 = first_max(pe, e_valid & (e_row != i0), e_row)
    tot = w0 + w1
    out = jnp.where(g_row == 0, i0.astype(F32), 0.0)
    out = jnp.where(g_row == 1, i1.astype(F32), out)
    out = jnp.where(g_row == 2, g_w * w0 / tot, out)
    out = jnp.where(g_row == 3, g_w * w1 / tot, out)
    o_ref[...] = out


def _row_gather(src_hbm, idx_ref, buf, sem):
    for r in range(buf.shape[0]):
        pltpu.make_async_copy(src_hbm.at[pl.ds(idx_ref[0, 0, r], 1), :], buf.at[pl.ds(r, 1), :], sem).start()


def _row_gather_wait(src_hbm, buf, sem):
    pltpu.make_async_copy(src_hbm.at[pl.ds(0, buf.shape[0]), :], buf, sem).wait()


def _row_scatter(buf, idx_ref, dst_hbm, sem):
    for r in range(buf.shape[0]):
        pltpu.make_async_copy(buf.at[pl.ds(r, 1), :], dst_hbm.at[pl.ds(idx_ref[0, 0, r], 1), :], sem).start()


def _row_scatter_wait(buf, dst_hbm, sem):
    pltpu.make_async_copy(buf, dst_hbm.at[pl.ds(0, buf.shape[0]), :], sem).wait()


def _moe_kernel(eid_ref, used_ref, src0_ref, src_ref, dstp_ref, h_hbm, wg_ref, wu_ref, wd_ref, out_hbm,
                xbuf0, xbuf1, ybuf0, ybuf1, wgb, wub, wdb, gsem, ssem, *, tm):
    t = pl.program_id(0)
    used = used_ref[0]
    xbuf = (xbuf0, xbuf1)
    ybuf = (ybuf0, ybuf1)

    @pl.when(t == 0)
    def _():
        _row_gather(h_hbm, src0_ref, xbuf0, gsem.at[0])
        ybuf1[...] = jnp.zeros(ybuf1.shape, F32)
        fill = pltpu.make_async_copy(ybuf1, out_hbm.at[pl.ds(out_hbm.shape[0] - 2 * tm, tm), :], ssem.at[0])
        fill.start()
        fill.wait()

    @pl.when((t < used) & ((t == 0) | (eid_ref[t] != eid_ref[jnp.maximum(t - 1, 0)])))
    def _():
        wgb[...] = wg_ref[0, 0].astype(BF16)
        wub[...] = wu_ref[0, 0].astype(BF16)
        wdb[...] = wd_ref[0, 0].astype(BF16)

    for s in range(2):
        @pl.when((t < used) & (t % 2 == s))
        def _(s=s):
            _row_gather_wait(h_hbm, xbuf[s], gsem.at[s])

            @pl.when(t >= 1)
            def _():
                _row_scatter_wait(ybuf[s], out_hbm, ssem.at[s])

            _row_scatter(ybuf[1 - s], dstp_ref, out_hbm, ssem.at[1 - s])
            _row_gather(h_hbm, src_ref, xbuf[1 - s], gsem.at[1 - s])
            xb = xbuf[s][...].astype(BF16)
            gate = _dot(xb, wgb[...])
            up = _dot(xb, wub[...])
            hid = (gate * (1.0 / (1.0 + jnp.exp(-gate))) * up).astype(BF16)
            ybuf[s][...] = _dot(hid, wdb[...])

        @pl.when((t == used) & (t % 2 == s))
        def _(s=s):
            _row_scatter(ybuf[1 - s], dstp_ref, out_hbm, ssem.at[1 - s])
            _row_scatter_wait(ybuf[s], out_hbm, ssem.at[s])
            _row_scatter_wait(ybuf[1 - s], out_hbm, ssem.at[1 - s])
            _row_gather_wait(h_hbm, xbuf[s], gsem.at[s])


def _combine_ln_kernel(h_ref, y0_ref, y1_ref, w_ref, g_ref, b_ref, o_ref, *, alpha):
    w = w_ref[...]
    y = w[:, 0:1] * y0_ref[...] + w[:, 1:2] * y1_ref[...]
    o_ref[...] = _layer_norm(alpha * h_ref[...] + y, g_ref[...], b_ref[...])


def _hier_moe_ln(h, w_group, w_expert, w_gate, w_up, w_down, layer, ln_g, ln_b, alpha):
    t, d = h.shape
    tm = MOE_ROWS
    wr = jnp.concatenate([jnp.pad(w_group, ((0, 0), (0, LANES - M_GROUPS))),
                          jnp.pad(w_expert, ((0, 0), (0, LANES - M_EXPERTS)))], axis=1).astype(BF16)
    routed = pl.pallas_call(
        _router_kernel,
        grid=(t // LN_ROWS,),
        in_specs=[pl.BlockSpec((LN_ROWS, d), lambda i: (i, 0)),
                  pl.BlockSpec((d, 2 * LANES), lambda i: (0, 0))],
        out_specs=pl.BlockSpec((8, LN_ROWS), lambda i: (0, i)),
        out_shape=jax.ShapeDtypeStruct((8, t), F32),
        compiler_params=_params("parallel"),
        name="moe_router",
    )(h, wr)

    n_rows = t * M_TOPK
    n_tiles = n_rows // tm + M_EXPERTS
    expert = routed[:M_TOPK].T.astype(jnp.int32).reshape(-1)
    weight = routed[M_TOPK:2 * M_TOPK].T
    order = jnp.argsort(expert).astype(jnp.int32)
    experts = jnp.arange(M_EXPERTS, dtype=jnp.int32)[None, :]
    sizes = jnp.sum(expert[:, None] == experts, axis=0, dtype=jnp.int32)
    tiles = (sizes + tm - 1) // tm
    tile_end = jnp.cumsum(tiles)
    seg_start = jnp.cumsum(sizes) - sizes
    used = tile_end[-1]
    tile_ids = jnp.arange(n_tiles, dtype=jnp.int32)
    tile_eid = jnp.sum(jnp.minimum(tile_ids, used - 1)[:, None] >= tile_end[None, :], axis=1).astype(jnp.int32)
    tile_first = (tile_end - tiles)[tile_eid]
    lane_r = jnp.arange(tm, dtype=jnp.int32)[None, :]
    offs = (tile_ids - tile_first)[:, None] * tm + lane_r
    valid = (offs < sizes[tile_eid][:, None]) & (tile_ids < used)[:, None]
    row = order[jnp.clip(seg_start[tile_eid][:, None] + offs, 0, n_rows - 1)]
    src = jnp.where(valid, row // M_TOPK, 0).reshape(n_tiles, 1, tm)
    trash = n_rows + (tile_ids % 2)[:, None] * tm + lane_r
    dst = jnp.where(valid, (row % M_TOPK) * t + row // M_TOPK, trash)
    dstp = jnp.concatenate([n_rows + tm + lane_r, dst], axis=0).reshape(n_tiles + 1, 1, tm)

    wspec = lambda shape: pl.BlockSpec((1, 1) + shape, lambda i, eid, used: (layer, eid[i], 0, 0))
    idx_spec = pl.BlockSpec((1, 1, tm), lambda i, eid, used: (i, 0, 0), memory_space=pltpu.SMEM)
    nxt_spec = pl.BlockSpec((1, 1, tm), lambda i, eid, used: (jnp.minimum(i + 1, n_tiles - 1), 0, 0),
                            memory_space=pltpu.SMEM)
    y = pl.pallas_call(
        functools.partial(_moe_kernel, tm=tm),
        grid_spec=pltpu.PrefetchScalarGridSpec(
            num_scalar_prefetch=2,
            grid=(n_tiles,),
            in_specs=[idx_spec, nxt_spec, idx_spec,
                      pl.BlockSpec(memory_space=pl.ANY),
                      wspec((d, M_HIDDEN)), wspec((d, M_HIDDEN)), wspec((M_HIDDEN, d))],
            out_specs=pl.BlockSpec(memory_space=pl.ANY),
            scratch_shapes=[pltpu.VMEM((tm, d), F32), pltpu.VMEM((tm, d), F32),
                            pltpu.VMEM((tm, d), F32), pltpu.VMEM((tm, d), F32),
                            pltpu.VMEM((d, M_HIDDEN), BF16), pltpu.VMEM((d, M_HIDDEN), BF16),
                            pltpu.VMEM((M_HIDDEN, d), BF16),
                            pltpu.SemaphoreType.DMA((2,)), pltpu.SemaphoreType.DMA((2,))]),
        out_shape=jax.ShapeDtypeStruct((n_rows + 2 * tm, d), F32),
        compiler_params=_params("arbitrary"),
        name="moe_experts",
    )(tile_eid, used.reshape(1).astype(jnp.int32), src, src, dstp, h, w_gate, w_up, w_down)

    tc = LN_ROWS
    nt = t // tc
    return pl.pallas_call(
        functools.partial(_combine_ln_kernel, alpha=alpha),
        grid=(nt,),
        in_specs=[pl.BlockSpec((tc, d), lambda i: (i, 0)),
                  pl.BlockSpec((tc, d), lambda i: (i, 0)),
                  pl.BlockSpec((tc, d), lambda i: (nt + i, 0)),
                  pl.BlockSpec((tc, M_TOPK), lambda i: (i, 0)),
                  pl.BlockSpec((1, d), lambda i: (0, 0)),
                  pl.BlockSpec((1, d), lambda i: (0, 0))],
        out_specs=pl.BlockSpec((tc, d), lambda i: (i, 0)),
        out_shape=jax.ShapeDtypeStruct((t, d), F32),
        compiler_params=_params("parallel"),
        name="moe_combine_ln",
    )(h, y, y, weight, ln_g.reshape(1, d), ln_b.reshape(1, d))


def kernel(x, a_w_in, a_sinks, a_w_o, b_w_down, b_q_norm, b_kv_norm, b_w_uq, b_w_ukv, b_w_o, c_w_in, c_pos_k, c_pos_v, c_wk1, c_wk2, c_wv1, c_wv2, c_w_o, d_w_in, d_lq1, d_lk1, d_lq2, d_lk2, d_subln, d_w_o, moe_w_group, moe_w_expert, moe_w_gate, moe_w_up, moe_w_down, ln_g, ln_b):
    batch, seq, d = x.shape
    depth = ln_g.shape[0]
    alpha = (2 * depth) ** 0.25
    h = x.reshape(batch * seq, d)
    for i in range(depth):
        kind, j = i % N_MIXERS, i // N_MIXERS
        g, b = ln_g[i, 0], ln_b[i, 0]
        if kind == 0:
            h = _mixer_a(h, a_w_in[j], a_sinks[j], a_w_o[j], g, b, alpha, batch, seq)
        elif kind == 1:
            h = _mixer_b(h, b_w_down[j], b_q_norm[j], b_kv_norm[j], b_w_uq[j], b_w_ukv[j], b_w_o[j],
                         g, b, alpha, batch, seq)
        elif kind == 2:
            h = _mixer_c(h, c_w_in[j], c_pos_k[j], c_pos_v[j], c_wk1[j], c_wk2[j], c_wv1[j], c_wv2[j],
                         c_w_o[j], g, b, alpha, batch, seq)
        else:
            h = _mixer_d(h, d_w_in[j], d_lq1[j], d_lk1[j], d_lq2[j], d_lk2[j], d_subln[j], d_w_o[j],
                         g, b, alpha, i, batch, seq)
        h = _hier_moe_ln(h, moe_w_group[i], moe_w_expert[i], moe_w_gate, moe_w_up, moe_w_down, i,
                         ln_g[i, 1], ln_b[i, 1], alpha)
    return h.reshape(batch, seq, d)
```
